```python
import jax, jax.numpy as jnp
from jax import lax
import numpy as np

D_MODEL = 1024
BATCH = 1
SEQ = 16384
DEPTH = 2
DEC_BATCH = 128
DEC_SEQ = 1
PAST_LEN = 16384
PAGE_SIZE = 128

MLA_HEADS = 8
MLA_NOPE = 64
MLA_ROPE = 32
MLA_V = 64
Q_LORA = 256
KV_LORA = 128
ROPE_BASE = 10000.0
MLA_SCALE = (MLA_NOPE + MLA_ROPE) ** -0.5
ATTN_BLOCK = 128
GLA_HEADS = 4
GLA_DK = 64
GLA_DV = 128
GLA_GATE_RANK = 16
GLA_GATE_TEMP = 16.0
GLA_CHUNK = 64
ML_HEADS = 4
ML_DK = 64
ML_DV = 128
ML_CHUNK = 64
PEER_HEADS = 8
PEER_KEYS = 128
PEER_EXPERTS = PEER_KEYS * PEER_KEYS
PEER_QDIM = 256
PEER_HALF = PEER_QDIM // 2
PEER_TOPK = 16
PEER_BLOCK = 128
NORM_EPS = 1e-6
BRANCH_WIDTH = MLA_HEADS * MLA_V

IN_SIZES = (Q_LORA, KV_LORA, MLA_ROPE,
            GLA_HEADS * GLA_DK, GLA_HEADS * GLA_DK, GLA_HEADS * GLA_DV, GLA_GATE_RANK, GLA_HEADS * GLA_DV,
            ML_HEADS * ML_DK, ML_HEADS * ML_DK, ML_HEADS * ML_DV, ML_HEADS, ML_HEADS, ML_HEADS * ML_DV,
            D_MODEL, D_MODEL, D_MODEL)
IN_WIDTH = sum(IN_SIZES)

kernel_name = 'hybrid_mla_gla_mlstm_peer_step'


def rms_norm(x, g):
    xf = x.astype(jnp.float32)
    y = xf * lax.rsqrt(jnp.mean(xf * xf, axis=-1, keepdims=True) + NORM_EPS)
    return y.astype(x.dtype) * g


def head_rms_norm(o, g):
    B, T, H, Dv = o.shape
    return rms_norm(o, g.reshape(H, Dv)).reshape(B, T, H * Dv)


def apply_rope(x, pos):
    half = x.shape[-1] // 2
    inv = ROPE_BASE ** (-jnp.arange(half, dtype=jnp.float32) / half)
    ang = pos.astype(jnp.float32)[:, None] * inv[None, :]
    cos = jnp.cos(ang)[None, :, None, :]
    sin = jnp.sin(ang)[None, :, None, :]
    xf = x.astype(jnp.float32)
    x1, x2 = xf[..., :half], xf[..., half:]
    return jnp.concatenate([x1 * cos - x2 * sin, x1 * sin + x2 * cos], axis=-1).astype(x.dtype)


def pad_time(x, L, value):
    pad = (-x.shape[1]) % L
    return jnp.pad(x, [(0, 0), (0, pad)] + [(0, 0)] * (x.ndim - 2), constant_values=value)


def to_chunks(x, L):
    B, Tp = x.shape[:2]
    return jnp.moveaxis(x.reshape((B, Tp // L, L) + x.shape[2:]), 1, 0)


def from_chunks(y, T):
    y = jnp.moveaxis(y, 0, 1)
    return y.reshape((y.shape[0], y.shape[1] * y.shape[2]) + y.shape[3:])[:, :T]


def mla_prompt(q_nope, q_pe, latent, k_rope, w_uk, w_uv):
    B, S = q_nope.shape[:2]
    k_nope = jnp.einsum('bsc,chd->bshd', latent, w_uk)
    v = jnp.einsum('bsc,chd->bshd', latent, w_uv)
    L = min(ATTN_BLOCK, S)
    n_blk = S // L
    qn = jnp.moveaxis(q_nope.reshape(B, n_blk, L, MLA_HEADS, MLA_NOPE), 1, 0)
    qr = jnp.moveaxis(q_pe.reshape(B, n_blk, L, MLA_HEADS, MLA_ROPE), 1, 0)
    key_pos = jnp.arange(S)

    def one_block(args):
        qn_b, qr_b, blk = args
        s = (jnp.einsum('bqhd,bkhd->bhqk', qn_b, k_nope)
             + jnp.einsum('bqhr,bkr->bhqk', qr_b, k_rope)).astype(jnp.float32) * MLA_SCALE
        q_pos = blk * L + jnp.arange(L)
        mask = key_pos[None, :] <= q_pos[:, None]
        p = jax.nn.softmax(jnp.where(mask[None, None], s, -jnp.inf), axis=-1)
        return jnp.einsum('bhqk,bkhd->bqhd', p.astype(v.dtype), v)

    o = lax.map(one_block, (qn, qr, jnp.arange(n_blk)))
    return jnp.moveaxis(o, 0, 1).reshape(B, S, MLA_HEADS * MLA_V)


def mla_sample(q_nope, q_pe, latent, k_rope, cache_lat, cache_rope, layer, page_table, w_uk, w_uv):
    B, T = q_nope.shape[:2]
    past = page_table.shape[1] * PAGE_SIZE
    q_lat = jnp.einsum('bthd,chd->bthc', q_nope, w_uk)
    key_pos = jnp.arange(past + T)
    q_pos = past + jnp.arange(T)
    mask = key_pos[None, :] <= q_pos[:, None]

    def one_seq(args):
        ql, qr, lat_new, kr_new, pages = args
        lat = jnp.concatenate([cache_lat[layer, pages].reshape(past, KV_LORA).astype(lat_new.dtype), lat_new], axis=0)
        kr = jnp.concatenate([cache_rope[layer, pages].reshape(past, MLA_ROPE).astype(kr_new.dtype), kr_new], axis=0)
        s = (jnp.einsum('thc,kc->htk', ql, lat)
             + jnp.einsum('thr,kr->htk', qr, kr)).astype(jnp.float32) * MLA_SCALE
        p = jax.nn.softmax(jnp.where(mask[None], s, -jnp.inf), axis=-1)
        return jnp.einsum('htk,kc->thc', p.astype(lat.dtype), lat)

    o_lat = lax.map(one_seq, (q_lat, q_pe, latent, k_rope, page_table))
    return jnp.einsum('bthc,chd->bthd', o_lat, w_uv).reshape(B, T, MLA_HEADS * MLA_V)


def gla_chunked(q, k, v, log_a, s0):
    T = q.shape[1]
    L = min(GLA_CHUNK, T)
    qc, kc, vc = (to_chunks(pad_time(t, L, 0.0), L) for t in (q, k, v))
    ac = to_chunks(pad_time(log_a, L, 0.0), L)
    causal = jnp.tril(jnp.ones((L, L), dtype=bool))

    def step(S, inp):
        qb, kb, vb, ab = inp
        qb, kb, vb = qb.astype(jnp.float32), kb.astype(jnp.float32), vb.astype(jnp.float32)
        b = jnp.cumsum(ab, axis=1)
        diff = b[:, :, None] - b[:, None, :]
        decay = jnp.exp(jnp.where(causal[None, :, :, None, None], diff, -jnp.inf))
        att = jnp.einsum('bthd,bshd,btshd->bhts', qb, kb, decay)
        o = jnp.einsum('bhts,bshv->bthv', att, vb) + jnp.einsum('bthd,bhdv->bthv', qb * jnp.exp(b), S)
        b_last = b[:, -1]
        S_new = jnp.exp(b_last)[..., None] * S + jnp.einsum('bshd,bshv->bhdv', kb * jnp.exp(b_last[:, None] - b), vb)
        return S_new, o

    S_fin, o = lax.scan(step, s0.astype(jnp.float32), (qc, kc, vc, ac))
    return from_chunks(o, T).astype(v.dtype), S_fin


def mlstm_chunked(q, k, v, i_pre, log_f, C0, n0, m0):
    T = q.shape[1]
    L = min(ML_CHUNK, T)
    qc, kc, vc = (to_chunks(pad_time(t, L, 0.0), L) for t in (q, k, v))
    ic = to_chunks(pad_time(i_pre, L, -jnp.inf), L)
    fc = to_chunks(pad_time(log_f, L, 0.0), L)
    causal = jnp.tril(jnp.ones((L, L), dtype=bool))

    def step(carry, inp):
        C, n, m = carry
        qb, kb, vb, ib, fb = inp
        qb, kb, vb = qb.astype(jnp.float32), kb.astype(jnp.float32), vb.astype(jnp.float32)
        b = jnp.cumsum(fb, axis=1)
        dlog = b[:, :, None, :] - b[:, None, :, :] + ib[:, None, :, :]
        dlog = jnp.where(causal[None, :, :, None], dlog, -jnp.inf)
        inter_log = b + m[:, None, :]
        m_t = jnp.maximum(inter_log, jnp.max(dlog, axis=2))
        w_intra = jnp.exp(dlog - m_t[:, :, None, :]) * jnp.einsum('bthd,bshd->btsh', qb, kb)
        w_inter = jnp.exp(inter_log - m_t)
        num = jnp.einsum('btsh,bshv->bthv', w_intra, vb) + w_inter[..., None] * jnp.einsum('bthd,bhdv->bthv', qb, C)
        nq = jnp.sum(w_intra, axis=2) + w_inter * jnp.einsum('bthd,bhd->bth', qb, n)
        h = num / jnp.maximum(jnp.abs(nq), jnp.exp(-m_t))[..., None]
        m_new = m_t[:, -1]
        keep = jnp.exp(b[:, -1] + m - m_new)
        w_new = jnp.exp(b[:, -1][:, None, :] - b + ib - m_new[:, None, :])
        C_new = keep[..., None, None] * C + jnp.einsum('bsh,bshd,bshv->bhdv', w_new, kb, vb)
        n_new = keep[..., None] * n + jnp.einsum('bsh,bshd->bhd', w_new, kb)
        return (C_new, n_new, m_new), h

    carry0 = (C0.astype(jnp.float32), n0.astype(jnp.float32), m0.astype(jnp.float32))
    (C_f, n_f, m_f), h = lax.scan(step, carry0, (qc, kc, vc, ic, fc))
    return from_chunks(h, T).astype(v.dtype), (C_f, n_f, m_f)


def mixer_block(h, pos, l, P, attn_fn, gla_s0, ml_C0, ml_n0, ml_m0):
    B, T, _ = h.shape
    z = jnp.einsum('btd,dw->btw', h, P['w_in'][l])
    (q_c, kv_c, k_pe, gq, gk, gv, ga, gg, mq, mk, mv, mi, mf, mo, g_a, g_b, g_c) = jnp.split(
        z, np.cumsum(IN_SIZES)[:-1].tolist(), axis=-1)
    q = jnp.einsum('btc,chd->bthd', rms_norm(q_c, P['mla_q_norm'][l]), P['w_uq'][l])
    q_nope = q[..., :MLA_NOPE]
    q_pe = apply_rope(q[..., MLA_NOPE:], pos)
    latent = rms_norm(kv_c, P['mla_kv_norm'][l])
    k_rope = apply_rope(k_pe[:, :, None, :], pos)[:, :, 0, :]
    a_out = attn_fn(l, q_nope, q_pe, latent, k_rope, P['w_uk'][l], P['w_uv'][l])
    log_a = jax.nn.log_sigmoid((jnp.einsum('btr,rk->btk', ga, P['w_gla_a'][l]) + P['b_gla_a'][l]).astype(jnp.float32)) / GLA_GATE_TEMP
    gla_o, gla_s = gla_chunked(gq.reshape(B, T, GLA_HEADS, GLA_DK) * GLA_DK ** -0.5,
                               gk.reshape(B, T, GLA_HEADS, GLA_DK),
                               gv.reshape(B, T, GLA_HEADS, GLA_DV),
                               log_a.reshape(B, T, GLA_HEADS, GLA_DK), gla_s0)
    b_out = head_rms_norm(gla_o, P['gla_norm'][l]) * jax.nn.silu(gg)
    i_pre = (mi + P['ml_b_i'][l]).astype(jnp.float32)
    log_f = jax.nn.log_sigmoid((mf + P['ml_b_f'][l]).astype(jnp.float32))
    ml_h, ml_state = mlstm_chunked(mq.reshape(B, T, ML_HEADS, ML_DK),
                                   mk.reshape(B, T, ML_HEADS, ML_DK) * ML_DK ** -0.5,
                                   mv.reshape(B, T, ML_HEADS, ML_DV), i_pre, log_f, ml_C0, ml_n0, ml_m0)
    c_out = head_rms_norm(ml_h, P['ml_norm'][l]) * jax.nn.sigmoid(mo)
    merged = (jax.nn.sigmoid(g_a) * (a_out @ P['w_proj_a'][l])
              + jax.nn.sigmoid(g_b) * (b_out @ P['w_proj_b'][l])
              + jax.nn.sigmoid(g_c) * (c_out @ P['w_proj_c'][l]))
    return merged @ P['w_o'][l], (latent, k_rope, gla_s) + ml_state


def peer_ffn(h, wq, sub_keys, u_tab, v_tab):
    B, T, D = h.shape
    n_tok = B * T
    pad = (-n_tok) % PEER_BLOCK
    tok = jnp.pad(h.reshape(n_tok, D), ((0, pad), (0, 0))).reshape(-1, PEER_BLOCK, D)
    keys = sub_keys.astype(jnp.float32)

    def one_block(xb):
        q = jnp.einsum('nd,dhq->nhq', xb, wq).astype(jnp.float32)
        s1 = jnp.einsum('nhc,hkc->nhk', q[..., :PEER_HALF], keys[:, 0])
        s2 = jnp.einsum('nhc,hkc->nhk', q[..., PEER_HALF:], keys[:, 1])
        v1, i1 = lax.top_k(s1, PEER_TOPK)
        v2, i2 = lax.top_k(s2, PEER_TOPK)
        cand = (v1[..., :, None] + v2[..., None, :]).reshape(xb.shape[0], PEER_HEADS, PEER_TOPK * PEER_TOPK)
        best, ci = lax.top_k(cand, PEER_TOPK)
        e = (jnp.take_along_axis(i1, ci // PEER_TOPK, axis=-1) * PEER_KEYS
             + jnp.take_along_axis(i2, ci % PEER_TOPK, axis=-1))
        g = jax.nn.softmax(best, axis=-1)
        act = jax.nn.gelu(jnp.einsum('nd,nhkd->nhk', xb, u_tab[e]).astype(jnp.float32))
        return jnp.einsum('nhk,nhkd->nd', (g * act).astype(xb.dtype), v_tab[e])

    out = lax.map(one_block, tok)
    return out.reshape(-1, D)[:n_tok].reshape(B, T, D)


def run_trunk(x, c, pos, attn_fn, init_state, P):
    lat_rows, rope_rows, gla_st, C_st, n_st, m_st = [], [], [], [], [], []
    for l in range(DEPTH):
        mod = jnp.einsum('bd,de->be', jax.nn.silu(c), P['w_ada'][l]) + P['b_ada'][l]
        shift1, scale1, gate1, shift2, scale2, gate2 = jnp.split(mod[:, None, :], 6, axis=-1)
        h = rms_norm(x, P['norm_mix'][l]) * (1.0 + scale1) + shift1
        mix, (lat, kr, s_gla, C, n, m) = mixer_block(h, pos, l, P, attn_fn, *init_state(l))
        x = x + gate1 * mix
        h = rms_norm(x, P['norm_ffn'][l]) * (1.0 + scale2) + shift2
        x = x + gate2 * peer_ffn(h, P['peer_wq'][l], P['peer_keys'][l], P['peer_u'][l], P['peer_v'][l])
        lat_rows.append(lat)
        rope_rows.append(kr)
        gla_st.append(s_gla)
        C_st.append(C)
        n_st.append(n)
        m_st.append(m)
    y = rms_norm(x, P['norm_final'])
    return y, (jnp.stack(lat_rows), jnp.stack(rope_rows), jnp.stack(gla_st),
               jnp.stack(C_st), jnp.stack(n_st), jnp.stack(m_st))


def setup_inputs(seed: int = 0) -> dict:
    key = jax.random.key(seed)
    ks = jax.random.split(key, 40)

    def nrm(i, shape, scale):
        return jax.random.normal(ks[i], shape, jnp.float32) * scale

    def gain(i, shape):
        return 1.0 + nrm(i, shape, 0.02)

    n_pages = PAST_LEN // PAGE_SIZE
    n_used = DEC_BATCH * n_pages
    n_pool = n_used + max(1, n_used // 4)
    page_table = jax.random.permutation(ks[8], n_pool)[:n_used].reshape(DEC_BATCH, n_pages).astype(jnp.int32)
    return {
        'x_prompt': nrm(0, (BATCH, SEQ, D_MODEL), 1.0),
        'x_sample': nrm(1, (DEC_BATCH, DEC_SEQ, D_MODEL), 1.0),
        'cache_kv_latent': nrm(2, (DEPTH, n_pool, PAGE_SIZE, KV_LORA), 1.0),
        'cache_k_rope': nrm(3, (DEPTH, n_pool, PAGE_SIZE, MLA_ROPE), 1.0),
        'state_gla': nrm(4, (DEPTH, DEC_BATCH, GLA_HEADS, GLA_DK, GLA_DV), 0.1),
        'state_mlstm_C': nrm(5, (DEPTH, DEC_BATCH, ML_HEADS, ML_DK, ML_DV), 0.1),
        'state_mlstm_n': nrm(6, (DEPTH, DEC_BATCH, ML_HEADS, ML_DK), 0.1),
        'state_mlstm_m': 2.0 + nrm(7, (DEPTH, DEC_BATCH, ML_HEADS), 0.5),
        'page_table': page_table,
        'c_prompt': nrm(9, (BATCH, D_MODEL), 1.0),
        'c_sample': nrm(10, (DEC_BATCH, D_MODEL), 1.0),
        'w_ada': nrm(11, (DEPTH, D_MODEL, 6 * D_MODEL), 0.5 * D_MODEL ** -0.5),
        'b_ada': nrm(12, (DEPTH, 6 * D_MODEL), 0.02),
        'norm_mix': gain(13, (DEPTH, D_MODEL)),
        'norm_ffn': gain(14, (DEPTH, D_MODEL)),
        'norm_final': gain(15, (D_MODEL,)),
        'w_in': nrm(16, (DEPTH, D_MODEL, IN_WIDTH), D_MODEL ** -0.5),
        'mla_q_norm': gain(17, (DEPTH, Q_LORA)),
        'mla_kv_norm': gain(18, (DEPTH, KV_LORA)),
        'w_uq': nrm(19, (DEPTH, Q_LORA, MLA_HEADS, MLA_NOPE + MLA_ROPE), Q_LORA ** -0.5),
        'w_uk': nrm(20, (DEPTH, KV_LORA, MLA_HEADS, MLA_NOPE), KV_LORA ** -0.5),
        'w_uv': nrm(21, (DEPTH, KV_LORA, MLA_HEADS, MLA_V), KV_LORA ** -0.5),
        'w_gla_a': nrm(22, (DEPTH, GLA_GATE_RANK, GLA_HEADS * GLA_DK), GLA_GATE_RANK ** -0.5),
        'b_gla_a': nrm(23, (DEPTH, GLA_HEADS * GLA_DK), 0.02),
        'gla_norm': gain(24, (DEPTH, GLA_HEADS * GLA_DV)),
        'ml_b_i': nrm(25, (DEPTH, ML_HEADS), 0.1),
        'ml_b_f': jnp.linspace(3.0, 6.0, ML_HEADS)[None, :] + nrm(26, (DEPTH, ML_HEADS), 0.1),
        'ml_norm': gain(27, (DEPTH, ML_HEADS * ML_DV)),
        'w_proj_a': nrm(28, (DEPTH, BRANCH_WIDTH, D_MODEL), BRANCH_WIDTH ** -0.5),
        'w_proj_b': nrm(29, (DEPTH, BRANCH_WIDTH, D_MODEL), BRANCH_WIDTH ** -0.5),
        'w_proj_c': nrm(30, (DEPTH, BRANCH_WIDTH, D_MODEL), BRANCH_WIDTH ** -0.5),
        'w_o': nrm(31, (DEPTH, D_MODEL, D_MODEL), D_MODEL ** -0.5),
        'peer_wq': nrm(32, (DEPTH, D_MODEL, PEER_HEADS, PEER_QDIM), D_MODEL ** -0.5),
        'peer_keys': nrm(33, (DEPTH, PEER_HEADS, 2, PEER_KEYS, PEER_HALF), PEER_HALF ** -0.5),
        'peer_u': nrm(34, (DEPTH, PEER_EXPERTS, D_MODEL), D_MODEL ** -0.5),
        'peer_v': nrm(35, (DEPTH, PEER_EXPERTS, D_MODEL), 0.25),
    }


def reference(x_prompt, x_sample, cache_kv_latent, cache_k_rope, state_gla, state_mlstm_C, state_mlstm_n,
              state_mlstm_m, page_table, c_prompt, c_sample, w_ada, b_ada, norm_mix, norm_ffn, norm_final,
              w_in, mla_q_norm, mla_kv_norm, w_uq, w_uk, w_uv, w_gla_a, b_gla_a, gla_norm, ml_b_i, ml_b_f,
              ml_norm, w_proj_a, w_proj_b, w_proj_c, w_o, peer_wq, peer_keys, peer_u, peer_v):
    P = {'w_ada': w_ada, 'b_ada': b_ada, 'norm_mix': norm_mix, 'norm_ffn': norm_ffn, 'norm_final': norm_final,
         'w_in': w_in, 'mla_q_norm': mla_q_norm, 'mla_kv_norm': mla_kv_norm, 'w_uq': w_uq, 'w_uk': w_uk,
         'w_uv': w_uv, 'w_gla_a': w_gla_a, 'b_gla_a': b_gla_a, 'gla_norm': gla_norm, 'ml_b_i': ml_b_i,
         'ml_b_f': ml_b_f, 'ml_norm': ml_norm, 'w_proj_a': w_proj_a, 'w_proj_b': w_proj_b,
         'w_proj_c': w_proj_c, 'w_o': w_o, 'peer_wq': peer_wq, 'peer_keys': peer_keys, 'peer_u': peer_u,
         'peer_v': peer_v}
    past = page_table.shape[1] * PAGE_SIZE
    pos_prompt = jnp.arange(x_prompt.shape[1], dtype=jnp.int32)
    pos_sample = past + jnp.arange(x_sample.shape[1], dtype=jnp.int32)
    b_p = x_prompt.shape[0]

    def attn_prompt(l, qn, qr, lat, kr, w_uk_l, w_uv_l):
        return mla_prompt(qn, qr, lat, kr, w_uk_l, w_uv_l)

    def attn_sample(l, qn, qr, lat, kr, w_uk_l, w_uv_l):
        return mla_sample(qn, qr, lat, kr, cache_kv_latent, cache_k_rope, l, page_table, w_uk_l, w_uv_l)

    def init_prompt(l):
        return (jnp.zeros((b_p, GLA_HEADS, GLA_DK, GLA_DV), jnp.float32),
                jnp.zeros((b_p, ML_HEADS, ML_DK, ML_DV), jnp.float32),
                jnp.zeros((b_p, ML_HEADS, ML_DK), jnp.float32),
                jnp.zeros((b_p, ML_HEADS), jnp.float32))

    def init_sample(l):
        return (state_gla[l], state_mlstm_C[l], state_mlstm_n[l], state_mlstm_m[l])

    y_prompt, (lat_p, rope_p, gla_p, C_p, n_p, m_p) = run_trunk(x_prompt, c_prompt, pos_prompt, attn_prompt, init_prompt, P)
    y_sample, (lat_s, rope_s, gla_s, C_s, n_s, m_s) = run_trunk(x_sample, c_sample, pos_sample, attn_sample, init_sample, P)
    return (y_prompt, y_sample, lat_p, rope_p, gla_p, C_p, n_p, m_p, lat_s, rope_s, gla_s, C_s, n_s, m_s)
```

```python
import functools

import numpy as np
import jax
import jax.numpy as jnp
from jax import lax
from jax.experimental import pallas as pl
from jax.experimental.pallas import tpu as pltpu

F32 = jnp.float32
BF16 = jnp.bfloat16

D_MODEL = 1024
PAGE_SIZE = 128
MLA_HEADS = 8
MLA_NOPE = 64
MLA_ROPE = 32
MLA_V = 64
Q_LORA = 256
KV_LORA = 128
ROPE_BASE = 10000.0
MLA_SCALE = (MLA_NOPE + MLA_ROPE) ** -0.5
GLA_HEADS = 4
GLA_DK = 64
GLA_DV = 128
GLA_GATE_RANK = 16
GLA_GATE_TEMP = 16.0
ML_HEADS = 4
ML_DK = 64
ML_DV = 128
PEER_HEADS = 8
PEER_KEYS = 128
PEER_HALF = 128
PEER_TOPK = 16
NORM_EPS = 1e-6

LANES = 128
SUBLANES = 8
VMEM_LIMIT = 48 * 1024 * 1024

NEG = -1e30

Z_GA, Z_GB, Z_GC = 0, 1024, 2048
Z_GV, Z_GG, Z_MV, Z_MO = 3072, 3584, 4096, 4608
Z_QC, Z_GQ, Z_GK, Z_MQ, Z_MK = 5120, 5376, 5632, 5888, 6144
Z_KVC, Z_SMALL = 6400, 6528
Z_WIDTH = 6656
SM_KPE, SM_GA, SM_MI, SM_MF = 0, 32, 48, 52


def _cp(*sem):
    return pltpu.CompilerParams(dimension_semantics=tuple(sem), vmem_limit_bytes=VMEM_LIMIT)


def _bdot(a, b):
    return jnp.dot(a.astype(BF16), b.astype(BF16), preferred_element_type=F32)


def _bdot_nt(a, b):
    return lax.dot_general(a.astype(BF16), b.astype(BF16), (((1,), (1,)), ((), ())),
                           preferred_element_type=F32)


def _split3(x):
    hi = x.astype(BF16)
    r = x - hi.astype(F32)
    mid = r.astype(BF16)
    lo = (r - mid.astype(F32)).astype(BF16)
    return hi, mid, lo


def _dot01(m01, x):
    hi, mid, lo = _split3(x)
    return (jnp.dot(m01, hi, preferred_element_type=F32) + jnp.dot(m01, mid, preferred_element_type=F32)
            + jnp.dot(m01, lo, preferred_element_type=F32))


def _dot01_r(x, m01):
    hi, mid, lo = _split3(x)
    return (jnp.dot(hi, m01, preferred_element_type=F32) + jnp.dot(mid, m01, preferred_element_type=F32)
            + jnp.dot(lo, m01, preferred_element_type=F32))


def _sigmoid(x):
    return 1.0 / (1.0 + jnp.exp(-x))


def _log_sigmoid(x):
    return jnp.minimum(x, 0.0) - jnp.log(1.0 + jnp.exp(-jnp.abs(x)))


def _rms(x):
    return x * lax.rsqrt(jnp.mean(x * x, axis=-1, keepdims=True) + NORM_EPS)


def _gelu_tanh(x):
    return 0.5 * x * (1.0 + jnp.tanh(0.7978845608028654 * (x + 0.044715 * (x * x * x))))


def _ada_kernel(c_ref, w_ref, b_ref, o_ref):
    c = c_ref[...]
    o_ref[...] = _bdot(c * _sigmoid(c), w_ref[...]) + b_ref[...]


def ada_mod(c, w, b):
    rows, d = c.shape
    n = w.shape[1]
    tn = 1536
    return pl.pallas_call(
        _ada_kernel,
        out_shape=jax.ShapeDtypeStruct((rows, n), F32),
        grid=(n // tn,),
        in_specs=[pl.BlockSpec((rows, d), lambda j: (0, 0)),
                  pl.BlockSpec((d, tn), lambda j: (0, j)),
                  pl.BlockSpec((1, tn), lambda j: (0, j))],
        out_specs=pl.BlockSpec((rows, tn), lambda j: (0, j)),
        compiler_params=_cp("arbitrary"),
        name="ada_mod",
    )(c, w, b.reshape(1, n))


def _norm_proj_kernel(x_ref, g_ref, sc_ref, sh_ref, w_ref, o_ref, hn_ref):
    @pl.when(pl.program_id(1) == 0)
    def _():
        h = _rms(x_ref[...]) * g_ref[...] * (1.0 + sc_ref[...]) + sh_ref[...]
        hn_ref[...] = h.astype(BF16)

    o_ref[...] = jnp.dot(hn_ref[...], w_ref[...], preferred_element_type=F32)


def norm_proj(x, g, scale, shift, w, tm):
    t, d = x.shape
    n = w.shape[1]
    tn = 512
    per_tok = scale.shape[0] != 1
    mod_spec = (pl.BlockSpec((tm, d), lambda i, j: (i, 0)) if per_tok
                else pl.BlockSpec((1, d), lambda i, j: (0, 0)))
    return pl.pallas_call(
        _norm_proj_kernel,
        out_shape=jax.ShapeDtypeStruct((t, n), F32),
        grid=(t // tm, n // tn),
        in_specs=[pl.BlockSpec((tm, d), lambda i, j: (i, 0)),
                  pl.BlockSpec((1, d), lambda i, j: (0, 0)),
                  mod_spec, mod_spec,
                  pl.BlockSpec((d, tn), lambda i, j: (0, j))],
        out_specs=pl.BlockSpec((tm, tn), lambda i, j: (i, j)),
        scratch_shapes=[pltpu.VMEM((tm, d), BF16)],
        compiler_params=_cp("arbitrary", "arbitrary"),
        name="norm_proj",
    )(x, g.reshape(1, d), scale, shift, w)


def _rope_lanes(x, c, sa, sb):
    return x * c + pltpu.roll(x, LANES - 16, 1) * sa + pltpu.roll(x, 16, 1) * sb


def _mla_prep_kernel(qc_ref, kvc_ref, sm_ref, gq_ref, gkv_ref, wq_ref, wuk_ref, wuv_ref,
                     cq_ref, sqa_ref, sqb_ref, ck_ref, ska_ref, skb_ref,
                     q_out, k_out, v_out, lat_out, kr_out):
    qn = (_rms(qc_ref[...]) * gq_ref[...]).astype(BF16)
    lat = _rms(kvc_ref[...]) * gkv_ref[...]
    lat_out[...] = lat
    latb = lat.astype(BF16)
    krf = _rope_lanes(sm_ref[...], ck_ref[...], ska_ref[...], skb_ref[...])
    kr_out[...] = krf[:, :MLA_ROPE]
    kr_sh = pltpu.roll(krf, MLA_NOPE, 1)
    cq, sqa, sqb = cq_ref[...], sqa_ref[...], sqb_ref[...]
    for h in range(MLA_HEADS):
        qh = jnp.dot(qn, wq_ref[h], preferred_element_type=F32)
        q_out[h] = _rope_lanes(qh, cq, sqa, sqb).astype(BF16)
        k_out[h] = (jnp.dot(latb, wuk_ref[h], preferred_element_type=F32) + kr_sh).astype(BF16)
        v_out[h] = jnp.dot(latb, wuv_ref[h], preferred_element_type=F32).astype(BF16)


def mla_prep(z, gq, gkv, wq, wuk, wuv, tabs, tm):
    t = z.shape[0]
    hh = MLA_HEADS
    row = lambda w, off: pl.BlockSpec((tm, w), lambda i, o=off // w: (i, o))
    const2 = lambda a: pl.BlockSpec(a.shape, lambda i: (0, 0))
    const3 = lambda a: pl.BlockSpec(a.shape, lambda i: (0, 0, 0))
    tab_rows = tabs[0].shape[0]
    tab_spec = (pl.BlockSpec((tm, LANES), lambda i: (i, 0)) if tab_rows != 1
                else pl.BlockSpec((1, LANES), lambda i: (0, 0)))
    return pl.pallas_call(
        _mla_prep_kernel,
        out_shape=(jax.ShapeDtypeStruct((hh, t, LANES), BF16),
                   jax.ShapeDtypeStruct((hh, t, LANES), BF16),
                   jax.ShapeDtypeStruct((hh, t, LANES), BF16),
                   jax.ShapeDtypeStruct((t, KV_LORA), F32),
                   jax.ShapeDtypeStruct((t, MLA_ROPE), F32)),
        grid=(t // tm,),
        in_specs=[row(Q_LORA, Z_QC), row(KV_LORA, Z_KVC), row(LANES, Z_SMALL),
                  const2(gq), const2(gkv), const3(wq), const3(wuk), const3(wuv)] + [tab_spec] * 6,
        out_specs=(pl.BlockSpec((hh, tm, LANES), lambda i: (0, i, 0)),
                   pl.BlockSpec((hh, tm, LANES), lambda i: (0, i, 0)),
                   pl.BlockSpec((hh, tm, LANES), lambda i: (0, i, 0)),
                   pl.BlockSpec((tm, KV_LORA), lambda i: (i, 0)),
                   pl.BlockSpec((tm, MLA_ROPE), lambda i: (i, 0))),
        compiler_params=_cp("arbitrary"),
        name="mla_prep",
    )(z, z, z, gq, gkv, wq, wuk, wuv, *tabs)


def _flash_kernel(q_ref, k_ref, v_ref, o_ref, m_ref, l_ref, acc_ref, *, tq):
    qi = pl.program_id(0)
    ki = pl.program_id(1)

    @pl.when(ki == 0)
    def _():
        m_ref[...] = jnp.full(m_ref.shape, NEG, F32)
        l_ref[...] = jnp.zeros(l_ref.shape, F32)
        acc_ref[...] = jnp.zeros(acc_ref.shape, F32)

    def step(diag):
        if diag:
            row = lax.broadcasted_iota(jnp.int32, (tq, tq), 0)
            col = lax.broadcasted_iota(jnp.int32, (tq, tq), 1)
            keep = col <= row
        for h in range(MLA_HEADS):
            s = lax.dot_general(q_ref[h], k_ref[h], (((1,), (1,)), ((), ())),
                                preferred_element_type=F32) * MLA_SCALE
            if diag:
                s = jnp.where(keep, s, NEG)
            m_prev = m_ref[h][:, :1]
            m_new = jnp.maximum(m_prev, jnp.max(s, axis=1, keepdims=True))
            alpha = jnp.exp(m_prev - m_new)
            p = jnp.exp(s - m_new)
            l_new = alpha * l_ref[h][:, :1] + jnp.sum(p, axis=1, keepdims=True)
            acc_ref[h] = alpha * acc_ref[h] + jnp.dot(p.astype(BF16), v_ref[h], preferred_element_type=F32)
            m_ref[h] = jnp.broadcast_to(m_new, (tq, LANES))
            l_ref[h] = jnp.broadcast_to(l_new, (tq, LANES))

    @pl.when(ki < qi)
    def _():
        step(False)

    @pl.when(ki == qi)
    def _():
        step(True)
        for h in range(MLA_HEADS):
            o_ref[:, h * LANES:(h + 1) * LANES] = (acc_ref[h] / l_ref[h][:, :1]).astype(BF16)


def flash_attention(q, k, v, tq):
    hh, t, _ = q.shape
    n = t // tq
    return pl.pallas_call(
        functools.partial(_flash_kernel, tq=tq),
        out_shape=jax.ShapeDtypeStruct((t, hh * LANES), BF16),
        grid=(n, n),
        in_specs=[pl.BlockSpec((hh, tq, LANES), lambda i, j: (0, i, 0)),
                  pl.BlockSpec((hh, tq, LANES), lambda i, j: (0, jnp.minimum(i, j), 0)),
                  pl.BlockSpec((hh, tq, LANES), lambda i, j: (0, jnp.minimum(i, j), 0))],
        out_specs=pl.BlockSpec((tq, hh * LANES), lambda i, j: (i, 0)),
        scratch_shapes=[pltpu.VMEM((hh, tq, LANES), F32), pltpu.VMEM((hh, tq, LANES), F32),
                        pltpu.VMEM((hh, tq, LANES), F32)],
        compiler_params=_cp("arbitrary", "arbitrary"),
        name="flash_attention",
    )(q, k, v)


def _chunk_constants(L):
    nl = int(np.log2(L))
    assert 2 ** nl == L
    idx = np.arange(L)
    tri = (idx[None, :] <= idx[:, None]).astype(np.float32)
    mq, mk, pm = [], [], []
    for lev in range(nl):
        p = 1 << lev
        start_right = (idx >> (lev + 1) << (lev + 1)) + p
        right = ((idx >> lev) & 1) == 1
        u = idx[None, :]
        mq.append((right[:, None] & (u >= start_right[:, None]) & (u <= idx[:, None])).astype(np.float32))
        mk.append(((~right)[:, None] & (u > idx[:, None]) & (u < start_right[:, None])).astype(np.float32))
        same = (idx[:, None] >> (lev + 1)) == (idx[None, :] >> (lev + 1))
        pm.append((same & right[:, None] & (~right)[None, :]).astype(np.float32))
    pm.append(np.eye(L, dtype=np.float32))
    sums = np.concatenate([tri] + mq + mk, axis=0)
    return jnp.asarray(sums, BF16), jnp.asarray(np.stack(pm), F32), jnp.asarray(tri, BF16), jnp.asarray(tri.T, BF16)


def _head_mask(width, heads, h):
    lane = lax.broadcasted_iota(jnp.int32, (1, width), 1)
    per = width // heads
    return ((lane >= h * per) & (lane < (h + 1) * per)).astype(F32)


def _gla_scan_kernel(gq_ref, gk_ref, gv_ref, gg_ref, sm_ref, wa_ref, ba_ref, sums_ref, pm_ref, gn_ref,
                     o_ref, st_out, st_ref, *, L, nl):
    step = pl.program_id(0)

    @pl.when(step == 0)
    def _():
        st_ref[...] = jnp.zeros(st_ref.shape, F32)

    width = GLA_HEADS * GLA_DK
    x = _bdot(sm_ref[...], wa_ref[...]) + ba_ref[...]
    la = _log_sigmoid(x) * (1.0 / GLA_GATE_TEMP)
    dsum = _dot01(sums_ref[...], la)
    b = dsum[0:L]
    b_last = b[L - 1:L, :]
    q = gq_ref[...] * (GLA_DK ** -0.5)
    k = gk_ref[...]
    q_in = (q * jnp.exp(b)).astype(BF16)
    k_out = k * jnp.exp(b_last - b)
    decay_all = jnp.exp(b_last)
    qs = [(q * jnp.exp(dsum[(1 + lev) * L:(2 + lev) * L])) for lev in range(nl)] + [q]
    ks = [(k * jnp.exp(dsum[(1 + nl + lev) * L:(2 + nl + lev) * L])).astype(BF16) for lev in range(nl)]
    ks = ks + [k.astype(BF16)]
    for h in range(GLA_HEADS):
        hm = _head_mask(width, GLA_HEADS, h)
        att = jnp.zeros((L, L), F32)
        for lev in range(nl + 1):
            sc = lax.dot_general((qs[lev] * hm).astype(BF16), ks[lev], (((1,), (1,)), ((), ())),
                                 preferred_element_type=F32)
            att = att + jnp.where(pm_ref[lev] > 0.0, sc, 0.0)
        vh = gv_ref[:, h * GLA_DV:(h + 1) * GLA_DV]
        st = st_ref[h]
        o = (jnp.dot(att.astype(BF16), vh.astype(BF16), preferred_element_type=F32)
             + lax.dot_general(q_in, st.astype(BF16), (((1,), (1,)), ((), ())), preferred_element_type=F32))
        st_ref[h] = st * decay_all + jnp.dot(vh.T.astype(BF16), (k_out * hm).astype(BF16),
                                             preferred_element_type=F32)
        gg = gg_ref[:, h * GLA_DV:(h + 1) * GLA_DV]
        y = _rms(o) * gn_ref[:, h * GLA_DV:(h + 1) * GLA_DV] * (gg * _sigmoid(gg))
        o_ref[:, h * GLA_DV:(h + 1) * GLA_DV] = y.astype(BF16)

    @pl.when(step == pl.num_programs(0) - 1)
    def _():
        st_out[...] = st_ref[...]


def gla_scan(z, wa, ba, gn, consts, L):
    t = z.shape[0]
    sums, pm, _, _ = consts
    nl = pm.shape[0] - 1
    width = GLA_HEADS * GLA_DK
    vw = GLA_HEADS * GLA_DV
    row = lambda w, off: pl.BlockSpec((L, w), lambda i, o=off // w: (i, o))
    c2 = lambda a: pl.BlockSpec(a.shape, lambda i: (0, 0))
    c3 = lambda a: pl.BlockSpec(a.shape, lambda i: (0, 0, 0))
    return pl.pallas_call(
        functools.partial(_gla_scan_kernel, L=L, nl=nl),
        out_shape=(jax.ShapeDtypeStruct((t, vw), BF16),
                   jax.ShapeDtypeStruct((GLA_HEADS, GLA_DV, width), F32)),
        grid=(t // L,),
        in_specs=[row(width, Z_GQ), row(width, Z_GK), row(vw, Z_GV), row(vw, Z_GG), row(LANES, Z_SMALL),
                  c2(wa), c2(ba), c2(sums), c3(pm), c2(gn)],
        out_specs=(pl.BlockSpec((L, vw), lambda i: (i, 0)),
                   pl.BlockSpec((GLA_HEADS, GLA_DV, width), lambda i: (0, 0, 0))),
        scratch_shapes=[pltpu.VMEM((GLA_HEADS, GLA_DV, width), F32)],
        compiler_params=_cp("arbitrary"),
        name="gla_scan",
    )(z, z, z, z, z, wa, ba, sums, pm, gn)


def _mlstm_scan_kernel(mq_ref, mk_ref, mv_ref, mo_ref, sm_ref, smt_ref, brow_ref, bcol_ref, tri_ref, trit_ref,
                       nrm_ref, o_ref, ct_out, n_out, m_out, ct_ref, n_ref, m_ref, *, L):
    step = pl.program_id(0)

    @pl.when(step == 0)
    def _():
        ct_ref[...] = jnp.zeros(ct_ref.shape, F32)
        n_ref[...] = jnp.zeros(n_ref.shape, F32)
        m_ref[...] = jnp.zeros(m_ref.shape, F32)

    width = ML_HEADS * ML_DK
    pre_c = sm_ref[...] + brow_ref[...]
    pre_r = smt_ref[...] + bcol_ref[...]
    b_c_all = _dot01(tri_ref[...], _log_sigmoid(pre_c))
    b_r_all = _dot01_r(_log_sigmoid(pre_r), trit_ref[...])
    q = mq_ref[...]
    k = mk_ref[...] * (ML_DK ** -0.5)
    kb = k.astype(BF16)
    row = lax.broadcasted_iota(jnp.int32, (L, L), 0)
    col = lax.broadcasted_iota(jnp.int32, (L, L), 1)
    causal = col <= row
    for h in range(ML_HEADS):
        hm = _head_mask(width, ML_HEADS, h)
        bc = b_c_all[:, SM_MF + h:SM_MF + h + 1]
        br = b_r_all[ML_HEADS + h:ML_HEADS + h + 1, :]
        ic = pre_c[:, SM_MI + h:SM_MI + h + 1]
        ir = pre_r[h:h + 1, :]
        m_prev = m_ref[h:h + 1, 0:1]
        dlog = jnp.where(causal, bc - br + ir, NEG)
        inter = bc + m_prev
        mt = jnp.maximum(inter, jnp.max(dlog, axis=1, keepdims=True))
        qh = (q * hm).astype(BF16)
        qk = lax.dot_general(qh, kb, (((1,), (1,)), ((), ())), preferred_element_type=F32)
        wi = jnp.exp(dlog - mt) * qk
        wint = jnp.exp(inter - mt)
        vh = mv_ref[:, h * ML_DV:(h + 1) * ML_DV]
        ct = ct_ref[h]
        nrow = n_ref[h:h + 1, :]
        num = (jnp.dot(wi.astype(BF16), vh.astype(BF16), preferred_element_type=F32)
               + wint * lax.dot_general(qh, ct.astype(BF16), (((1,), (1,)), ((), ())),
                                        preferred_element_type=F32))
        nq = jnp.sum(wi, axis=1, keepdims=True) + wint * jnp.sum(q * nrow, axis=1, keepdims=True)
        hh = num / jnp.maximum(jnp.abs(nq), jnp.exp(-mt))
        m_new = mt[L - 1:L, :]
        b_last = bc[L - 1:L, :]
        keep = jnp.exp(b_last + m_prev - m_new)
        wn = jnp.exp(b_last - bc + ic - m_new)
        kw = k * hm * wn
        ct_ref[h] = keep * ct + jnp.dot(vh.T.astype(BF16), kw.astype(BF16), preferred_element_type=F32)
        n_ref[h:h + 1, :] = keep * nrow + jnp.sum(kw, axis=0, keepdims=True)
        m_ref[h:h + 1, :] = jnp.broadcast_to(m_new, (1, LANES))
        mo = mo_ref[:, h * ML_DV:(h + 1) * ML_DV]
        y = _rms(hh) * nrm_ref[:, h * ML_DV:(h + 1) * ML_DV] * _sigmoid(mo)
        o_ref[:, h * ML_DV:(h + 1) * ML_DV] = y.astype(BF16)

    @pl.when(step == pl.num_programs(0) - 1)
    def _():
        ct_out[...] = ct_ref[...]
        n_out[...] = n_ref[...]
        m_out[...] = m_ref[...]


def mlstm_scan(z, smt, brow, bcol, nrm, consts, L):
    t = z.shape[0]
    _, _, tri, trit = consts
    width = ML_HEADS * ML_DK
    vw = ML_HEADS * ML_DV
    row = lambda w, off: pl.BlockSpec((L, w), lambda i, o=off // w: (i, o))
    c2 = lambda a: pl.BlockSpec(a.shape, lambda i: (0, 0))
    st_shapes = [((ML_HEADS, ML_DV, width), lambda i: (0, 0, 0)), ((SUBLANES, width), lambda i: (0, 0)),
                 ((SUBLANES, LANES), lambda i: (0, 0))]
    return pl.pallas_call(
        functools.partial(_mlstm_scan_kernel, L=L),
        out_shape=(jax.ShapeDtypeStruct((t, vw), BF16),) + tuple(
            jax.ShapeDtypeStruct(s, F32) for s, _ in st_shapes),
        grid=(t // L,),
        in_specs=[row(width, Z_MQ), row(width, Z_MK), row(vw, Z_MV), row(vw, Z_MO), row(LANES, Z_SMALL),
                  pl.BlockSpec((SUBLANES, L), lambda i: (0, i)),
                  c2(brow), c2(bcol), c2(tri), c2(trit), c2(nrm)],
        out_specs=(pl.BlockSpec((L, vw), lambda i: (i, 0)),) + tuple(pl.BlockSpec(s, f) for s, f in st_shapes),
        scratch_shapes=[pltpu.VMEM(s, F32) for s, _ in st_shapes],
        compiler_params=_cp("arbitrary"),
        name="mlstm_scan",
    )(z, z, z, z, z, smt, brow, bcol, tri, trit, nrm)


def _rec_step_kernel(gq_ref, gk_ref, gv_ref, gg_ref, mq_ref, mk_ref, mv_ref, mo_ref, sm_ref,
                     gqt_ref, gkt_ref, mqt_ref, mkt_ref, smt_ref,
                     wa_ref, ba_ref, wat_ref, bat_ref, brow_ref, gn_ref, mn_ref,
                     s_ref, c_ref, n_ref, m_ref,
                     bo_ref, co_ref, s_out, c_out, n_out, m_out, *, rows):
    blk = pl.program_id(0)
    nb = gqt_ref.shape[1]
    gw = GLA_HEADS * GLA_DK
    mw = ML_HEADS * ML_DK

    pre = sm_ref[...] + brow_ref[...]
    i_v = pre
    f_v = _log_sigmoid(pltpu.roll(pre, LANES - (SM_MF - SM_MI), 1))
    m_prev = m_ref[...]
    mt = jnp.maximum(f_v + m_prev, i_v)
    w_intra = jnp.exp(i_v - mt)
    w_inter = jnp.exp(f_v + m_prev - mt)
    inv_floor = jnp.exp(-mt)
    m_out[...] = mt
    mq = mq_ref[...]
    mk = mk_ref[...] * (ML_DK ** -0.5)
    n_old = n_ref[...]
    gq = gq_ref[...] * (GLA_DK ** -0.5)
    gk = gk_ref[...]
    qk_g, qk_m, qn_m, den = [], [], [], []
    keep_b = jnp.zeros((rows, mw), F32)
    wnew_b = jnp.zeros((rows, mw), F32)
    for h in range(ML_HEADS):
        hm = _head_mask(mw, ML_HEADS, h)
        qk_m.append(jnp.sum(mq * mk * hm, axis=1, keepdims=True))
        qn_m.append(jnp.sum(mq * n_old * hm, axis=1, keepdims=True))
        wi_h = w_intra[:, SM_MI + h:SM_MI + h + 1]
        we_h = w_inter[:, SM_MI + h:SM_MI + h + 1]
        nq = wi_h * qk_m[h] + we_h * qn_m[h]
        den.append(jnp.maximum(jnp.abs(nq), inv_floor[:, SM_MI + h:SM_MI + h + 1]))
        keep_b = keep_b + hm * we_h
        wnew_b = wnew_b + hm * wi_h
    n_out[...] = keep_b * n_old + wnew_b * mk
    for h in range(GLA_HEADS):
        qk_g.append(jnp.sum(gq * gk * _head_mask(gw, GLA_HEADS, h), axis=1, keepdims=True))

    xt = _bdot(wat_ref[...], smt_ref[...]) + bat_ref[...]
    at = jnp.exp(_log_sigmoid(xt) * (1.0 / GLA_GATE_TEMP))
    seq_lane = lax.broadcasted_iota(jnp.int32, (nb, LANES), 0)

    for r in range(rows):
        onehot = (seq_lane == blk * rows + r).astype(BF16)
        a_bc = _dot01_r(at, onehot)
        gk_bc = _dot01_r(gkt_ref[...], onehot)
        gq_bc = _dot01_r(gqt_ref[...], onehot) * (GLA_DK ** -0.5)
        mk_bc = _dot01_r(mkt_ref[...], onehot) * (ML_DK ** -0.5)
        mq_bc = _dot01_r(mqt_ref[...], onehot)
        for h in range(GLA_HEADS):
            sl = slice(h * GLA_DK, (h + 1) * GLA_DK)
            v_row = gv_ref[r:r + 1, h * GLA_DV:(h + 1) * GLA_DV]
            s_new = a_bc[sl] * s_ref[r, h] + gk_bc[sl] * v_row
            s_out[r, h] = s_new
            o = jnp.sum(gq_bc[sl] * s_new, axis=0, keepdims=True)
            gg = gg_ref[r:r + 1, h * GLA_DV:(h + 1) * GLA_DV]
            y = _rms(o) * gn_ref[:, h * GLA_DV:(h + 1) * GLA_DV] * (gg * _sigmoid(gg))
            bo_ref[r:r + 1, h * GLA_DV:(h + 1) * GLA_DV] = y.astype(BF16)
        for h in range(ML_HEADS):
            sl = slice(h * ML_DK, (h + 1) * ML_DK)
            v_row = mv_ref[r:r + 1, h * ML_DV:(h + 1) * ML_DV]
            c_old = c_ref[r, h]
            wi_s = w_intra[r:r + 1, SM_MI + h:SM_MI + h + 1]
            we_s = w_inter[r:r + 1, SM_MI + h:SM_MI + h + 1]
            num = (wi_s * qk_m[h][r:r + 1, :]) * v_row + we_s * jnp.sum(mq_bc[sl] * c_old, axis=0, keepdims=True)
            hh = num / den[h][r:r + 1, :]
            c_out[r, h] = we_s * c_old + wi_s * (mk_bc[sl] * v_row)
            mo = mo_ref[r:r + 1, h * ML_DV:(h + 1) * ML_DV]
            y = _rms(hh) * mn_ref[:, h * ML_DV:(h + 1) * ML_DV] * _sigmoid(mo)
            co_ref[r:r + 1, h * ML_DV:(h + 1) * ML_DV] = y.astype(BF16)


def rec_step(z, zt, wa, ba, wat, bat, brow, gn, mn, s0, c0, n0, m0):
    nb = z.shape[0]
    rows = SUBLANES
    gw = GLA_HEADS * GLA_DK
    mw = ML_HEADS * ML_DK
    gvw = GLA_HEADS * GLA_DV
    mvw = ML_HEADS * ML_DV
    row = lambda w, off: pl.BlockSpec((rows, w), lambda i, o=off // w: (i, o))
    colb = lambda w, off: pl.BlockSpec((w, nb), lambda i, o=off // w: (o, 0))
    c2 = lambda a: pl.BlockSpec(a.shape, lambda i: (0, 0))
    st4 = lambda hds, dk, dv: pl.BlockSpec((rows, hds, dk, dv), lambda i: (i, 0, 0, 0))
    return pl.pallas_call(
        functools.partial(_rec_step_kernel, rows=rows),
        out_shape=(jax.ShapeDtypeStruct((nb, gvw), BF16), jax.ShapeDtypeStruct((nb, mvw), BF16),
                   jax.ShapeDtypeStruct(s0.shape, F32), jax.ShapeDtypeStruct(c0.shape, F32),
                   jax.ShapeDtypeStruct((nb, mw), F32), jax.ShapeDtypeStruct((nb, LANES), F32)),
        grid=(nb // rows,),
        in_specs=[row(gw, Z_GQ), row(gw, Z_GK), row(gvw, Z_GV), row(gvw, Z_GG),
                  row(mw, Z_MQ), row(mw, Z_MK), row(mvw, Z_MV), row(mvw, Z_MO), row(LANES, Z_SMALL),
                  colb(gw, Z_GQ), colb(gw, Z_GK), colb(mw, Z_MQ), colb(mw, Z_MK), colb(LANES, Z_SMALL),
                  c2(wa), c2(ba), c2(wat), c2(bat), c2(brow), c2(gn), c2(mn),
                  st4(GLA_HEADS, GLA_DK, GLA_DV), st4(ML_HEADS, ML_DK, ML_DV),
                  pl.BlockSpec((rows, mw), lambda i: (i, 0)), pl.BlockSpec((rows, LANES), lambda i: (i, 0))],
        out_specs=(pl.BlockSpec((rows, gvw), lambda i: (i, 0)), pl.BlockSpec((rows, mvw), lambda i: (i, 0)),
                   st4(GLA_HEADS, GLA_DK, GLA_DV), st4(ML_HEADS, ML_DK, ML_DV),
                   pl.BlockSpec((rows, mw), lambda i: (i, 0)), pl.BlockSpec((rows, LANES), lambda i: (i, 0))),
        compiler_params=_cp("arbitrary"),
        name="rec_step",
    )(z, z, z, z, z, z, z, z, z, zt, zt, zt, zt, zt, wa, ba, wat, bat, brow, gn, mn, s0, c0, n0, m0)


def _mla_prep_s_kernel(qc_ref, kvc_ref, sm_ref, gq_ref, gkv_ref, wq_ref, wukt_ref,
                       cq_ref, sqa_ref, sqb_ref, ck_ref, ska_ref, skb_ref,
                       ql_out, qp_out, lat_out, kr_out):
    qn = (_rms(qc_ref[...]) * gq_ref[...]).astype(BF16)
    lat_out[...] = _rms(kvc_ref[...]) * gkv_ref[...]
    krf = _rope_lanes(sm_ref[...], ck_ref[...], ska_ref[...], skb_ref[...])
    kr_out[...] = krf[:, :MLA_ROPE]
    cq, sqa, sqb = cq_ref[...], sqa_ref[...], sqb_ref[...]
    for h in range(MLA_HEADS):
        qh = _rope_lanes(jnp.dot(qn, wq_ref[h], preferred_element_type=F32), cq, sqa, sqb)
        ql_out[h] = _bdot(qh, wukt_ref[h])
        qp_out[h] = pltpu.roll(qh, LANES - MLA_NOPE, 1)


def mla_prep_s(z, gq, gkv, wq, wukt, tabs):
    nb = z.shape[0]
    hh = MLA_HEADS
    row = lambda w, off: pl.BlockSpec((nb, w), lambda i, o=off // w: (0, o))
    c2 = lambda a: pl.BlockSpec(a.shape, lambda i: (0, 0))
    c3 = lambda a: pl.BlockSpec(a.shape, lambda i: (0, 0, 0))
    return pl.pallas_call(
        _mla_prep_s_kernel,
        out_shape=(jax.ShapeDtypeStruct((hh, nb, LANES), F32), jax.ShapeDtypeStruct((hh, nb, LANES), F32),
                   jax.ShapeDtypeStruct((nb, KV_LORA), F32), jax.ShapeDtypeStruct((nb, MLA_ROPE), F32)),
        grid=(1,),
        in_specs=[row(Q_LORA, Z_QC), row(KV_LORA, Z_KVC), row(LANES, Z_SMALL),
                  c2(gq), c2(gkv), c3(wq), c3(wukt)] + [c2(t) for t in tabs],
        out_specs=(pl.BlockSpec((hh, nb, LANES), lambda i: (0, 0, 0)),
                   pl.BlockSpec((hh, nb, LANES), lambda i: (0, 0, 0)),
                   pl.BlockSpec((nb, KV_LORA), lambda i: (0, 0)), pl.BlockSpec((nb, MLA_ROPE), lambda i: (0, 0))),
        compiler_params=_cp("arbitrary"),
        name="mla_prep_sample",
    )(z, z, z, gq, gkv, wq, wukt, *tabs)


def _mla_decode_kernel(pt_ref, ql_ref, qp_ref, ln_ref, kn_ref, *rest, G):
    lat_refs = rest[:G]
    kr_refs = rest[G:2 * G]
    o_ref, m_ref, l_ref, acc_ref = rest[2 * G:]
    s_idx = pl.program_id(1)

    @pl.when(s_idx == 0)
    def _():
        m_ref[...] = jnp.full(m_ref.shape, NEG, F32)
        l_ref[...] = jnp.zeros(l_ref.shape, F32)
        acc_ref[...] = jnp.zeros(acc_ref.shape, F32)

    ql = ql_ref[0]
    qp = qp_ref[0][:, :MLA_ROPE]
    qlb = ql.astype(BF16)
    qpb = qp.astype(BF16)
    lats = [r[...].astype(BF16) for r in lat_refs]
    sc = [(lax.dot_general(qlb, lats[g], (((1,), (1,)), ((), ())), preferred_element_type=F32)
           + lax.dot_general(qpb, kr_refs[g][...].astype(BF16), (((1,), (1,)), ((), ())),
                             preferred_element_type=F32)) for g in range(G)]
    s = jnp.concatenate(sc, axis=1) * MLA_SCALE
    m_prev = m_ref[:, :1]
    m_new = jnp.maximum(m_prev, jnp.max(s, axis=1, keepdims=True))
    alpha = jnp.exp(m_prev - m_new)
    p = jnp.exp(s - m_new)
    l_new = alpha * l_ref[:, :1] + jnp.sum(p, axis=1, keepdims=True)
    pb = p.astype(BF16)
    acc = alpha * acc_ref[...]
    for g in range(G):
        acc = acc + jnp.dot(pb[:, g * PAGE_SIZE:(g + 1) * PAGE_SIZE], lats[g], preferred_element_type=F32)
    acc_ref[...] = acc
    m_ref[...] = jnp.broadcast_to(m_new, m_ref.shape)
    l_ref[...] = jnp.broadcast_to(l_new, l_ref.shape)

    @pl.when(s_idx == pl.num_programs(1) - 1)
    def _():
        lat_new = ln_ref[0]
        s_new = (jnp.sum(ql * lat_new, axis=1, keepdims=True)
                 + jnp.sum(qp * kn_ref[0], axis=1, keepdims=True)) * MLA_SCALE
        m_old = m_ref[:, :1]
        m_fin = jnp.maximum(m_old, s_new)
        a = jnp.exp(m_old - m_fin)
        pn = jnp.exp(s_new - m_fin)
        o_ref[0] = (a * acc_ref[...] + pn * lat_new) / (a * l_ref[:, :1] + pn)


def mla_decode(page_table, ql, qp, lat_new, kr_new, cache_lat, cache_rope, layer, G):
    nb, n_pages = page_table.shape
    hh = MLA_HEADS
    seq = lambda w: pl.BlockSpec((1, hh, w), lambda b, s, pt: (b, 0, 0))
    new = lambda w: pl.BlockSpec((1, 1, w), lambda b, s, pt: (b, 0, 0))
    lat_specs = [pl.BlockSpec((None, None, PAGE_SIZE, KV_LORA),
                              lambda b, s, pt, g=g: (layer, pt[b, s * G + g], 0, 0)) for g in range(G)]
    kr_specs = [pl.BlockSpec((None, None, PAGE_SIZE, MLA_ROPE),
                             lambda b, s, pt, g=g: (layer, pt[b, s * G + g], 0, 0)) for g in range(G)]
    return pl.pallas_call(
        functools.partial(_mla_decode_kernel, G=G),
        out_shape=jax.ShapeDtypeStruct((nb, hh, KV_LORA), F32),
        grid_spec=pltpu.PrefetchScalarGridSpec(
            num_scalar_prefetch=1,
            grid=(nb, n_pages // G),
            in_specs=[seq(LANES), seq(LANES), new(KV_LORA), new(MLA_ROPE)] + lat_specs + kr_specs,
            out_specs=pl.BlockSpec((1, hh, KV_LORA), lambda b, s, pt: (b, 0, 0)),
            scratch_shapes=[pltpu.VMEM((hh, LANES), F32), pltpu.VMEM((hh, LANES), F32),
                            pltpu.VMEM((hh, KV_LORA), F32)]),
        compiler_params=_cp("arbitrary", "arbitrary"),
        name="mla_decode",
    )(page_table, ql, qp, lat_new, kr_new, *([cache_lat] * G), *([cache_rope] * G))


def _uv_proj_kernel(o_ref, w_ref, a_ref):
    a_ref[...] = _bdot(o_ref[...], w_ref[0]).astype(BF16)


def uv_proj(o_lat, wuv):
    nb = o_lat.shape[0]
    return pl.pallas_call(
        _uv_proj_kernel,
        out_shape=jax.ShapeDtypeStruct((nb, MLA_HEADS * LANES), BF16),
        grid=(MLA_HEADS,),
        in_specs=[pl.BlockSpec((nb, KV_LORA), lambda h: (0, h)),
                  pl.BlockSpec((1, KV_LORA, LANES), lambda h: (h, 0, 0))],
        out_specs=pl.BlockSpec((nb, LANES), lambda h: (0, h)),
        compiler_params=_cp("arbitrary"),
        name="uv_proj",
    )(o_lat, wuv)


def _merge_kernel(a_ref, b_ref, c_ref, ga_ref, gb_ref, gc_ref, x_ref, g1_ref, wa_ref, wb_ref, wc_ref, wo_ref,
                  nf_ref, sc_ref, sh_ref, x1_ref, h2_ref):
    merged = (_sigmoid(ga_ref[...]) * jnp.dot(a_ref[...], wa_ref[...], preferred_element_type=F32)
              + _sigmoid(gb_ref[...]) * jnp.dot(b_ref[...], wb_ref[...], preferred_element_type=F32)
              + _sigmoid(gc_ref[...]) * jnp.dot(c_ref[...], wc_ref[...], preferred_element_type=F32))
    mix = _bdot(merged, wo_ref[...])
    x1 = x_ref[...] + g1_ref[...] * mix
    x1_ref[...] = x1
    h2 = _rms(x1) * nf_ref[...] * (1.0 + sc_ref[...]) + sh_ref[...]
    h2_ref[...] = h2.astype(BF16)


def merge(a, b, c, z, x, gate1, wa, wb, wc, wo, nf, scale2, shift2, tm):
    t, d = x.shape
    per_tok = gate1.shape[0] != 1
    mod_spec = (pl.BlockSpec((tm, d), lambda i: (i, 0)) if per_tok else pl.BlockSpec((1, d), lambda i: (0, 0)))
    rowb = lambda w, o=0: pl.BlockSpec((tm, w), lambda i, o=o: (i, o))
    c2 = lambda arr: pl.BlockSpec(arr.shape, lambda i: (0, 0))
    return pl.pallas_call(
        _merge_kernel,
        out_shape=(jax.ShapeDtypeStruct((t, d), F32), jax.ShapeDtypeStruct((t, d), BF16)),
        grid=(t // tm,),
        in_specs=[rowb(a.shape[1]), rowb(b.shape[1]), rowb(c.shape[1]),
                  rowb(d, Z_GA // d), rowb(d, Z_GB // d), rowb(d, Z_GC // d), rowb(d), mod_spec,
                  c2(wa), c2(wb), c2(wc), c2(wo), pl.BlockSpec((1, d), lambda i: (0, 0)), mod_spec, mod_spec],
        out_specs=(rowb(d), rowb(d)),
        compiler_params=_cp("arbitrary"),
        name="merge",
    )(a, b, c, z, z, z, x, gate1, wa, wb, wc, wo, nf.reshape(1, d), scale2, shift2)


def _top_rows(x, n):
    rows = []
    for _ in range(n):
        m = jnp.max(x, axis=0, keepdims=True)
        rows.append(m)
        x = jnp.where(x == m, -jnp.inf, x)
    return rows


def _peer_topk_kernel(h2_ref, wq_ref, k1_ref, k2_ref, s1_out, s2_out, e1_out, e2_out, tau_out):
    h2 = h2_ref[...]
    taus = []
    for h in range(PEER_HEADS):
        qh = jnp.dot(h2, wq_ref[h], preferred_element_type=F32)
        s1 = _bdot_nt(k1_ref[h], qh[:, :PEER_HALF])
        s2 = _bdot_nt(k2_ref[h], qh[:, PEER_HALF:])
        a = _top_rows(s1, PEER_TOPK)
        b = _top_rows(s2, PEER_TOPK)
        b_all = jnp.concatenate(b, axis=0)
        cand = []
        for p in range(PEER_TOPK):
            nq = -(-(PEER_TOPK // (p + 1)) // SUBLANES) * SUBLANES
            cand.append(a[p] + b_all[:nq])
        best = _top_rows(jnp.concatenate(cand, axis=0), PEER_TOPK)
        zsum = jnp.zeros_like(best[0])
        for c in best:
            zsum = zsum + jnp.exp(c - best[0])
        s1_out[h] = s1
        s2_out[h] = s2
        e1_out[h] = jnp.exp(s1 - a[0]) / zsum
        e2_out[h] = jnp.exp(s2 - b[0])
        taus.append(best[PEER_TOPK - 1])
    tau_out[...] = jnp.concatenate(taus, axis=0)


def peer_topk(h2, wq, k1, k2, tm):
    t, d = h2.shape
    hh = PEER_HEADS
    c3 = lambda a: pl.BlockSpec(a.shape, lambda i: (0, 0, 0))
    tab = jax.ShapeDtypeStruct((hh, PEER_KEYS, t), F32)
    tab_spec = pl.BlockSpec((hh, PEER_KEYS, tm), lambda i: (0, 0, i))
    return pl.pallas_call(
        _peer_topk_kernel,
        out_shape=(tab, tab, tab, tab, jax.ShapeDtypeStruct((hh, t), F32)),
        grid=(t // tm,),
        in_specs=[pl.BlockSpec((tm, d), lambda i: (i, 0)), c3(wq), c3(k1), c3(k2)],
        out_specs=(tab_spec, tab_spec, tab_spec, tab_spec, pl.BlockSpec((hh, tm), lambda i: (0, i))),
        compiler_params=_cp("arbitrary"),
        name="peer_topk",
    )(h2, wq, k1, k2)


def _peer_dense_kernel(h2_ref, u_ref, vt_ref, s1_ref, e1_ref, s2_ref, e2_ref, tau_ref, x1_ref, g2_ref,
                       o_ref, acc_ref, *, ni):
    e = pl.program_id(1)

    @pl.when(e == 0)
    def _():
        acc_ref[...] = jnp.zeros(acc_ref.shape, F32)

    act = _gelu_tanh(lax.dot_general(u_ref[...], h2_ref[...], (((1,), (1,)), ((), ())),
                                     preferred_element_type=F32))
    blocks = []
    for ii in range(ni):
        w = jnp.zeros((PEER_KEYS, act.shape[1]), F32)
        for h in range(PEER_HEADS):
            c = s2_ref[h] + s1_ref[h, ii:ii + 1, :]
            w = w + jnp.where(c >= tau_ref[h:h + 1, :], e2_ref[h] * e1_ref[h, ii:ii + 1, :], 0.0)
        blocks.append(w)
    gate = jnp.concatenate(blocks, axis=0)
    acc_ref[...] += jnp.dot(vt_ref[...], (gate * act).astype(BF16), preferred_element_type=F32)

    @pl.when(e == pl.num_programs(1) - 1)
    def _():
        o_ref[...] = x1_ref[...] + g2_ref[...] * acc_ref[...].T


def peer_dense(h2, u, vt, s1, e1, s2, e2, tau, x1, gate2, tm, ni):
    t, d = h2.shape
    n_exp = u.shape[0]
    te = ni * PEER_KEYS
    hh = PEER_HEADS
    per_tok = gate2.shape[0] != 1
    mod_spec = (pl.BlockSpec((tm, d), lambda i, e: (i, 0)) if per_tok
                else pl.BlockSpec((1, d), lambda i, e: (0, 0)))
    sub = pl.BlockSpec((hh, ni, tm), lambda i, e: (0, e, i))
    full = pl.BlockSpec((hh, PEER_KEYS, tm), lambda i, e: (0, 0, i))
    return pl.pallas_call(
        functools.partial(_peer_dense_kernel, ni=ni),
        out_shape=jax.ShapeDtypeStruct((t, d), F32),
        grid=(t // tm, n_exp // te),
        in_specs=[pl.BlockSpec((tm, d), lambda i, e: (i, 0)),
                  pl.BlockSpec((te, d), lambda i, e: (e, 0)),
                  pl.BlockSpec((d, te), lambda i, e: (0, e)),
                  sub, sub, full, full,
                  pl.BlockSpec((hh, tm), lambda i, e: (0, i)),
                  pl.BlockSpec((tm, d), lambda i, e: (i, 0)), mod_spec],
        out_specs=pl.BlockSpec((tm, d), lambda i, e: (i, 0)),
        scratch_shapes=[pltpu.VMEM((d, tm), F32)],
        compiler_params=_cp("arbitrary", "arbitrary"),
        name="peer_dense",
    )(h2, u, vt, s1, e1, s2, e2, tau, x1, gate2)


def _final_norm_kernel(x_ref, g_ref, o_ref):
    o_ref[...] = _rms(x_ref[...]) * g_ref[...]


def final_norm(x, g, tm):
    t, d = x.shape
    return pl.pallas_call(
        _final_norm_kernel,
        out_shape=jax.ShapeDtypeStruct((t, d), F32),
        grid=(t // tm,),
        in_specs=[pl.BlockSpec((tm, d), lambda i: (i, 0)), pl.BlockSpec((1, d), lambda i: (0, 0))],
        out_specs=pl.BlockSpec((tm, d), lambda i: (i, 0)),
        compiler_params=_cp("arbitrary"),
        name="final_norm",
    )(x, g.reshape(1, d))


def _pad_to(a, axis, size):
    pad = [(0, 0)] * a.ndim
    pad[axis] = (0, size - a.shape[axis])
    return jnp.pad(a, pad)


def _prep_w_in(w):
    sizes = (Q_LORA, KV_LORA, MLA_ROPE, 256, 256, 512, GLA_GATE_RANK, 512, 256, 256, 512, ML_HEADS, ML_HEADS, 512,
             D_MODEL, D_MODEL, D_MODEL)
    offs = np.concatenate([[0], np.cumsum(sizes)])
    (q_c, kv_c, k_pe, gq, gk, gv, ga, gg, mq, mk, mv, mi, mf, mo, g_a, g_b, g_c) = [
        w[:, offs[i]:offs[i + 1]] for i in range(len(sizes))]
    small = _pad_to(jnp.concatenate([k_pe, ga, mi, mf], axis=1), 1, LANES)
    out = jnp.concatenate([g_a, g_b, g_c, gv, gg, mv, mo, q_c, gq, gk, mq, mk, kv_c, small], axis=1)
    assert out.shape[1] == Z_WIDTH
    return out.astype(BF16)


def _rope_tables(pos):
    half = MLA_ROPE // 2
    inv = ROPE_BASE ** (-jnp.arange(half, dtype=F32) / half)
    ang = pos.astype(F32)[:, None] * inv[None, :]
    cos, sin = jnp.cos(ang), jnp.sin(ang)
    n = pos.shape[0]
    z = lambda w: jnp.zeros((n, w), F32)
    one = jnp.ones((n, MLA_NOPE), F32)
    rest = LANES - MLA_NOPE - MLA_ROPE
    cq = jnp.concatenate([one, cos, cos, z(rest)], axis=1)
    sqa = jnp.concatenate([z(MLA_NOPE), -sin, z(half), z(rest)], axis=1)
    sqb = jnp.concatenate([z(MLA_NOPE), z(half), sin, z(rest)], axis=1)
    ck = jnp.concatenate([cos, cos, z(LANES - MLA_ROPE)], axis=1)
    ska = jnp.concatenate([-sin, z(LANES - half)], axis=1)
    skb = jnp.concatenate([z(half), sin, z(LANES - MLA_ROPE)], axis=1)
    return (cq, sqa, sqb, ck, ska, skb)


def _layer_weights(l, P):
    w = {}
    w['w_in'] = _prep_w_in(P['w_in'][l])
    w['gq'] = P['mla_q_norm'][l].reshape(1, Q_LORA)
    w['gkv'] = P['mla_kv_norm'][l].reshape(1, KV_LORA)
    w['wq'] = _pad_to(jnp.transpose(P['w_uq'][l], (1, 0, 2)), 2, LANES).astype(BF16)
    wuk = jnp.transpose(P['w_uk'][l], (1, 0, 2))
    w['wuk'] = _pad_to(wuk, 2, LANES).astype(BF16)
    w['wukt'] = _pad_to(jnp.transpose(wuk, (0, 2, 1)), 1, LANES).astype(BF16)
    w['wuv'] = _pad_to(jnp.transpose(P['w_uv'][l], (1, 0, 2)), 2, LANES).astype(BF16)
    wa = jnp.zeros((LANES, GLA_HEADS * GLA_DK), F32).at[SM_GA:SM_GA + GLA_GATE_RANK].set(P['w_gla_a'][l])
    w['gla_wa'] = wa.astype(BF16)
    w['gla_wat'] = wa.T.astype(BF16)
    w['gla_ba'] = P['b_gla_a'][l].reshape(1, -1)
    w['gla_bat'] = P['b_gla_a'][l].reshape(-1, 1)
    w['gla_norm'] = P['gla_norm'][l].reshape(1, -1)
    w['ml_norm'] = P['ml_norm'][l].reshape(1, -1)
    brow = jnp.zeros((1, LANES), F32)
    brow = brow.at[0, SM_MI:SM_MI + ML_HEADS].set(P['ml_b_i'][l]).at[0, SM_MF:SM_MF + ML_HEADS].set(P['ml_b_f'][l])
    w['ml_brow'] = brow
    w['ml_bcol'] = jnp.concatenate([P['ml_b_i'][l], P['ml_b_f'][l]]).reshape(2 * ML_HEADS, 1)
    wpa = P['w_proj_a'][l].reshape(MLA_HEADS, MLA_V, D_MODEL)
    w['wpa'] = _pad_to(wpa, 1, LANES).reshape(MLA_HEADS * LANES, D_MODEL).astype(BF16)
    w['wpb'] = P['w_proj_b'][l].astype(BF16)
    w['wpc'] = P['w_proj_c'][l].astype(BF16)
    w['wo'] = P['w_o'][l].astype(BF16)
    w['peer_wq'] = jnp.transpose(P['peer_wq'][l], (1, 0, 2)).astype(BF16)
    w['peer_k1'] = P['peer_keys'][l][:, 0].astype(BF16)
    w['peer_k2'] = P['peer_keys'][l][:, 1].astype(BF16)
    w['peer_u'] = P['peer_u'][l].astype(BF16)
    w['peer_vt'] = P['peer_v'][l].T.astype(BF16)
    return w


def _mods(c, w_ada, b_ada):
    rows = c.shape[0]
    cp = _pad_to(c, 0, SUBLANES) if rows < SUBLANES else c
    mod = ada_mod(cp, w_ada, b_ada)[:rows]
    return jnp.split(mod, 6, axis=-1)


def _peer_block(h2, x1, gate2, w, tm, ni):
    s1, s2, e1, e2, tau = peer_topk(h2, w['peer_wq'], w['peer_k1'], w['peer_k2'], tm)
    return peer_dense(h2, w['peer_u'], w['peer_vt'], s1, e1, s2, e2, tau, x1, gate2, tm, ni)


def _trunk_prompt(x, c, P, W):
    t = x.shape[0]
    tm = min(512, t)
    L = min(128, t)
    consts = _chunk_constants(L)
    tabs = _rope_tables(jnp.arange(t, dtype=jnp.int32))
    lat_rows, rope_rows, gla_st, c_st, n_st, m_st = [], [], [], [], [], []
    for l in range(len(W)):
        w = W[l]
        shift1, scale1, gate1, shift2, scale2, gate2 = _mods(c, P['w_ada'][l], P['b_ada'][l])
        z = norm_proj(x, P['norm_mix'][l], scale1, shift1, w['w_in'], tm)
        q, k, v, lat, kr = mla_prep(z, w['gq'], w['gkv'], w['wq'], w['wuk'], w['wuv'], tabs, tm)
        a = flash_attention(q, k, v, tm)
        b, st = gla_scan(z, w['gla_wa'], w['gla_ba'], w['gla_norm'], consts, L)
        smt = z[:, Z_SMALL + SM_MI:Z_SMALL + SM_MI + 2 * ML_HEADS].T
        cc, ct, nrow, mrow = mlstm_scan(z, smt, w['ml_brow'], w['ml_bcol'], w['ml_norm'], consts, L)
        x1, h2 = merge(a, b, cc, z, x, gate1, w['wpa'], w['wpb'], w['wpc'], w['wo'],
                       P['norm_ffn'][l], scale2, shift2, tm)
        x = _peer_block(h2, x1, gate2, w, tm, SUBLANES)
        lat_rows.append(lat)
        rope_rows.append(kr)
        gla_st.append(jnp.stack([st[h][:, h * GLA_DK:(h + 1) * GLA_DK].T for h in range(GLA_HEADS)]))
        c_st.append(jnp.stack([ct[h][:, h * ML_DK:(h + 1) * ML_DK].T for h in range(ML_HEADS)]))
        n_st.append(jnp.stack([nrow[h, h * ML_DK:(h + 1) * ML_DK] for h in range(ML_HEADS)]))
        m_st.append(mrow[:ML_HEADS, 0])
    y = final_norm(x, P['norm_final'], tm)
    return y, tuple(jnp.stack(r) for r in (lat_rows, rope_rows, gla_st, c_st, n_st, m_st))


def _trunk_sample(x, c, P, W, cache_lat, cache_rope, page_table, s_gla, s_c, s_n, s_m):
    nb = x.shape[0]
    past = page_table.shape[1] * PAGE_SIZE
    tabs = _rope_tables(jnp.full((1,), past, dtype=jnp.int32))
    G = min(16, page_table.shape[1])
    lat_rows, rope_rows, gla_st, c_st, n_st, m_st = [], [], [], [], [], []
    for l in range(len(W)):
        w = W[l]
        shift1, scale1, gate1, shift2, scale2, gate2 = _mods(c, P['w_ada'][l], P['b_ada'][l])
        z = norm_proj(x, P['norm_mix'][l], scale1, shift1, w['w_in'], nb)
        ql, qp, lat, kr = mla_prep_s(z, w['gq'], w['gkv'], w['wq'], w['wukt'], tabs)
        o_lat = mla_decode(page_table, jnp.transpose(ql, (1, 0, 2)), jnp.transpose(qp, (1, 0, 2)),
                           lat[:, None, :], kr[:, None, :], cache_lat, cache_rope, l, G)
        a = uv_proj(o_lat.reshape(nb, MLA_HEADS * KV_LORA), w['wuv'])
        m_in = jnp.zeros((nb, LANES), F32).at[:, SM_MI:SM_MI + ML_HEADS].set(s_m[l])
        b, cc, s_new, c_new, n_new, m_new = rec_step(
            z, z.T, w['gla_wa'], w['gla_ba'], w['gla_wat'], w['gla_bat'], w['ml_brow'], w['gla_norm'], w['ml_norm'],
            s_gla[l], s_c[l], s_n[l].reshape(nb, ML_HEADS * ML_DK), m_in)
        x1, h2 = merge(a, b, cc, z, x, gate1, w['wpa'], w['wpb'], w['wpc'], w['wo'],
                       P['norm_ffn'][l], scale2, shift2, nb)
        x = _peer_block(h2, x1, gate2, w, nb, SUBLANES)
        lat_rows.append(lat[:, None, :])
        rope_rows.append(kr[:, None, :])
        gla_st.append(s_new)
        c_st.append(c_new)
        n_st.append(n_new.reshape(nb, ML_HEADS, ML_DK))
        m_st.append(m_new[:, SM_MI:SM_MI + ML_HEADS])
    y = final_norm(x, P['norm_final'], nb)
    return y, tuple(jnp.stack(r) for r in (lat_rows, rope_rows, gla_st, c_st, n_st, m_st))


def kernel(x_prompt, x_sample, cache_kv_latent, cache_k_rope, state_gla, state_mlstm_C, state_mlstm_n, state_mlstm_m, page_table, c_prompt, c_sample, w_ada, b_ada, norm_mix, norm_ffn, norm_final, w_in, mla_q_norm, mla_kv_norm, w_uq, w_uk, w_uv, w_gla_a, b_gla_a, gla_norm, ml_b_i, ml_b_f, ml_norm, w_proj_a, w_proj_b, w_proj_c, w_o, peer_wq, peer_keys, peer_u, peer_v):
    P = {'w_ada': w_ada, 'b_ada': b_ada, 'norm_mix': norm_mix, 'norm_ffn': norm_ffn, 'norm_final': norm_final,
         'w_in': w_in, 'mla_q_norm': mla_q_norm, 'mla_kv_norm': mla_kv_norm, 'w_uq': w_uq, 'w_uk': w_uk,
         'w_uv': w_uv, 'w_gla_a': w_gla_a, 'b_gla_a': b_gla_a, 'gla_norm': gla_norm, 'ml_b_i': ml_b_i,
         'ml_b_f': ml_b_f, 'ml_norm': ml_norm, 'w_proj_a': w_proj_a, 'w_proj_b': w_proj_b,
         'w_proj_c': w_proj_c, 'w_o': w_o, 'peer_wq': peer_wq, 'peer_keys': peer_keys, 'peer_u': peer_u,
         'peer_v': peer_v}
    depth = w_in.shape[0]
    W = [_layer_weights(l, P) for l in range(depth)]
    bp, sp, d = x_prompt.shape
    assert bp == 1 and x_sample.shape[1] == 1
    nb = x_sample.shape[0]
    y_p, (lat_p, rope_p, gla_p, c_p, n_p, m_p) = _trunk_prompt(x_prompt.reshape(sp, d), c_prompt, P, W)
    y_s, (lat_s, rope_s, gla_s, c_s, n_s, m_s) = _trunk_sample(
        x_sample.reshape(nb, d), c_sample, P, W, cache_kv_latent, cache_k_rope, page_table,
        state_gla, state_mlstm_C, state_mlstm_n, state_mlstm_m)
    return (y_p.reshape(bp, sp, d), y_s.reshape(nb, 1, d),
            lat_p[:, None], rope_p[:, None], gla_p[:, None], c_p[:, None], n_p[:, None], m_p[:, None],
            lat_s, rope_s, gla_s, c_s, n_s, m_s)
```

```python
import functools

import numpy as np
import jax
import jax.numpy as jnp
from jax import lax
from jax.experimental import pallas as pl
from jax.experimental.pallas import tpu as pltpu

F32 = jnp.float32
BF16 = jnp.bfloat16

D_MODEL = 1024
PAGE_SIZE = 128
MLA_HEADS = 8
MLA_NOPE = 64
MLA_ROPE = 32
MLA_V = 64
Q_LORA = 256
KV_LORA = 128
ROPE_BASE = 10000.0
MLA_SCALE = (MLA_NOPE + MLA_ROPE) ** -0.5
GLA_HEADS = 4
GLA_DK = 64
GLA_DV = 128
GLA_GATE_RANK = 16
GLA_GATE_TEMP = 16.0
ML_HEADS = 4
ML_DK = 64
ML_DV = 128
PEER_HEADS = 8
PEER_KEYS = 128
PEER_HALF = 128
PEER_TOPK = 16
NORM_EPS = 1e-6

LANES = 128
SUBLANES = 8
VMEM_LIMIT = 48 * 1024 * 1024

NEG = -1e30
LOG2E = 1.4426950408889634

Z_GA, Z_GB, Z_GC = 0, 1024, 2048
Z_GV, Z_GG, Z_MV, Z_MO = 3072, 3584, 4096, 4608
Z_QC, Z_GQ, Z_GK, Z_MQ, Z_MK = 5120, 5376, 5632, 5888, 6144
Z_KVC, Z_SMALL = 6400, 6528
Z_WIDTH = 6656
SM_KPE, SM_GA, SM_MI, SM_MF = 0, 32, 48, 52


def _cp(*sem):
    return pltpu.CompilerParams(dimension_semantics=tuple(sem), vmem_limit_bytes=VMEM_LIMIT)


def _bdot(a, b):
    return jnp.dot(a.astype(BF16), b.astype(BF16), preferred_element_type=F32)


def _bdot_nt(a, b):
    return lax.dot_general(a.astype(BF16), b.astype(BF16), (((1,), (1,)), ((), ())),
                           preferred_element_type=F32)


def _split3(x):
    hi = x.astype(BF16)
    r = x - hi.astype(F32)
    mid = r.astype(BF16)
    lo = (r - mid.astype(F32)).astype(BF16)
    return hi, mid, lo


def _dot01(m01, x):
    hi, mid, lo = _split3(x)
    return (jnp.dot(m01, hi, preferred_element_type=F32) + jnp.dot(m01, mid, preferred_element_type=F32)
            + jnp.dot(m01, lo, preferred_element_type=F32))


def _dot01_r(x, m01):
    hi, mid, lo = _split3(x)
    return (jnp.dot(hi, m01, preferred_element_type=F32) + jnp.dot(mid, m01, preferred_element_type=F32)
            + jnp.dot(lo, m01, preferred_element_type=F32))


def _sigmoid(x):
    return 1.0 / (1.0 + jnp.exp(-x))


def _log_sigmoid(x):
    return jnp.minimum(x, 0.0) - jnp.log(1.0 + jnp.exp(-jnp.abs(x)))


def _rms(x):
    return x * lax.rsqrt(jnp.mean(x * x, axis=-1, keepdims=True) + NORM_EPS)


def _gelu_tanh(x):
    return 0.5 * x * (1.0 + jnp.tanh(0.7978845608028654 * (x + 0.044715 * (x * x * x))))


def _ada_kernel(c_ref, w_ref, b_ref, o_ref):
    c = c_ref[...]
    o_ref[...] = _bdot(c * _sigmoid(c), w_ref[...]) + b_ref[...]


def ada_mod(c, w, b):
    rows, d = c.shape
    n = w.shape[1]
    tn = 1536
    return pl.pallas_call(
        _ada_kernel,
        out_shape=jax.ShapeDtypeStruct((rows, n), F32),
        grid=(n // tn,),
        in_specs=[pl.BlockSpec((rows, d), lambda j: (0, 0)),
                  pl.BlockSpec((d, tn), lambda j: (0, j)),
                  pl.BlockSpec((1, tn), lambda j: (0, j))],
        out_specs=pl.BlockSpec((rows, tn), lambda j: (0, j)),
        compiler_params=_cp("arbitrary"),
        name="ada_mod",
    )(c, w, b.reshape(1, n))


def _norm_proj_kernel(x_ref, g_ref, sc_ref, sh_ref, w_ref, o_ref, hn_ref):
    @pl.when(pl.program_id(1) == 0)
    def _():
        h = _rms(x_ref[...]) * g_ref[...] * (1.0 + sc_ref[...]) + sh_ref[...]
        hn_ref[...] = h.astype(BF16)

    o_ref[...] = jnp.dot(hn_ref[...], w_ref[...], preferred_element_type=F32)


def norm_proj(x, g, scale, shift, w, tm):
    t, d = x.shape
    n = w.shape[1]
    tn = 512
    per_tok = scale.shape[0] != 1
    mod_spec = (pl.BlockSpec((tm, d), lambda i, j: (i, 0)) if per_tok
                else pl.BlockSpec((1, d), lambda i, j: (0, 0)))
    return pl.pallas_call(
        _norm_proj_kernel,
        out_shape=jax.ShapeDtypeStruct((t, n), F32),
        grid=(t // tm, n // tn),
        in_specs=[pl.BlockSpec((tm, d), lambda i, j: (i, 0)),
                  pl.BlockSpec((1, d), lambda i, j: (0, 0)),
                  mod_spec, mod_spec,
                  pl.BlockSpec((d, tn), lambda i, j: (0, j))],
        out_specs=pl.BlockSpec((tm, tn), lambda i, j: (i, j)),
        scratch_shapes=[pltpu.VMEM((tm, d), BF16)],
        compiler_params=_cp("arbitrary", "arbitrary"),
        name="norm_proj",
    )(x, g.reshape(1, d), scale, shift, w)


def _rope_lanes(x, c, sa, sb):
    return x * c + pltpu.roll(x, LANES - 16, 1) * sa + pltpu.roll(x, 16, 1) * sb


def _mla_prep_kernel(qc_ref, kvc_ref, sm_ref, gq_ref, gkv_ref, wqt_ref, wuk_ref, wuvt_ref,
                     ct_ref, st_ref, ck_ref, ska_ref, skb_ref,
                     qt_out, k_out, vt_out, lat_out, kr_out):
    half = MLA_ROPE // 2
    qn = (_rms(qc_ref[...]) * gq_ref[...]).astype(BF16)
    lat = _rms(kvc_ref[...]) * gkv_ref[...]
    lat_out[...] = lat
    latb = lat.astype(BF16)
    krf = _rope_lanes(sm_ref[...], ck_ref[...], ska_ref[...], skb_ref[...])
    kr_out[...] = krf[:, :MLA_ROPE]
    kr_sh = pltpu.roll(krf, MLA_NOPE, 1)
    cos_t, sin_t = ct_ref[...], st_ref[...]
    tm = qn.shape[0]
    ones_row = (lax.broadcasted_iota(jnp.int32, (LANES, tm), 0) == MLA_V).astype(F32)
    pad = jnp.zeros((LANES - MLA_NOPE - MLA_ROPE, tm), F32)
    for h in range(MLA_HEADS):
        qt = lax.dot_general(wqt_ref[h], qn, (((1,), (1,)), ((), ())), preferred_element_type=F32)
        x1 = qt[MLA_NOPE:MLA_NOPE + half]
        x2 = qt[MLA_NOPE + half:MLA_NOPE + MLA_ROPE]
        qt = jnp.concatenate([qt[:MLA_NOPE], x1 * cos_t - x2 * sin_t, x1 * sin_t + x2 * cos_t, pad], axis=0)
        qt_out[h] = (qt * (MLA_SCALE * LOG2E)).astype(BF16)
        k_out[h] = (jnp.dot(latb, wuk_ref[h], preferred_element_type=F32) + kr_sh).astype(BF16)
        vt = lax.dot_general(wuvt_ref[h], latb, (((1,), (1,)), ((), ())), preferred_element_type=F32)
        vt_out[h] = (vt + ones_row).astype(BF16)


def mla_prep(z, gq, gkv, wqt, wuk, wuvt, tabs_t, tabs, tm):
    t = z.shape[0]
    hh = MLA_HEADS
    row = lambda w, off: pl.BlockSpec((tm, w), lambda i, o=off // w: (i, o))
    const2 = lambda a: pl.BlockSpec(a.shape, lambda i: (0, 0))
    const3 = lambda a: pl.BlockSpec(a.shape, lambda i: (0, 0, 0))
    tab_spec = pl.BlockSpec((tm, LANES), lambda i: (i, 0))
    tabt_spec = pl.BlockSpec((MLA_ROPE // 2, tm), lambda i: (0, i))
    return pl.pallas_call(
        _mla_prep_kernel,
        out_shape=(jax.ShapeDtypeStruct((hh, LANES, t), BF16),
                   jax.ShapeDtypeStruct((hh, t, LANES), BF16),
                   jax.ShapeDtypeStruct((hh, LANES, t), BF16),
                   jax.ShapeDtypeStruct((t, KV_LORA), F32),
                   jax.ShapeDtypeStruct((t, MLA_ROPE), F32)),
        grid=(t // tm,),
        in_specs=[row(Q_LORA, Z_QC), row(KV_LORA, Z_KVC), row(LANES, Z_SMALL),
                  const2(gq), const2(gkv), const3(wqt), const3(wuk), const3(wuvt),
                  tabt_spec, tabt_spec, tab_spec, tab_spec, tab_spec],
        out_specs=(pl.BlockSpec((hh, LANES, tm), lambda i: (0, 0, i)),
                   pl.BlockSpec((hh, tm, LANES), lambda i: (0, i, 0)),
                   pl.BlockSpec((hh, LANES, tm), lambda i: (0, 0, i)),
                   pl.BlockSpec((tm, KV_LORA), lambda i: (i, 0)),
                   pl.BlockSpec((tm, MLA_ROPE), lambda i: (i, 0))),
        compiler_params=_cp("arbitrary"),
        name="mla_prep",
    )(z, z, z, gq, gkv, wqt, wuk, wuvt, *tabs_t, *tabs[3:])


def _flash_kernel(qt_ref, k_ref, vt_ref, o_ref, m_ref, acc_ref, *, tq):
    qi = pl.program_id(0)
    ki = pl.program_id(1)

    @pl.when(ki == 0)
    def _():
        m_ref[...] = jnp.full(m_ref.shape, NEG, F32)
        acc_ref[...] = jnp.zeros(acc_ref.shape, F32)

    def scores(h):
        return jnp.dot(k_ref[h], qt_ref[h], preferred_element_type=F32)

    def accumulate(h, p, alpha):
        acc_ref[h] = alpha * acc_ref[h] + jnp.dot(vt_ref[h], p, preferred_element_type=F32)

    def step(diag):
        if diag:
            key = lax.broadcasted_iota(jnp.int32, (tq, tq), 0)
            qry = lax.broadcasted_iota(jnp.int32, (tq, tq), 1)
            keep = key <= qry
        s_next = scores(0)
        pending = None
        for h in range(MLA_HEADS):
            s = s_next
            if h + 1 < MLA_HEADS:
                s_next = scores(h + 1)
            if pending is not None:
                accumulate(*pending)
            if diag:
                s = jnp.where(keep, s, NEG)
            m_prev = m_ref[h:h + 1, :]
            m_new = jnp.maximum(m_prev, jnp.max(s, axis=0, keepdims=True))
            p = jnp.exp2(s - m_new).astype(BF16)
            m_ref[h:h + 1, :] = m_new
            pending = (h, p, jnp.exp2(m_prev - m_new))
        accumulate(*pending)

    @pl.when(ki < qi)
    def _():
        step(False)

    @pl.when(ki == qi)
    def _():
        step(True)
        for h in range(MLA_HEADS):
            acc = acc_ref[h]
            o_ref[:, h * LANES:(h + 1) * LANES] = (acc / acc[MLA_V:MLA_V + 1, :]).T.astype(BF16)


def flash_attention(qt, k, vt, tq):
    hh, t, _ = k.shape
    n = t // tq
    return pl.pallas_call(
        functools.partial(_flash_kernel, tq=tq),
        out_shape=jax.ShapeDtypeStruct((t, hh * LANES), BF16),
        grid=(n, n),
        in_specs=[pl.BlockSpec((hh, LANES, tq), lambda i, j: (0, 0, i)),
                  pl.BlockSpec((hh, tq, LANES), lambda i, j: (0, jnp.minimum(i, j), 0)),
                  pl.BlockSpec((hh, LANES, tq), lambda i, j: (0, 0, jnp.minimum(i, j)))],
        out_specs=pl.BlockSpec((tq, hh * LANES), lambda i, j: (i, 0)),
        scratch_shapes=[pltpu.VMEM((hh, tq), F32), pltpu.VMEM((hh, LANES, tq), F32)],
        compiler_params=_cp("arbitrary", "arbitrary"),
        name="flash_attention",
    )(qt, k, vt)


def _chunk_constants(L):
    nl = int(np.log2(L))
    assert 2 ** nl == L
    idx = np.arange(L)
    tri = (idx[None, :] <= idx[:, None]).astype(np.float32)
    mq, mk, pm = [], [], []
    for lev in range(nl):
        p = 1 << lev
        start_right = (idx >> (lev + 1) << (lev + 1)) + p
        right = ((idx >> lev) & 1) == 1
        u = idx[None, :]
        mq.append((right[:, None] & (u >= start_right[:, None]) & (u <= idx[:, None])).astype(np.float32))
        mk.append(((~right)[:, None] & (u > idx[:, None]) & (u < start_right[:, None])).astype(np.float32))
        same = (idx[:, None] >> (lev + 1)) == (idx[None, :] >> (lev + 1))
        pm.append((same & right[:, None] & (~right)[None, :]).astype(np.float32))
    pm.append(np.eye(L, dtype=np.float32))
    sums = np.concatenate([tri] + mq + mk, axis=0)
    return jnp.asarray(sums, BF16), jnp.asarray(np.stack(pm), F32), jnp.asarray(tri, BF16), jnp.asarray(tri.T, BF16)


def _head_mask(width, heads, h):
    lane = lax.broadcasted_iota(jnp.int32, (1, width), 1)
    per = width // heads
    return ((lane >= h * per) & (lane < (h + 1) * per)).astype(F32)


def _gla_scan_kernel(gq_ref, gk_ref, gv_ref, gg_ref, sm_ref, wa_ref, ba_ref, sums_ref, pm_ref, gn_ref,
                     o_ref, st_out, st_ref, *, L, nl):
    step = pl.program_id(0)

    @pl.when(step == 0)
    def _():
        st_ref[...] = jnp.zeros(st_ref.shape, F32)

    width = GLA_HEADS * GLA_DK
    x = _bdot(sm_ref[...], wa_ref[...]) + ba_ref[...]
    la = _log_sigmoid(x) * (1.0 / GLA_GATE_TEMP)
    dsum = _dot01(sums_ref[...], la)
    b = dsum[0:L]
    b_last = b[L - 1:L, :]
    q = gq_ref[...] * (GLA_DK ** -0.5)
    k = gk_ref[...]
    q_in = (q * jnp.exp(b)).astype(BF16)
    k_out = k * jnp.exp(b_last - b)
    decay_all = jnp.exp(b_last)
    qs = [(q * jnp.exp(dsum[(1 + lev) * L:(2 + lev) * L])) for lev in range(nl)] + [q]
    ks = [(k * jnp.exp(dsum[(1 + nl + lev) * L:(2 + nl + lev) * L])).astype(BF16) for lev in range(nl)]
    ks = ks + [k.astype(BF16)]
    for h in range(GLA_HEADS):
        hm = _head_mask(width, GLA_HEADS, h)
        att = jnp.zeros((L, L), F32)
        for lev in range(nl + 1):
            sc = lax.dot_general((qs[lev] * hm).astype(BF16), ks[lev], (((1,), (1,)), ((), ())),
                                 preferred_element_type=F32)
            att = att + jnp.where(pm_ref[lev] > 0.0, sc, 0.0)
        vh = gv_ref[:, h * GLA_DV:(h + 1) * GLA_DV]
        st = st_ref[h]
        o = (jnp.dot(att.astype(BF16), vh.astype(BF16), preferred_element_type=F32)
             + lax.dot_general(q_in, st.astype(BF16), (((1,), (1,)), ((), ())), preferred_element_type=F32))
        st_ref[h] = st * decay_all + jnp.dot(vh.T.astype(BF16), (k_out * hm).astype(BF16),
                                             preferred_element_type=F32)
        gg = gg_ref[:, h * GLA_DV:(h + 1) * GLA_DV]
        y = _rms(o) * gn_ref[:, h * GLA_DV:(h + 1) * GLA_DV] * (gg * _sigmoid(gg))
        o_ref[:, h * GLA_DV:(h + 1) * GLA_DV] = y.astype(BF16)

    @pl.when(step == pl.num_programs(0) - 1)
    def _():
        st_out[...] = st_ref[...]


def gla_scan(z, wa, ba, gn, consts, L):
    t = z.shape[0]
    sums, pm, _, _ = consts
    nl = pm.shape[0] - 1
    width = GLA_HEADS * GLA_DK
    vw = GLA_HEADS * GLA_DV
    row = lambda w, off: pl.BlockSpec((L, w), lambda i, o=off // w: (i, o))
    c2 = lambda a: pl.BlockSpec(a.shape, lambda i: (0, 0))
    c3 = lambda a: pl.BlockSpec(a.shape, lambda i: (0, 0, 0))
    return pl.pallas_call(
        functools.partial(_gla_scan_kernel, L=L, nl=nl),
        out_shape=(jax.ShapeDtypeStruct((t, vw), BF16),
                   jax.ShapeDtypeStruct((GLA_HEADS, GLA_DV, width), F32)),
        grid=(t // L,),
        in_specs=[row(width, Z_GQ), row(width, Z_GK), row(vw, Z_GV), row(vw, Z_GG), row(LANES, Z_SMALL),
                  c2(wa), c2(ba), c2(sums), c3(pm), c2(gn)],
        out_specs=(pl.BlockSpec((L, vw), lambda i: (i, 0)),
                   pl.BlockSpec((GLA_HEADS, GLA_DV, width), lambda i: (0, 0, 0))),
        scratch_shapes=[pltpu.VMEM((GLA_HEADS, GLA_DV, width), F32)],
        compiler_params=_cp("arbitrary"),
        name="gla_scan",
    )(z, z, z, z, z, wa, ba, sums, pm, gn)


def _mlstm_scan_kernel(mq_ref, mk_ref, mv_ref, mo_ref, sm_ref, smt_ref, brow_ref, bcol_ref, tri_ref, trit_ref,
                       nrm_ref, o_ref, ct_out, n_out, m_out, ct_ref, n_ref, m_ref, *, L):
    step = pl.program_id(0)

    @pl.when(step == 0)
    def _():
        ct_ref[...] = jnp.zeros(ct_ref.shape, F32)
        n_ref[...] = jnp.zeros(n_ref.shape, F32)
        m_ref[...] = jnp.zeros(m_ref.shape, F32)

    width = ML_HEADS * ML_DK
    pre_c = sm_ref[...] + brow_ref[...]
    pre_r = smt_ref[...] + bcol_ref[...]
    b_c_all = _dot01(tri_ref[...], _log_sigmoid(pre_c))
    b_r_all = _dot01_r(_log_sigmoid(pre_r), trit_ref[...])
    q = mq_ref[...]
    k = mk_ref[...] * (ML_DK ** -0.5)
    kb = k.astype(BF16)
    row = lax.broadcasted_iota(jnp.int32, (L, L), 0)
    col = lax.broadcasted_iota(jnp.int32, (L, L), 1)
    causal = col <= row
    for h in range(ML_HEADS):
        hm = _head_mask(width, ML_HEADS, h)
        bc = b_c_all[:, SM_MF + h:SM_MF + h + 1]
        br = b_r_all[ML_HEADS + h:ML_HEADS + h + 1, :]
        ic = pre_c[:, SM_MI + h:SM_MI + h + 1]
        ir = pre_r[h:h + 1, :]
        m_prev = m_ref[h:h + 1, 0:1]
        dlog = jnp.where(causal, bc - br + ir, NEG)
        inter = bc + m_prev
        mt = jnp.maximum(inter, jnp.max(dlog, axis=1, keepdims=True))
        qh = (q * hm).astype(BF16)
        qk = lax.dot_general(qh, kb, (((1,), (1,)), ((), ())), preferred_element_type=F32)
        wi = jnp.exp(dlog - mt) * qk
        wint = jnp.exp(inter - mt)
        vh = mv_ref[:, h * ML_DV:(h + 1) * ML_DV]
        ct = ct_ref[h]
        nrow = n_ref[h:h + 1, :]
        num = (jnp.dot(wi.astype(BF16), vh.astype(BF16), preferred_element_type=F32)
               + wint * lax.dot_general(qh, ct.astype(BF16), (((1,), (1,)), ((), ())),
                                        preferred_element_type=F32))
        nq = jnp.sum(wi, axis=1, keepdims=True) + wint * jnp.sum(q * nrow, axis=1, keepdims=True)
        hh = num / jnp.maximum(jnp.abs(nq), jnp.exp(-mt))
        m_new = mt[L - 1:L, :]
        b_last = bc[L - 1:L, :]
        keep = jnp.exp(b_last + m_prev - m_new)
        wn = jnp.exp(b_last - bc + ic - m_new)
        kw = k * hm * wn
        ct_ref[h] = keep * ct + jnp.dot(vh.T.astype(BF16), kw.astype(BF16), preferred_element_type=F32)
        n_ref[h:h + 1, :] = keep * nrow + jnp.sum(kw, axis=0, keepdims=True)
        m_ref[h:h + 1, :] = jnp.broadcast_to(m_new, (1, LANES))
        mo = mo_ref[:, h * ML_DV:(h + 1) * ML_DV]
        y = _rms(hh) * nrm_ref[:, h * ML_DV:(h + 1) * ML_DV] * _sigmoid(mo)
        o_ref[:, h * ML_DV:(h + 1) * ML_DV] = y.astype(BF16)

    @pl.when(step == pl.num_programs(0) - 1)
    def _():
        ct_out[...] = ct_ref[...]
        n_out[...] = n_ref[...]
        m_out[...] = m_ref[...]


def mlstm_scan(z, smt, brow, bcol, nrm, consts, L):
    t = z.shape[0]
    _, _, tri, trit = consts
    width = ML_HEADS * ML_DK
    vw = ML_HEADS * ML_DV
    row = lambda w, off: pl.BlockSpec((L, w), lambda i, o=off // w: (i, o))
    c2 = lambda a: pl.BlockSpec(a.shape, lambda i: (0, 0))
    st_shapes = [((ML_HEADS, ML_DV, width), lambda i: (0, 0, 0)), ((SUBLANES, width), lambda i: (0, 0)),
                 ((SUBLANES, LANES), lambda i: (0, 0))]
    return pl.pallas_call(
        functools.partial(_mlstm_scan_kernel, L=L),
        out_shape=(jax.ShapeDtypeStruct((t, vw), BF16),) + tuple(
            jax.ShapeDtypeStruct(s, F32) for s, _ in st_shapes),
        grid=(t // L,),
        in_specs=[row(width, Z_MQ), row(width, Z_MK), row(vw, Z_MV), row(vw, Z_MO), row(LANES, Z_SMALL),
                  pl.BlockSpec((SUBLANES, L), lambda i: (0, i)),
                  c2(brow), c2(bcol), c2(tri), c2(trit), c2(nrm)],
        out_specs=(pl.BlockSpec((L, vw), lambda i: (i, 0)),) + tuple(pl.BlockSpec(s, f) for s, f in st_shapes),
        scratch_shapes=[pltpu.VMEM(s, F32) for s, _ in st_shapes],
        compiler_params=_cp("arbitrary"),
        name="mlstm_scan",
    )(z, z, z, z, z, smt, brow, bcol, tri, trit, nrm)


def _rec_step_kernel(gq_ref, gk_ref, gv_ref, gg_ref, mq_ref, mk_ref, mv_ref, mo_ref, sm_ref,
                     gqt_ref, gkt_ref, mqt_ref, mkt_ref, smt_ref,
                     wa_ref, ba_ref, wat_ref, bat_ref, brow_ref, gn_ref, mn_ref,
                     s_ref, c_ref, n_ref, m_ref,
                     bo_ref, co_ref, s_out, c_out, n_out, m_out, *, rows):
    blk = pl.program_id(0)
    nb = gqt_ref.shape[1]
    gw = GLA_HEADS * GLA_DK
    mw = ML_HEADS * ML_DK

    pre = sm_ref[...] + brow_ref[...]
    i_v = pre
    f_v = _log_sigmoid(pltpu.roll(pre, LANES - (SM_MF - SM_MI), 1))
    m_prev = m_ref[...]
    mt = jnp.maximum(f_v + m_prev, i_v)
    w_intra = jnp.exp(i_v - mt)
    w_inter = jnp.exp(f_v + m_prev - mt)
    inv_floor = jnp.exp(-mt)
    m_out[...] = mt
    mq = mq_ref[...]
    mk = mk_ref[...] * (ML_DK ** -0.5)
    n_old = n_ref[...]
    gq = gq_ref[...] * (GLA_DK ** -0.5)
    gk = gk_ref[...]
    qk_g, qk_m, qn_m, den = [], [], [], []
    keep_b = jnp.zeros((rows, mw), F32)
    wnew_b = jnp.zeros((rows, mw), F32)
    for h in range(ML_HEADS):
        hm = _head_mask(mw, ML_HEADS, h)
        qk_m.append(jnp.sum(mq * mk * hm, axis=1, keepdims=True))
        qn_m.append(jnp.sum(mq * n_old * hm, axis=1, keepdims=True))
        wi_h = w_intra[:, SM_MI + h:SM_MI + h + 1]
        we_h = w_inter[:, SM_MI + h:SM_MI + h + 1]
        nq = wi_h * qk_m[h] + we_h * qn_m[h]
        den.append(jnp.maximum(jnp.abs(nq), inv_floor[:, SM_MI + h:SM_MI + h + 1]))
        keep_b = keep_b + hm * we_h
        wnew_b = wnew_b + hm * wi_h
    n_out[...] = keep_b * n_old + wnew_b * mk
    for h in range(GLA_HEADS):
        qk_g.append(jnp.sum(gq * gk * _head_mask(gw, GLA_HEADS, h), axis=1, keepdims=True))

    xt = _bdot(wat_ref[...], smt_ref[...]) + bat_ref[...]
    at = jnp.exp(_log_sigmoid(xt) * (1.0 / GLA_GATE_TEMP))
    seq_lane = lax.broadcasted_iota(jnp.int32, (nb, LANES), 0)

    for r in range(rows):
        onehot = (seq_lane == blk * rows + r).astype(BF16)
        a_bc = _dot01_r(at, onehot)
        gk_bc = _dot01_r(gkt_ref[...], onehot)
        gq_bc = _dot01_r(gqt_ref[...], onehot) * (GLA_DK ** -0.5)
        mk_bc = _dot01_r(mkt_ref[...], onehot) * (ML_DK ** -0.5)
        mq_bc = _dot01_r(mqt_ref[...], onehot)
        for h in range(GLA_HEADS):
            sl = slice(h * GLA_DK, (h + 1) * GLA_DK)
            v_row = gv_ref[r:r + 1, h * GLA_DV:(h + 1) * GLA_DV]
            s_new = a_bc[sl] * s_ref[r, h] + gk_bc[sl] * v_row
            s_out[r, h] = s_new
            o = jnp.sum(gq_bc[sl] * s_new, axis=0, keepdims=True)
            gg = gg_ref[r:r + 1, h * GLA_DV:(h + 1) * GLA_DV]
            y = _rms(o) * gn_ref[:, h * GLA_DV:(h + 1) * GLA_DV] * (gg * _sigmoid(gg))
            bo_ref[r:r + 1, h * GLA_DV:(h + 1) * GLA_DV] = y.astype(BF16)
        for h in range(ML_HEADS):
            sl = slice(h * ML_DK, (h + 1) * ML_DK)
            v_row = mv_ref[r:r + 1, h * ML_DV:(h + 1) * ML_DV]
            c_old = c_ref[r, h]
            wi_s = w_intra[r:r + 1, SM_MI + h:SM_MI + h + 1]
            we_s = w_inter[r:r + 1, SM_MI + h:SM_MI + h + 1]
            num = (wi_s * qk_m[h][r:r + 1, :]) * v_row + we_s * jnp.sum(mq_bc[sl] * c_old, axis=0, keepdims=True)
            hh = num / den[h][r:r + 1, :]
            c_out[r, h] = we_s * c_old + wi_s * (mk_bc[sl] * v_row)
            mo = mo_ref[r:r + 1, h * ML_DV:(h + 1) * ML_DV]
            y = _rms(hh) * mn_ref[:, h * ML_DV:(h + 1) * ML_DV] * _sigmoid(mo)
            co_ref[r:r + 1, h * ML_DV:(h + 1) * ML_DV] = y.astype(BF16)


def rec_step(z, zt, wa, ba, wat, bat, brow, gn, mn, s0, c0, n0, m0):
    nb = z.shape[0]
    rows = SUBLANES
    gw = GLA_HEADS * GLA_DK
    mw = ML_HEADS * ML_DK
    gvw = GLA_HEADS * GLA_DV
    mvw = ML_HEADS * ML_DV
    row = lambda w, off: pl.BlockSpec((rows, w), lambda i, o=off // w: (i, o))
    colb = lambda w, off: pl.BlockSpec((w, nb), lambda i, o=off // w: (o, 0))
    c2 = lambda a: pl.BlockSpec(a.shape, lambda i: (0, 0))
    st4 = lambda hds, dk, dv: pl.BlockSpec((rows, hds, dk, dv), lambda i: (i, 0, 0, 0))
    return pl.pallas_call(
        functools.partial(_rec_step_kernel, rows=rows),
        out_shape=(jax.ShapeDtypeStruct((nb, gvw), BF16), jax.ShapeDtypeStruct((nb, mvw), BF16),
                   jax.ShapeDtypeStruct(s0.shape, F32), jax.ShapeDtypeStruct(c0.shape, F32),
                   jax.ShapeDtypeStruct((nb, mw), F32), jax.ShapeDtypeStruct((nb, LANES), F32)),
        grid=(nb // rows,),
        in_specs=[row(gw, Z_GQ), row(gw, Z_GK), row(gvw, Z_GV), row(gvw, Z_GG),
                  row(mw, Z_MQ), row(mw, Z_MK), row(mvw, Z_MV), row(mvw, Z_MO), row(LANES, Z_SMALL),
                  colb(gw, Z_GQ), colb(gw, Z_GK), colb(mw, Z_MQ), colb(mw, Z_MK), colb(LANES, Z_SMALL),
                  c2(wa), c2(ba), c2(wat), c2(bat), c2(brow), c2(gn), c2(mn),
                  st4(GLA_HEADS, GLA_DK, GLA_DV), st4(ML_HEADS, ML_DK, ML_DV),
                  pl.BlockSpec((rows, mw), lambda i: (i, 0)), pl.BlockSpec((rows, LANES), lambda i: (i, 0))],
        out_specs=(pl.BlockSpec((rows, gvw), lambda i: (i, 0)), pl.BlockSpec((rows, mvw), lambda i: (i, 0)),
                   st4(GLA_HEADS, GLA_DK, GLA_DV), st4(ML_HEADS, ML_DK, ML_DV),
                   pl.BlockSpec((rows, mw), lambda i: (i, 0)), pl.BlockSpec((rows, LANES), lambda i: (i, 0))),
        compiler_params=_cp("arbitrary"),
        name="rec_step",
    )(z, z, z, z, z, z, z, z, z, zt, zt, zt, zt, zt, wa, ba, wat, bat, brow, gn, mn, s0, c0, n0, m0)


def _mla_prep_s_kernel(qc_ref, kvc_ref, sm_ref, gq_ref, gkv_ref, wq_ref, wukt_ref,
                       cq_ref, sqa_ref, sqb_ref, ck_ref, ska_ref, skb_ref,
                       ql_out, qp_out, lat_out, kr_out):
    qn = (_rms(qc_ref[...]) * gq_ref[...]).astype(BF16)
    lat_out[...] = _rms(kvc_ref[...]) * gkv_ref[...]
    krf = _rope_lanes(sm_ref[...], ck_ref[...], ska_ref[...], skb_ref[...])
    kr_out[...] = krf[:, :MLA_ROPE]
    cq, sqa, sqb = cq_ref[...], sqa_ref[...], sqb_ref[...]
    for h in range(MLA_HEADS):
        qh = _rope_lanes(jnp.dot(qn, wq_ref[h], preferred_element_type=F32), cq, sqa, sqb)
        ql_out[h] = _bdot(qh, wukt_ref[h])
        qp_out[h] = pltpu.roll(qh, LANES - MLA_NOPE, 1)


def mla_prep_s(z, gq, gkv, wq, wukt, tabs):
    nb = z.shape[0]
    hh = MLA_HEADS
    row = lambda w, off: pl.BlockSpec((nb, w), lambda i, o=off // w: (0, o))
    c2 = lambda a: pl.BlockSpec(a.shape, lambda i: (0, 0))
    c3 = lambda a: pl.BlockSpec(a.shape, lambda i: (0, 0, 0))
    return pl.pallas_call(
        _mla_prep_s_kernel,
        out_shape=(jax.ShapeDtypeStruct((hh, nb, LANES), F32), jax.ShapeDtypeStruct((hh, nb, LANES), F32),
                   jax.ShapeDtypeStruct((nb, KV_LORA), F32), jax.ShapeDtypeStruct((nb, MLA_ROPE), F32)),
        grid=(1,),
        in_specs=[row(Q_LORA, Z_QC), row(KV_LORA, Z_KVC), row(LANES, Z_SMALL),
                  c2(gq), c2(gkv), c3(wq), c3(wukt)] + [c2(t) for t in tabs],
        out_specs=(pl.BlockSpec((hh, nb, LANES), lambda i: (0, 0, 0)),
                   pl.BlockSpec((hh, nb, LANES), lambda i: (0, 0, 0)),
                   pl.BlockSpec((nb, KV_LORA), lambda i: (0, 0)), pl.BlockSpec((nb, MLA_ROPE), lambda i: (0, 0))),
        compiler_params=_cp("arbitrary"),
        name="mla_prep_sample",
    )(z, z, z, gq, gkv, wq, wukt, *tabs)


def _mla_decode_kernel(pt_ref, ql_ref, qp_ref, ln_ref, kn_ref, *rest, G):
    lat_refs = rest[:G]
    kr_refs = rest[G:2 * G]
    o_ref, m_ref, l_ref, acc_ref = rest[2 * G:]
    s_idx = pl.program_id(1)

    @pl.when(s_idx == 0)
    def _():
        m_ref[...] = jnp.full(m_ref.shape, NEG, F32)
        l_ref[...] = jnp.zeros(l_ref.shape, F32)
        acc_ref[...] = jnp.zeros(acc_ref.shape, F32)

    ql = ql_ref[0]
    qp = qp_ref[0][:, :MLA_ROPE]
    qlb = ql.astype(BF16)
    qpb = qp.astype(BF16)
    lats = [r[...].astype(BF16) for r in lat_refs]
    sc = [(lax.dot_general(qlb, lats[g], (((1,), (1,)), ((), ())), preferred_element_type=F32)
           + jnp.dot(qpb, kr_refs[g][...].astype(BF16), preferred_element_type=F32)) for g in range(G)]
    s = jnp.concatenate(sc, axis=1) * MLA_SCALE
    m_prev = m_ref[:, :1]
    m_new = jnp.maximum(m_prev, jnp.max(s, axis=1, keepdims=True))
    alpha = jnp.exp(m_prev - m_new)
    p = jnp.exp(s - m_new)
    l_new = alpha * l_ref[:, :1] + jnp.sum(p, axis=1, keepdims=True)
    pb = p.astype(BF16)
    acc = alpha * acc_ref[...]
    for g in range(G):
        acc = acc + jnp.dot(pb[:, g * PAGE_SIZE:(g + 1) * PAGE_SIZE], lats[g], preferred_element_type=F32)
    acc_ref[...] = acc
    m_ref[...] = jnp.broadcast_to(m_new, m_ref.shape)
    l_ref[...] = jnp.broadcast_to(l_new, l_ref.shape)

    @pl.when(s_idx == pl.num_programs(1) - 1)
    def _():
        lat_new = ln_ref[0]
        s_new = (jnp.sum(ql * lat_new, axis=1, keepdims=True)
                 + jnp.sum(qp * kn_ref[0], axis=1, keepdims=True)) * MLA_SCALE
        m_old = m_ref[:, :1]
        m_fin = jnp.maximum(m_old, s_new)
        a = jnp.exp(m_old - m_fin)
        pn = jnp.exp(s_new - m_fin)
        o_ref[0] = (a * acc_ref[...] + pn * lat_new) / (a * l_ref[:, :1] + pn)


def mla_decode(page_table, ql, qp, lat_new, kr_new, cache_lat, cache_rope_t, layer, G):
    nb, n_pages = page_table.shape
    hh = MLA_HEADS
    seq = lambda w: pl.BlockSpec((1, hh, w), lambda b, s, pt: (b, 0, 0))
    new = lambda w: pl.BlockSpec((1, 1, w), lambda b, s, pt: (b, 0, 0))
    lat_specs = [pl.BlockSpec((None, None, PAGE_SIZE, KV_LORA),
                              lambda b, s, pt, g=g: (layer, pt[b, s * G + g], 0, 0)) for g in range(G)]
    kr_specs = [pl.BlockSpec((None, None, MLA_ROPE, PAGE_SIZE),
                             lambda b, s, pt, g=g: (layer, pt[b, s * G + g], 0, 0)) for g in range(G)]
    return pl.pallas_call(
        functools.partial(_mla_decode_kernel, G=G),
        out_shape=jax.ShapeDtypeStruct((nb, hh, KV_LORA), F32),
        grid_spec=pltpu.PrefetchScalarGridSpec(
            num_scalar_prefetch=1,
            grid=(nb, n_pages // G),
            in_specs=[seq(LANES), seq(LANES), new(KV_LORA), new(MLA_ROPE)] + lat_specs + kr_specs,
            out_specs=pl.BlockSpec((1, hh, KV_LORA), lambda b, s, pt: (b, 0, 0)),
            scratch_shapes=[pltpu.VMEM((hh, LANES), F32), pltpu.VMEM((hh, LANES), F32),
                            pltpu.VMEM((hh, KV_LORA), F32)]),
        compiler_params=_cp("arbitrary", "arbitrary"),
        name="mla_decode",
    )(page_table, ql, qp, lat_new, kr_new, *([cache_lat] * G), *([cache_rope_t] * G))


def _uv_proj_kernel(o_ref, w_ref, a_ref):
    a_ref[...] = _bdot(o_ref[...], w_ref[0]).astype(BF16)


def uv_proj(o_lat, wuv):
    nb = o_lat.shape[0]
    return pl.pallas_call(
        _uv_proj_kernel,
        out_shape=jax.ShapeDtypeStruct((nb, MLA_HEADS * LANES), BF16),
        grid=(MLA_HEADS,),
        in_specs=[pl.BlockSpec((nb, KV_LORA), lambda h: (0, h)),
                  pl.BlockSpec((1, KV_LORA, LANES), lambda h: (h, 0, 0))],
        out_specs=pl.BlockSpec((nb, LANES), lambda h: (0, h)),
        compiler_params=_cp("arbitrary"),
        name="uv_proj",
    )(o_lat, wuv)


def _merge_kernel(a_ref, b_ref, c_ref, ga_ref, gb_ref, gc_ref, x_ref, g1_ref, wa_ref, wb_ref, wc_ref, wo_ref,
                  nf_ref, sc_ref, sh_ref, x1_ref, h2_ref):
    merged = (_sigmoid(ga_ref[...]) * jnp.dot(a_ref[...], wa_ref[...], preferred_element_type=F32)
              + _sigmoid(gb_ref[...]) * jnp.dot(b_ref[...], wb_ref[...], preferred_element_type=F32)
              + _sigmoid(gc_ref[...]) * jnp.dot(c_ref[...], wc_ref[...], preferred_element_type=F32))
    mix = _bdot(merged, wo_ref[...])
    x1 = x_ref[...] + g1_ref[...] * mix
    x1_ref[...] = x1
    h2 = _rms(x1) * nf_ref[...] * (1.0 + sc_ref[...]) + sh_ref[...]
    h2_ref[...] = h2.astype(BF16)


def merge(a, b, c, z, x, gate1, wa, wb, wc, wo, nf, scale2, shift2, tm):
    t, d = x.shape
    per_tok = gate1.shape[0] != 1
    mod_spec = (pl.BlockSpec((tm, d), lambda i: (i, 0)) if per_tok else pl.BlockSpec((1, d), lambda i: (0, 0)))
    rowb = lambda w, o=0: pl.BlockSpec((tm, w), lambda i, o=o: (i, o))
    c2 = lambda arr: pl.BlockSpec(arr.shape, lambda i: (0, 0))
    return pl.pallas_call(
        _merge_kernel,
        out_shape=(jax.ShapeDtypeStruct((t, d), F32), jax.ShapeDtypeStruct((t, d), BF16)),
        grid=(t // tm,),
        in_specs=[rowb(a.shape[1]), rowb(b.shape[1]), rowb(c.shape[1]),
                  rowb(d, Z_GA // d), rowb(d, Z_GB // d), rowb(d, Z_GC // d), rowb(d), mod_spec,
                  c2(wa), c2(wb), c2(wc), c2(wo), pl.BlockSpec((1, d), lambda i: (0, 0)), mod_spec, mod_spec],
        out_specs=(rowb(d), rowb(d)),
        compiler_params=_cp("arbitrary"),
        name="merge",
    )(a, b, c, z, z, z, x, gate1, wa, wb, wc, wo, nf.reshape(1, d), scale2, shift2)


def _top_rows(x, n):
    rows = []
    for _ in range(n):
        m = jnp.max(x, axis=0, keepdims=True)
        rows.append(m)
        x = jnp.where(x == m, -jnp.inf, x)
    return rows


def _peer_topk_kernel(h2_ref, wq_ref, k1_ref, k2_ref, thr_out, s2_out, e1_out, e2_out):
    h2 = h2_ref[...]
    k1 = PEER_TOPK + 1
    for h in range(PEER_HEADS):
        qh = jnp.dot(h2, wq_ref[h], preferred_element_type=F32)
        s1 = _bdot_nt(k1_ref[h], qh[:, :PEER_HALF])
        s2 = _bdot_nt(k2_ref[h], qh[:, PEER_HALF:])
        a = _top_rows(s1, k1)
        b = _top_rows(s2, k1)
        n_rows = -(-k1 // SUBLANES) * SUBLANES
        b_all = jnp.concatenate(b + [jnp.full_like(b[0], -jnp.inf)] * (n_rows - k1), axis=0)
        cand = []
        for p in range(k1):
            nq = -(-(k1 // (p + 1)) // SUBLANES) * SUBLANES
            cand.append(a[p] + b_all[:nq])
        best = _top_rows(jnp.concatenate(cand, axis=0), k1)
        zsum = jnp.zeros_like(best[0])
        for c in best[:PEER_TOPK]:
            zsum = zsum + jnp.exp(c - best[0])
        tabs = ((thr_out, 0.5 * (best[PEER_TOPK - 1] + best[PEER_TOPK]) - s1), (s2_out, s2),
                (e1_out, jnp.exp(s1 - a[0]) / zsum), (e2_out, jnp.exp(s2 - b[0])))
        for ref, val in tabs:
            for c in range(val.shape[1] // LANES):
                ref[h, c] = val[:, c * LANES:(c + 1) * LANES]


def peer_topk(h2, wq, k1, k2, tm):
    t, d = h2.shape
    hh = PEER_HEADS
    c3 = lambda a: pl.BlockSpec(a.shape, lambda i: (0, 0, 0))
    assert tm % LANES == 0
    tab = jax.ShapeDtypeStruct((hh, t // LANES, PEER_KEYS, LANES), F32)
    tab_spec = pl.BlockSpec((hh, tm // LANES, PEER_KEYS, LANES), lambda i: (0, i, 0, 0))
    return pl.pallas_call(
        _peer_topk_kernel,
        out_shape=(tab, tab, tab, tab),
        grid=(t // tm,),
        in_specs=[pl.BlockSpec((tm, d), lambda i: (i, 0)), c3(wq), c3(k1), c3(k2)],
        out_specs=(tab_spec, tab_spec, tab_spec, tab_spec),
        compiler_params=_cp("arbitrary"),
        name="peer_topk",
    )(h2, wq, k1, k2)


def _peer_dense_kernel(h2_ref, u_ref, vt_ref, thr_ref, e1_ref, s2_ref, e2_ref, x1_ref, g2_ref,
                       o_ref, acc_ref, act_ref, wg_ref, *, ni):
    e = pl.program_id(1)

    @pl.when(e == 0)
    def _():
        acc_ref[...] = jnp.zeros(acc_ref.shape, F32)

    act_ref[...] = _gelu_tanh(lax.dot_general(u_ref[...], h2_ref[...], (((1,), (1,)), ((), ())),
                                              preferred_element_type=F32))
    tm = act_ref.shape[1]
    for c in range(tm // LANES):
        cs = slice(c * LANES, (c + 1) * LANES)
        for ii in range(ni):
            rs = slice(ii * PEER_KEYS, (ii + 1) * PEER_KEYS)
            w = jnp.zeros((PEER_KEYS, LANES), F32)
            for h in range(PEER_HEADS):
                keep = s2_ref[h, c] >= thr_ref[h, c, ii:ii + 1, :]
                w = w + jnp.where(keep, e2_ref[h, c] * e1_ref[h, c, ii:ii + 1, :], 0.0)
            wg_ref[rs, cs] = (w * act_ref[rs, cs]).astype(BF16)
    acc_ref[...] += jnp.dot(vt_ref[...], wg_ref[...], preferred_element_type=F32)

    @pl.when(e == pl.num_programs(1) - 1)
    def _():
        o_ref[...] = x1_ref[...] + g2_ref[...] * acc_ref[...].T


def peer_dense(h2, u, vt, thr, e1, s2, e2, x1, gate2, tm, ni):
    t, d = h2.shape
    n_exp = u.shape[0]
    te = ni * PEER_KEYS
    hh = PEER_HEADS
    per_tok = gate2.shape[0] != 1
    mod_spec = (pl.BlockSpec((tm, d), lambda i, e: (i, 0)) if per_tok
                else pl.BlockSpec((1, d), lambda i, e: (0, 0)))
    nc = tm // LANES
    sub = pl.BlockSpec((hh, nc, ni, LANES), lambda i, e: (0, i, e, 0))
    full = pl.BlockSpec((hh, nc, PEER_KEYS, LANES), lambda i, e: (0, i, 0, 0))
    return pl.pallas_call(
        functools.partial(_peer_dense_kernel, ni=ni),
        out_shape=jax.ShapeDtypeStruct((t, d), F32),
        grid=(t // tm, n_exp // te),
        in_specs=[pl.BlockSpec((tm, d), lambda i, e: (i, 0)),
                  pl.BlockSpec((te, d), lambda i, e: (e, 0)),
                  pl.BlockSpec((d, te), lambda i, e: (0, e)),
                  sub, sub, full, full,
                  pl.BlockSpec((tm, d), lambda i, e: (i, 0)), mod_spec],
        out_specs=pl.BlockSpec((tm, d), lambda i, e: (i, 0)),
        scratch_shapes=[pltpu.VMEM((d, tm), F32), pltpu.VMEM((te, tm), F32), pltpu.VMEM((te, tm), BF16)],
        compiler_params=_cp("arbitrary", "arbitrary"),
        name="peer_dense",
    )(h2, u, vt, thr, e1, s2, e2, x1, gate2)


def _final_norm_kernel(x_ref, g_ref, o_ref):
    o_ref[...] = _rms(x_ref[...]) * g_ref[...]


def final_norm(x, g, tm):
    t, d = x.shape
    return pl.pallas_call(
        _final_norm_kernel,
        out_shape=jax.ShapeDtypeStruct((t, d), F32),
        grid=(t // tm,),
        in_specs=[pl.BlockSpec((tm, d), lambda i: (i, 0)), pl.BlockSpec((1, d), lambda i: (0, 0))],
        out_specs=pl.BlockSpec((tm, d), lambda i: (i, 0)),
        compiler_params=_cp("arbitrary"),
        name="final_norm",
    )(x, g.reshape(1, d))


def _pad_to(a, axis, size):
    pad = [(0, 0)] * a.ndim
    pad[axis] = (0, size - a.shape[axis])
    return jnp.pad(a, pad)


def _prep_w_in(w):
    sizes = (Q_LORA, KV_LORA, MLA_ROPE, 256, 256, 512, GLA_GATE_RANK, 512, 256, 256, 512, ML_HEADS, ML_HEADS, 512,
             D_MODEL, D_MODEL, D_MODEL)
    offs = np.concatenate([[0], np.cumsum(sizes)])
    (q_c, kv_c, k_pe, gq, gk, gv, ga, gg, mq, mk, mv, mi, mf, mo, g_a, g_b, g_c) = [
        w[:, offs[i]:offs[i + 1]] for i in range(len(sizes))]
    small = _pad_to(jnp.concatenate([k_pe, ga, mi, mf], axis=1), 1, LANES)
    out = jnp.concatenate([g_a, g_b, g_c, gv, gg, mv, mo, q_c, gq, gk, mq, mk, kv_c, small], axis=1)
    assert out.shape[1] == Z_WIDTH
    return out.astype(BF16)


def _rope_tables(pos):
    half = MLA_ROPE // 2
    inv = ROPE_BASE ** (-jnp.arange(half, dtype=F32) / half)
    ang = pos.astype(F32)[:, None] * inv[None, :]
    cos, sin = jnp.cos(ang), jnp.sin(ang)
    n = pos.shape[0]
    z = lambda w: jnp.zeros((n, w), F32)
    one = jnp.ones((n, MLA_NOPE), F32)
    rest = LANES - MLA_NOPE - MLA_ROPE
    cq = jnp.concatenate([one, cos, cos, z(rest)], axis=1)
    sqa = jnp.concatenate([z(MLA_NOPE), -sin, z(half), z(rest)], axis=1)
    sqb = jnp.concatenate([z(MLA_NOPE), z(half), sin, z(rest)], axis=1)
    ck = jnp.concatenate([cos, cos, z(LANES - MLA_ROPE)], axis=1)
    ska = jnp.concatenate([-sin, z(LANES - half)], axis=1)
    skb = jnp.concatenate([z(half), sin, z(LANES - MLA_ROPE)], axis=1)
    return (cq, sqa, sqb, ck, ska, skb), (cos.T, sin.T)


def _layer_weights(l, P):
    w = {}
    w['w_in'] = _prep_w_in(P['w_in'][l])
    w['gq'] = P['mla_q_norm'][l].reshape(1, Q_LORA)
    w['gkv'] = P['mla_kv_norm'][l].reshape(1, KV_LORA)
    w['wq'] = _pad_to(jnp.transpose(P['w_uq'][l], (1, 0, 2)), 2, LANES).astype(BF16)
    wuk = jnp.transpose(P['w_uk'][l], (1, 0, 2))
    w['wuk'] = _pad_to(wuk, 2, LANES).astype(BF16)
    w['wukt'] = _pad_to(jnp.transpose(wuk, (0, 2, 1)), 1, LANES).astype(BF16)
    w['wuv'] = _pad_to(jnp.transpose(P['w_uv'][l], (1, 0, 2)), 2, LANES).astype(BF16)
    w['wqt'] = _pad_to(jnp.transpose(P['w_uq'][l], (1, 2, 0)), 1, LANES).astype(BF16)
    w['wuvt'] = _pad_to(jnp.transpose(P['w_uv'][l], (1, 2, 0)), 1, LANES).astype(BF16)
    wa = jnp.zeros((LANES, GLA_HEADS * GLA_DK), F32).at[SM_GA:SM_GA + GLA_GATE_RANK].set(P['w_gla_a'][l])
    w['gla_wa'] = wa.astype(BF16)
    w['gla_wat'] = wa.T.astype(BF16)
    w['gla_ba'] = P['b_gla_a'][l].reshape(1, -1)
    w['gla_bat'] = P['b_gla_a'][l].reshape(-1, 1)
    w['gla_norm'] = P['gla_norm'][l].reshape(1, -1)
    w['ml_norm'] = P['ml_norm'][l].reshape(1, -1)
    brow = jnp.zeros((1, LANES), F32)
    brow = brow.at[0, SM_MI:SM_MI + ML_HEADS].set(P['ml_b_i'][l]).at[0, SM_MF:SM_MF + ML_HEADS].set(P['ml_b_f'][l])
    w['ml_brow'] = brow
    w['ml_bcol'] = jnp.concatenate([P['ml_b_i'][l], P['ml_b_f'][l]]).reshape(2 * ML_HEADS, 1)
    wpa = P['w_proj_a'][l].reshape(MLA_HEADS, MLA_V, D_MODEL)
    w['wpa'] = _pad_to(wpa, 1, LANES).reshape(MLA_HEADS * LANES, D_MODEL).astype(BF16)
    w['wpb'] = P['w_proj_b'][l].astype(BF16)
    w['wpc'] = P['w_proj_c'][l].astype(BF16)
    w['wo'] = P['w_o'][l].astype(BF16)
    w['peer_wq'] = jnp.transpose(P['peer_wq'][l], (1, 0, 2)).astype(BF16)
    w['peer_k1'] = P['peer_keys'][l][:, 0].astype(BF16)
    w['peer_k2'] = P['peer_keys'][l][:, 1].astype(BF16)
    w['peer_u'] = P['peer_u'][l].astype(BF16)
    w['peer_vt'] = P['peer_v'][l].T.astype(BF16)
    return w


def _mods(c, w_ada, b_ada):
    rows = c.shape[0]
    cp = _pad_to(c, 0, SUBLANES) if rows < SUBLANES else c
    mod = ada_mod(cp, w_ada, b_ada)[:rows]
    return jnp.split(mod, 6, axis=-1)


def _peer_block(h2, x1, gate2, w, tm, ni):
    thr, s2, e1, e2 = peer_topk(h2, w['peer_wq'], w['peer_k1'], w['peer_k2'], tm)
    return peer_dense(h2, w['peer_u'], w['peer_vt'], thr, e1, s2, e2, x1, gate2, tm, ni)


def _trunk_prompt(x, c, P, W):
    t = x.shape[0]
    tm = min(512, t)
    L = min(128, t)
    consts = _chunk_constants(L)
    tabs, tabs_t = _rope_tables(jnp.arange(t, dtype=jnp.int32))
    lat_rows, rope_rows, gla_st, c_st, n_st, m_st = [], [], [], [], [], []
    for l in range(len(W)):
        w = W[l]
        shift1, scale1, gate1, shift2, scale2, gate2 = _mods(c, P['w_ada'][l], P['b_ada'][l])
        z = norm_proj(x, P['norm_mix'][l], scale1, shift1, w['w_in'], min(2 * tm, t))
        qt, k, vt, lat, kr = mla_prep(z, w['gq'], w['gkv'], w['wqt'], w['wuk'], w['wuvt'], tabs_t, tabs, tm)
        a = flash_attention(qt, k, vt, tm)
        b, st = gla_scan(z, w['gla_wa'], w['gla_ba'], w['gla_norm'], consts, L)
        smt = z[:, Z_SMALL + SM_MI:Z_SMALL + SM_MI + 2 * ML_HEADS].T
        cc, ct, nrow, mrow = mlstm_scan(z, smt, w['ml_brow'], w['ml_bcol'], w['ml_norm'], consts, L)
        x1, h2 = merge(a, b, cc, z, x, gate1, w['wpa'], w['wpb'], w['wpc'], w['wo'],
                       P['norm_ffn'][l], scale2, shift2, tm)
        x = _peer_block(h2, x1, gate2, w, tm, SUBLANES)
        lat_rows.append(lat)
        rope_rows.append(kr)
        gla_st.append(jnp.stack([st[h][:, h * GLA_DK:(h + 1) * GLA_DK].T for h in range(GLA_HEADS)]))
        c_st.append(jnp.stack([ct[h][:, h * ML_DK:(h + 1) * ML_DK].T for h in range(ML_HEADS)]))
        n_st.append(jnp.stack([nrow[h, h * ML_DK:(h + 1) * ML_DK] for h in range(ML_HEADS)]))
        m_st.append(mrow[:ML_HEADS, 0])
    y = final_norm(x, P['norm_final'], tm)
    return y, tuple(jnp.stack(r) for r in (lat_rows, rope_rows, gla_st, c_st, n_st, m_st))


def _trunk_sample(x, c, P, W, cache_lat, cache_rope, page_table, s_gla, s_c, s_n, s_m):
    nb = x.shape[0]
    past = page_table.shape[1] * PAGE_SIZE
    tabs, _ = _rope_tables(jnp.full((1,), past, dtype=jnp.int32))
    G = min(16, page_table.shape[1])
    cache_rope = jnp.swapaxes(cache_rope, 2, 3)
    lat_rows, rope_rows, gla_st, c_st, n_st, m_st = [], [], [], [], [], []
    for l in range(len(W)):
        w = W[l]
        shift1, scale1, gate1, shift2, scale2, gate2 = _mods(c, P['w_ada'][l], P['b_ada'][l])
        z = norm_proj(x, P['norm_mix'][l], scale1, shift1, w['w_in'], nb)
        ql, qp, lat, kr = mla_prep_s(z, w['gq'], w['gkv'], w['wq'], w['wukt'], tabs)
        o_lat = mla_decode(page_table, jnp.transpose(ql, (1, 0, 2)), jnp.transpose(qp, (1, 0, 2)),
                           lat[:, None, :], kr[:, None, :], cache_lat, cache_rope, l, G)
        a = uv_proj(o_lat.reshape(nb, MLA_HEADS * KV_LORA), w['wuv'])
        m_in = jnp.zeros((nb, LANES), F32).at[:, SM_MI:SM_MI + ML_HEADS].set(s_m[l])
        b, cc, s_new, c_new, n_new, m_new = rec_step(
            z, z.T, w['gla_wa'], w['gla_ba'], w['gla_wat'], w['gla_bat'], w['ml_brow'], w['gla_norm'], w['ml_norm'],
            s_gla[l], s_c[l], s_n[l].reshape(nb, ML_HEADS * ML_DK), m_in)
        x1, h2 = merge(a, b, cc, z, x, gate1, w['wpa'], w['wpb'], w['wpc'], w['wo'],
                       P['norm_ffn'][l], scale2, shift2, nb)
        x = _peer_block(h2, x1, gate2, w, nb, SUBLANES)
        lat_rows.append(lat[:, None, :])
        rope_rows.append(kr[:, None, :])
        gla_st.append(s_new)
        c_st.append(c_new)
        n_st.append(n_new.reshape(nb, ML_HEADS, ML_DK))
        m_st.append(m_new[:, SM_MI:SM_MI + ML_HEADS])
    y = final_norm(x, P['norm_final'], nb)
    return y, tuple(jnp.stack(r) for r in (lat_rows, rope_rows, gla_st, c_st, n_st, m_st))


def kernel(x_prompt, x_sample, cache_kv_latent, cache_k_rope, state_gla, state_mlstm_C, state_mlstm_n, state_mlstm_m, page_table, c_prompt, c_sample, w_ada, b_ada, norm_mix, norm_ffn, norm_final, w_in, mla_q_norm, mla_kv_norm, w_uq, w_uk, w_uv, w_gla_a, b_gla_a, gla_norm, ml_b_i, ml_b_f, ml_norm, w_proj_a, w_proj_b, w_proj_c, w_o, peer_wq, peer_keys, peer_u, peer_v):
    P = {'w_ada': w_ada, 'b_ada': b_ada, 'norm_mix': norm_mix, 'norm_ffn': norm_ffn, 'norm_final': norm_final,
         'w_in': w_in, 'mla_q_norm': mla_q_norm, 'mla_kv_norm': mla_kv_norm, 'w_uq': w_uq, 'w_uk': w_uk,
         'w_uv': w_uv, 'w_gla_a': w_gla_a, 'b_gla_a': b_gla_a, 'gla_norm': gla_norm, 'ml_b_i': ml_b_i,
         'ml_b_f': ml_b_f, 'ml_norm': ml_norm, 'w_proj_a': w_proj_a, 'w_proj_b': w_proj_b,
         'w_proj_c': w_proj_c, 'w_o': w_o, 'peer_wq': peer_wq, 'peer_keys': peer_keys, 'peer_u': peer_u,
         'peer_v': peer_v}
    depth = w_in.shape[0]
    W = [_layer_weights(l, P) for l in range(depth)]
    bp, sp, d = x_prompt.shape
    assert bp == 1 and x_sample.shape[1] == 1
    nb = x_sample.shape[0]
    y_p, (lat_p, rope_p, gla_p, c_p, n_p, m_p) = _trunk_prompt(x_prompt.reshape(sp, d), c_prompt, P, W)
    y_s, (lat_s, rope_s, gla_s, c_s, n_s, m_s) = _trunk_sample(
        x_sample.reshape(nb, d), c_sample, P, W, cache_kv_latent, cache_k_rope, page_table,
        state_gla, state_mlstm_C, state_mlstm_n, state_mlstm_m)
    return (y_p.reshape(bp, sp, d), y_s.reshape(nb, 1, d),
            lat_p[:, None], rope_p[:, None], gla_p[:, None], c_p[:, None], n_p[:, None], m_p[:, None],
            lat_s, rope_s, gla_s, c_s, n_s, m_s)
```

```python
import functools

import numpy as np
import jax
import jax.numpy as jnp
from jax import lax
from jax.experimental import pallas as pl
from jax.experimental.pallas import tpu as pltpu

F32 = jnp.float32
BF16 = jnp.bfloat16

D_MODEL = 1024
PAGE_SIZE = 128
MLA_HEADS = 8
MLA_NOPE = 64
MLA_ROPE = 32
MLA_V = 64
Q_LORA = 256
KV_LORA = 128
ROPE_BASE = 10000.0
MLA_SCALE = (MLA_NOPE + MLA_ROPE) ** -0.5
GLA_HEADS = 4
GLA_DK = 64
GLA_DV = 128
GLA_GATE_RANK = 16
GLA_GATE_TEMP = 16.0
ML_HEADS = 4
ML_DK = 64
ML_DV = 128
PEER_HEADS = 8
PEER_KEYS = 128
PEER_HALF = 128
PEER_TOPK = 16
NORM_EPS = 1e-6

LANES = 128
SUBLANES = 8
VMEM_LIMIT = 48 * 1024 * 1024

NEG = -1e30
LOG2E = 1.4426950408889634

Z_GA, Z_GB, Z_GC = 0, 1024, 2048
Z_GV, Z_GG, Z_MV, Z_MO = 3072, 3584, 4096, 4608
Z_QC, Z_GQ, Z_GK, Z_MQ, Z_MK = 5120, 5376, 5632, 5888, 6144
Z_KVC, Z_SMALL = 6400, 6528
Z_WIDTH = 6656
SM_KPE, SM_GA, SM_MI, SM_MF = 0, 32, 48, 52


def _cp(*sem):
    return pltpu.CompilerParams(dimension_semantics=tuple(sem), vmem_limit_bytes=VMEM_LIMIT)


def _bdot(a, b):
    return jnp.dot(a.astype(BF16), b.astype(BF16), preferred_element_type=F32)


def _bdot_nt(a, b):
    return lax.dot_general(a.astype(BF16), b.astype(BF16), (((1,), (1,)), ((), ())),
                           preferred_element_type=F32)


def _split3(x):
    hi = x.astype(BF16)
    r = x - hi.astype(F32)
    mid = r.astype(BF16)
    lo = (r - mid.astype(F32)).astype(BF16)
    return hi, mid, lo


def _dot01(m01, x):
    hi, mid, lo = _split3(x)
    return (jnp.dot(m01, hi, preferred_element_type=F32) + jnp.dot(m01, mid, preferred_element_type=F32)
            + jnp.dot(m01, lo, preferred_element_type=F32))


def _dot01_r(x, m01):
    hi, mid, lo = _split3(x)
    return (jnp.dot(hi, m01, preferred_element_type=F32) + jnp.dot(mid, m01, preferred_element_type=F32)
            + jnp.dot(lo, m01, preferred_element_type=F32))


def _sigmoid(x):
    return 1.0 / (1.0 + jnp.exp(-x))


def _log_sigmoid(x):
    return jnp.minimum(x, 0.0) - jnp.log(1.0 + jnp.exp(-jnp.abs(x)))


def _rms(x):
    return x * lax.rsqrt(jnp.mean(x * x, axis=-1, keepdims=True) + NORM_EPS)


def _gelu_tanh(x):
    return 0.5 * x * (1.0 + jnp.tanh(0.7978845608028654 * (x + 0.044715 * (x * x * x))))


def _ada_kernel(c_ref, w_ref, b_ref, o_ref):
    c = c_ref[...]
    o_ref[...] = _bdot(c * _sigmoid(c), w_ref[...]) + b_ref[...]


def ada_mod(c, w, b):
    rows, d = c.shape
    n = w.shape[1]
    tn = 1536
    return pl.pallas_call(
        _ada_kernel,
        out_shape=jax.ShapeDtypeStruct((rows, n), F32),
        grid=(n // tn,),
        in_specs=[pl.BlockSpec((rows, d), lambda j: (0, 0)),
                  pl.BlockSpec((d, tn), lambda j: (0, j)),
                  pl.BlockSpec((1, tn), lambda j: (0, j))],
        out_specs=pl.BlockSpec((rows, tn), lambda j: (0, j)),
        compiler_params=_cp("arbitrary"),
        name="ada_mod",
    )(c, w, b.reshape(1, n))


def _norm_proj_kernel(x_ref, g_ref, sc_ref, sh_ref, w_ref, o_ref, hn_ref):
    @pl.when(pl.program_id(1) == 0)
    def _():
        h = _rms(x_ref[...]) * g_ref[...] * (1.0 + sc_ref[...]) + sh_ref[...]
        hn_ref[...] = h.astype(BF16)

    o_ref[...] = jnp.dot(hn_ref[...], w_ref[...], preferred_element_type=F32)


def norm_proj(x, g, scale, shift, w, tm):
    t, d = x.shape
    n = w.shape[1]
    tn = 512
    per_tok = scale.shape[0] != 1
    mod_spec = (pl.BlockSpec((tm, d), lambda i, j: (i, 0)) if per_tok
                else pl.BlockSpec((1, d), lambda i, j: (0, 0)))
    return pl.pallas_call(
        _norm_proj_kernel,
        out_shape=jax.ShapeDtypeStruct((t, n), F32),
        grid=(t // tm, n // tn),
        in_specs=[pl.BlockSpec((tm, d), lambda i, j: (i, 0)),
                  pl.BlockSpec((1, d), lambda i, j: (0, 0)),
                  mod_spec, mod_spec,
                  pl.BlockSpec((d, tn), lambda i, j: (0, j))],
        out_specs=pl.BlockSpec((tm, tn), lambda i, j: (i, j)),
        scratch_shapes=[pltpu.VMEM((tm, d), BF16)],
        compiler_params=_cp("arbitrary", "arbitrary"),
        name="norm_proj",
    )(x, g.reshape(1, d), scale, shift, w)


def _rope_lanes(x, c, sa, sb):
    return x * c + pltpu.roll(x, LANES - 16, 1) * sa + pltpu.roll(x, 16, 1) * sb


def _mla_prep_kernel(qc_ref, kvc_ref, sm_ref, gq_ref, gkv_ref, wqt_ref, wuk_ref, wuvt_ref,
                     ct_ref, st_ref, ck_ref, ska_ref, skb_ref,
                     qt_out, k_out, vt_out, lat_out, kr_out):
    half = MLA_ROPE // 2
    qn = (_rms(qc_ref[...]) * gq_ref[...]).astype(BF16)
    lat = _rms(kvc_ref[...]) * gkv_ref[...]
    lat_out[...] = lat
    latb = lat.astype(BF16)
    krf = _rope_lanes(sm_ref[...], ck_ref[...], ska_ref[...], skb_ref[...])
    kr_out[...] = krf[:, :MLA_ROPE]
    kr_sh = pltpu.roll(krf, MLA_NOPE, 1)
    cos_t, sin_t = ct_ref[...], st_ref[...]
    tm = qn.shape[0]
    ones_row = (lax.broadcasted_iota(jnp.int32, (LANES, tm), 0) == MLA_V).astype(F32)
    pad = jnp.zeros((LANES - MLA_NOPE - MLA_ROPE, tm), F32)
    for h in range(MLA_HEADS):
        qt = lax.dot_general(wqt_ref[h], qn, (((1,), (1,)), ((), ())), preferred_element_type=F32)
        x1 = qt[MLA_NOPE:MLA_NOPE + half]
        x2 = qt[MLA_NOPE + half:MLA_NOPE + MLA_ROPE]
        qt = jnp.concatenate([qt[:MLA_NOPE], x1 * cos_t - x2 * sin_t, x1 * sin_t + x2 * cos_t, pad], axis=0)
        qt_out[h] = (qt * (MLA_SCALE * LOG2E)).astype(BF16)
        k_out[h] = (jnp.dot(latb, wuk_ref[h], preferred_element_type=F32) + kr_sh).astype(BF16)
        vt = lax.dot_general(wuvt_ref[h], latb, (((1,), (1,)), ((), ())), preferred_element_type=F32)
        vt_out[h] = (vt + ones_row).astype(BF16)


def mla_prep(z, gq, gkv, wqt, wuk, wuvt, tabs_t, tabs, tm):
    t = z.shape[0]
    hh = MLA_HEADS
    row = lambda w, off: pl.BlockSpec((tm, w), lambda i, o=off // w: (i, o))
    const2 = lambda a: pl.BlockSpec(a.shape, lambda i: (0, 0))
    const3 = lambda a: pl.BlockSpec(a.shape, lambda i: (0, 0, 0))
    tab_spec = pl.BlockSpec((tm, LANES), lambda i: (i, 0))
    tabt_spec = pl.BlockSpec((MLA_ROPE // 2, tm), lambda i: (0, i))
    return pl.pallas_call(
        _mla_prep_kernel,
        out_shape=(jax.ShapeDtypeStruct((hh, LANES, t), BF16),
                   jax.ShapeDtypeStruct((hh, t, LANES), BF16),
                   jax.ShapeDtypeStruct((hh, LANES, t), BF16),
                   jax.ShapeDtypeStruct((t, KV_LORA), F32),
                   jax.ShapeDtypeStruct((t, MLA_ROPE), F32)),
        grid=(t // tm,),
        in_specs=[row(Q_LORA, Z_QC), row(KV_LORA, Z_KVC), row(LANES, Z_SMALL),
                  const2(gq), const2(gkv), const3(wqt), const3(wuk), const3(wuvt),
                  tabt_spec, tabt_spec, tab_spec, tab_spec, tab_spec],
        out_specs=(pl.BlockSpec((hh, LANES, tm), lambda i: (0, 0, i)),
                   pl.BlockSpec((hh, tm, LANES), lambda i: (0, i, 0)),
                   pl.BlockSpec((hh, LANES, tm), lambda i: (0, 0, i)),
                   pl.BlockSpec((tm, KV_LORA), lambda i: (i, 0)),
                   pl.BlockSpec((tm, MLA_ROPE), lambda i: (i, 0))),
        compiler_params=_cp("arbitrary"),
        name="mla_prep",
    )(z, z, z, gq, gkv, wqt, wuk, wuvt, *tabs_t, *tabs[3:])


def _flash_kernel(qt_ref, k_ref, vt_ref, o_ref, m_ref, acc_ref, *, tq):
    qi = pl.program_id(0)
    ki = pl.program_id(1)

    @pl.when(ki == 0)
    def _():
        m_ref[...] = jnp.full(m_ref.shape, NEG, F32)
        acc_ref[...] = jnp.zeros(acc_ref.shape, F32)

    def scores(h):
        return jnp.dot(k_ref[h], qt_ref[h], preferred_element_type=F32)

    def accumulate(h, p, alpha):
        acc_ref[h] = alpha * acc_ref[h] + jnp.dot(vt_ref[h], p, preferred_element_type=F32)

    def step(diag):
        if diag:
            key = lax.broadcasted_iota(jnp.int32, (tq, tq), 0)
            qry = lax.broadcasted_iota(jnp.int32, (tq, tq), 1)
            keep = key <= qry
        s_next = scores(0)
        pending = None
        for h in range(MLA_HEADS):
            s = s_next
            if h + 1 < MLA_HEADS:
                s_next = scores(h + 1)
            if pending is not None:
                accumulate(*pending)
            if diag:
                s = jnp.where(keep, s, NEG)
            m_prev = m_ref[h:h + 1, :]
            m_new = jnp.maximum(m_prev, jnp.max(s, axis=0, keepdims=True))
            p = jnp.exp2(s - m_new).astype(BF16)
            m_ref[h:h + 1, :] = m_new
            pending = (h, p, jnp.exp2(m_prev - m_new))
        accumulate(*pending)

    @pl.when(ki < qi)
    def _():
        step(False)

    @pl.when(ki == qi)
    def _():
        step(True)
        for h in range(MLA_HEADS):
            acc = acc_ref[h]
            o_ref[:, h * LANES:(h + 1) * LANES] = (acc / acc[MLA_V:MLA_V + 1, :]).T.astype(BF16)


def flash_attention(qt, k, vt, tq):
    hh, t, _ = k.shape
    n = t // tq
    return pl.pallas_call(
        functools.partial(_flash_kernel, tq=tq),
        out_shape=jax.ShapeDtypeStruct((t, hh * LANES), BF16),
        grid=(n, n),
        in_specs=[pl.BlockSpec((hh, LANES, tq), lambda i, j: (0, 0, i)),
                  pl.BlockSpec((hh, tq, LANES), lambda i, j: (0, jnp.minimum(i, j), 0)),
                  pl.BlockSpec((hh, LANES, tq), lambda i, j: (0, 0, jnp.minimum(i, j)))],
        out_specs=pl.BlockSpec((tq, hh * LANES), lambda i, j: (i, 0)),
        scratch_shapes=[pltpu.VMEM((hh, tq), F32), pltpu.VMEM((hh, LANES, tq), F32)],
        compiler_params=_cp("arbitrary", "arbitrary"),
        name="flash_attention",
    )(qt, k, vt)


def _chunk_constants(L):
    nl = int(np.log2(L))
    assert 2 ** nl == L
    idx = np.arange(L)
    tri = (idx[None, :] <= idx[:, None]).astype(np.float32)
    mq, mk, pm = [], [], []
    for lev in range(nl):
        p = 1 << lev
        start_right = (idx >> (lev + 1) << (lev + 1)) + p
        right = ((idx >> lev) & 1) == 1
        u = idx[None, :]
        mq.append((right[:, None] & (u >= start_right[:, None]) & (u <= idx[:, None])).astype(np.float32))
        mk.append(((~right)[:, None] & (u > idx[:, None]) & (u < start_right[:, None])).astype(np.float32))
        same = (idx[:, None] >> (lev + 1)) == (idx[None, :] >> (lev + 1))
        pm.append((same & right[:, None] & (~right)[None, :]).astype(np.float32))
    pm.append(np.eye(L, dtype=np.float32))
    sums = np.concatenate([tri] + mq + mk, axis=0)
    return jnp.asarray(sums, BF16), jnp.asarray(np.stack(pm), F32), jnp.asarray(tri, BF16), jnp.asarray(tri.T, BF16)


def _head_mask(width, heads, h):
    lane = lax.broadcasted_iota(jnp.int32, (1, width), 1)
    per = width // heads
    return ((lane >= h * per) & (lane < (h + 1) * per)).astype(F32)


def _gla_scan_kernel(gq_ref, gk_ref, gv_ref, gg_ref, sm_ref, wa_ref, ba_ref, sums_ref, pm_ref, gn_ref,
                     o_ref, st_out, st_ref, *, L, nl):
    step = pl.program_id(0)

    @pl.when(step == 0)
    def _():
        st_ref[...] = jnp.zeros(st_ref.shape, F32)

    width = GLA_HEADS * GLA_DK
    x = _bdot(sm_ref[...], wa_ref[...]) + ba_ref[...]
    la = _log_sigmoid(x) * (1.0 / GLA_GATE_TEMP)
    dsum = _dot01(sums_ref[...], la)
    b = dsum[0:L]
    b_last = b[L - 1:L, :]
    q = gq_ref[...] * (GLA_DK ** -0.5)
    k = gk_ref[...]
    q_in = (q * jnp.exp(b)).astype(BF16)
    k_out = k * jnp.exp(b_last - b)
    decay_all = jnp.exp(b_last)
    qs = [(q * jnp.exp(dsum[(1 + lev) * L:(2 + lev) * L])) for lev in range(nl)] + [q]
    ks = [(k * jnp.exp(dsum[(1 + nl + lev) * L:(2 + nl + lev) * L])).astype(BF16) for lev in range(nl)]
    ks = ks + [k.astype(BF16)]
    for h in range(GLA_HEADS):
        hm = _head_mask(width, GLA_HEADS, h)
        att = jnp.zeros((L, L), F32)
        for lev in range(nl + 1):
            sc = lax.dot_general((qs[lev] * hm).astype(BF16), ks[lev], (((1,), (1,)), ((), ())),
                                 preferred_element_type=F32)
            att = att + jnp.where(pm_ref[lev] > 0.0, sc, 0.0)
        vh = gv_ref[:, h * GLA_DV:(h + 1) * GLA_DV]
        st = st_ref[h]
        o = (jnp.dot(att.astype(BF16), vh.astype(BF16), preferred_element_type=F32)
             + lax.dot_general(q_in, st.astype(BF16), (((1,), (1,)), ((), ())), preferred_element_type=F32))
        st_ref[h] = st * decay_all + jnp.dot(vh.T.astype(BF16), (k_out * hm).astype(BF16),
                                             preferred_element_type=F32)
        gg = gg_ref[:, h * GLA_DV:(h + 1) * GLA_DV]
        y = _rms(o) * gn_ref[:, h * GLA_DV:(h + 1) * GLA_DV] * (gg * _sigmoid(gg))
        o_ref[:, h * GLA_DV:(h + 1) * GLA_DV] = y.astype(BF16)

    @pl.when(step == pl.num_programs(0) - 1)
    def _():
        st_out[...] = st_ref[...]


def gla_scan(z, wa, ba, gn, consts, L):
    t = z.shape[0]
    sums, pm, _, _ = consts
    nl = pm.shape[0] - 1
    width = GLA_HEADS * GLA_DK
    vw = GLA_HEADS * GLA_DV
    row = lambda w, off: pl.BlockSpec((L, w), lambda i, o=off // w: (i, o))
    c2 = lambda a: pl.BlockSpec(a.shape, lambda i: (0, 0))
    c3 = lambda a: pl.BlockSpec(a.shape, lambda i: (0, 0, 0))
    return pl.pallas_call(
        functools.partial(_gla_scan_kernel, L=L, nl=nl),
        out_shape=(jax.ShapeDtypeStruct((t, vw), BF16),
                   jax.ShapeDtypeStruct((GLA_HEADS, GLA_DV, width), F32)),
        grid=(t // L,),
        in_specs=[row(width, Z_GQ), row(width, Z_GK), row(vw, Z_GV), row(vw, Z_GG), row(LANES, Z_SMALL),
                  c2(wa), c2(ba), c2(sums), c3(pm), c2(gn)],
        out_specs=(pl.BlockSpec((L, vw), lambda i: (i, 0)),
                   pl.BlockSpec((GLA_HEADS, GLA_DV, width), lambda i: (0, 0, 0))),
        scratch_shapes=[pltpu.VMEM((GLA_HEADS, GLA_DV, width), F32)],
        compiler_params=_cp("arbitrary"),
        name="gla_scan",
    )(z, z, z, z, z, wa, ba, sums, pm, gn)


def _mlstm_scan_kernel(mq_ref, mk_ref, mv_ref, mo_ref, sm_ref, smt_ref, brow_ref, bcol_ref, tri_ref, trit_ref,
                       nrm_ref, o_ref, ct_out, n_out, m_out, ct_ref, n_ref, m_ref, *, L):
    step = pl.program_id(0)

    @pl.when(step == 0)
    def _():
        ct_ref[...] = jnp.zeros(ct_ref.shape, F32)
        n_ref[...] = jnp.zeros(n_ref.shape, F32)
        m_ref[...] = jnp.zeros(m_ref.shape, F32)

    width = ML_HEADS * ML_DK
    pre_c = sm_ref[...] + brow_ref[...]
    pre_r = smt_ref[...] + bcol_ref[...]
    b_c_all = _dot01(tri_ref[...], _log_sigmoid(pre_c))
    b_r_all = _dot01_r(_log_sigmoid(pre_r), trit_ref[...])
    q = mq_ref[...]
    k = mk_ref[...] * (ML_DK ** -0.5)
    kb = k.astype(BF16)
    row = lax.broadcasted_iota(jnp.int32, (L, L), 0)
    col = lax.broadcasted_iota(jnp.int32, (L, L), 1)
    causal = col <= row
    for h in range(ML_HEADS):
        hm = _head_mask(width, ML_HEADS, h)
        bc = b_c_all[:, SM_MF + h:SM_MF + h + 1]
        br = b_r_all[ML_HEADS + h:ML_HEADS + h + 1, :]
        ic = pre_c[:, SM_MI + h:SM_MI + h + 1]
        ir = pre_r[h:h + 1, :]
        m_prev = m_ref[h:h + 1, 0:1]
        dlog = jnp.where(causal, bc - br + ir, NEG)
        inter = bc + m_prev
        mt = jnp.maximum(inter, jnp.max(dlog, axis=1, keepdims=True))
        qh = (q * hm).astype(BF16)
        qk = lax.dot_general(qh, kb, (((1,), (1,)), ((), ())), preferred_element_type=F32)
        wi = jnp.exp(dlog - mt) * qk
        wint = jnp.exp(inter - mt)
        vh = mv_ref[:, h * ML_DV:(h + 1) * ML_DV]
        ct = ct_ref[h]
        nrow = n_ref[h:h + 1, :]
        num = (jnp.dot(wi.astype(BF16), vh.astype(BF16), preferred_element_type=F32)
               + wint * lax.dot_general(qh, ct.astype(BF16), (((1,), (1,)), ((), ())),
                                        preferred_element_type=F32))
        nq = jnp.sum(wi, axis=1, keepdims=True) + wint * jnp.sum(q * nrow, axis=1, keepdims=True)
        hh = num / jnp.maximum(jnp.abs(nq), jnp.exp(-mt))
        m_new = mt[L - 1:L, :]
        b_last = bc[L - 1:L, :]
        keep = jnp.exp(b_last + m_prev - m_new)
        wn = jnp.exp(b_last - bc + ic - m_new)
        kw = k * hm * wn
        ct_ref[h] = keep * ct + jnp.dot(vh.T.astype(BF16), kw.astype(BF16), preferred_element_type=F32)
        n_ref[h:h + 1, :] = keep * nrow + jnp.sum(kw, axis=0, keepdims=True)
        m_ref[h:h + 1, :] = jnp.broadcast_to(m_new, (1, LANES))
        mo = mo_ref[:, h * ML_DV:(h + 1) * ML_DV]
        y = _rms(hh) * nrm_ref[:, h * ML_DV:(h + 1) * ML_DV] * _sigmoid(mo)
        o_ref[:, h * ML_DV:(h + 1) * ML_DV] = y.astype(BF16)

    @pl.when(step == pl.num_programs(0) - 1)
    def _():
        ct_out[...] = ct_ref[...]
        n_out[...] = n_ref[...]
        m_out[...] = m_ref[...]


def mlstm_scan(z, smt, brow, bcol, nrm, consts, L):
    t = z.shape[0]
    _, _, tri, trit = consts
    width = ML_HEADS * ML_DK
    vw = ML_HEADS * ML_DV
    row = lambda w, off: pl.BlockSpec((L, w), lambda i, o=off // w: (i, o))
    c2 = lambda a: pl.BlockSpec(a.shape, lambda i: (0, 0))
    st_shapes = [((ML_HEADS, ML_DV, width), lambda i: (0, 0, 0)), ((SUBLANES, width), lambda i: (0, 0)),
                 ((SUBLANES, LANES), lambda i: (0, 0))]
    return pl.pallas_call(
        functools.partial(_mlstm_scan_kernel, L=L),
        out_shape=(jax.ShapeDtypeStruct((t, vw), BF16),) + tuple(
            jax.ShapeDtypeStruct(s, F32) for s, _ in st_shapes),
        grid=(t // L,),
        in_specs=[row(width, Z_MQ), row(width, Z_MK), row(vw, Z_MV), row(vw, Z_MO), row(LANES, Z_SMALL),
                  pl.BlockSpec((SUBLANES, L), lambda i: (0, i)),
                  c2(brow), c2(bcol), c2(tri), c2(trit), c2(nrm)],
        out_specs=(pl.BlockSpec((L, vw), lambda i: (i, 0)),) + tuple(pl.BlockSpec(s, f) for s, f in st_shapes),
        scratch_shapes=[pltpu.VMEM(s, F32) for s, _ in st_shapes],
        compiler_params=_cp("arbitrary"),
        name="mlstm_scan",
    )(z, z, z, z, z, smt, brow, bcol, tri, trit, nrm)


def _rec_step_kernel(gq_ref, gk_ref, gv_ref, gg_ref, mq_ref, mk_ref, mv_ref, mo_ref, sm_ref,
                     gqt_ref, gkt_ref, mqt_ref, mkt_ref, smt_ref,
                     wa_ref, ba_ref, wat_ref, bat_ref, brow_ref, gn_ref, mn_ref,
                     s_ref, c_ref, n_ref, m_ref,
                     bo_ref, co_ref, s_out, c_out, n_out, m_out, *, rows):
    blk = pl.program_id(0)
    nb = gqt_ref.shape[1]
    gw = GLA_HEADS * GLA_DK
    mw = ML_HEADS * ML_DK

    pre = sm_ref[...] + brow_ref[...]
    i_v = pre
    f_v = _log_sigmoid(pltpu.roll(pre, LANES - (SM_MF - SM_MI), 1))
    m_prev = m_ref[...]
    mt = jnp.maximum(f_v + m_prev, i_v)
    w_intra = jnp.exp(i_v - mt)
    w_inter = jnp.exp(f_v + m_prev - mt)
    inv_floor = jnp.exp(-mt)
    m_out[...] = mt
    mq = mq_ref[...]
    mk = mk_ref[...] * (ML_DK ** -0.5)
    n_old = n_ref[...]
    gq = gq_ref[...] * (GLA_DK ** -0.5)
    gk = gk_ref[...]
    qk_g, qk_m, qn_m, den = [], [], [], []
    keep_b = jnp.zeros((rows, mw), F32)
    wnew_b = jnp.zeros((rows, mw), F32)
    for h in range(ML_HEADS):
        hm = _head_mask(mw, ML_HEADS, h)
        qk_m.append(jnp.sum(mq * mk * hm, axis=1, keepdims=True))
        qn_m.append(jnp.sum(mq * n_old * hm, axis=1, keepdims=True))
        wi_h = w_intra[:, SM_MI + h:SM_MI + h + 1]
        we_h = w_inter[:, SM_MI + h:SM_MI + h + 1]
        nq = wi_h * qk_m[h] + we_h * qn_m[h]
        den.append(jnp.maximum(jnp.abs(nq), inv_floor[:, SM_MI + h:SM_MI + h + 1]))
        keep_b = keep_b + hm * we_h
        wnew_b = wnew_b + hm * wi_h
    n_out[...] = keep_b * n_old + wnew_b * mk
    for h in range(GLA_HEADS):
        qk_g.append(jnp.sum(gq * gk * _head_mask(gw, GLA_HEADS, h), axis=1, keepdims=True))

    xt = _bdot(wat_ref[...], smt_ref[...]) + bat_ref[...]
    at = jnp.exp(_log_sigmoid(xt) * (1.0 / GLA_GATE_TEMP))
    seq_lane = lax.broadcasted_iota(jnp.int32, (nb, LANES), 0)

    for r in range(rows):
        onehot = (seq_lane == blk * rows + r).astype(BF16)
        a_bc = _dot01_r(at, onehot)
        gk_bc = _dot01_r(gkt_ref[...], onehot)
        gq_bc = _dot01_r(gqt_ref[...], onehot) * (GLA_DK ** -0.5)
        mk_bc = _dot01_r(mkt_ref[...], onehot) * (ML_DK ** -0.5)
        mq_bc = _dot01_r(mqt_ref[...], onehot)
        for h in range(GLA_HEADS):
            sl = slice(h * GLA_DK, (h + 1) * GLA_DK)
            v_row = gv_ref[r:r + 1, h * GLA_DV:(h + 1) * GLA_DV]
            s_new = a_bc[sl] * s_ref[r, h] + gk_bc[sl] * v_row
            s_out[r, h] = s_new
            o = jnp.sum(gq_bc[sl] * s_new, axis=0, keepdims=True)
            gg = gg_ref[r:r + 1, h * GLA_DV:(h + 1) * GLA_DV]
            y = _rms(o) * gn_ref[:, h * GLA_DV:(h + 1) * GLA_DV] * (gg * _sigmoid(gg))
            bo_ref[r:r + 1, h * GLA_DV:(h + 1) * GLA_DV] = y.astype(BF16)
        for h in range(ML_HEADS):
            sl = slice(h * ML_DK, (h + 1) * ML_DK)
            v_row = mv_ref[r:r + 1, h * ML_DV:(h + 1) * ML_DV]
            c_old = c_ref[r, h]
            wi_s = w_intra[r:r + 1, SM_MI + h:SM_MI + h + 1]
            we_s = w_inter[r:r + 1, SM_MI + h:SM_MI + h + 1]
            num = (wi_s * qk_m[h][r:r + 1, :]) * v_row + we_s * jnp.sum(mq_bc[sl] * c_old, axis=0, keepdims=True)
            hh = num / den[h][r:r + 1, :]
            c_out[r, h] = we_s * c_old + wi_s * (mk_bc[sl] * v_row)
            mo = mo_ref[r:r + 1, h * ML_DV:(h + 1) * ML_DV]
            y = _rms(hh) * mn_ref[:, h * ML_DV:(h + 1) * ML_DV] * _sigmoid(mo)
            co_ref[r:r + 1, h * ML_DV:(h + 1) * ML_DV] = y.astype(BF16)


def rec_step(z, zt, wa, ba, wat, bat, brow, gn, mn, s0, c0, n0, m0):
    nb = z.shape[0]
    rows = SUBLANES
    gw = GLA_HEADS * GLA_DK
    mw = ML_HEADS * ML_DK
    gvw = GLA_HEADS * GLA_DV
    mvw = ML_HEADS * ML_DV
    row = lambda w, off: pl.BlockSpec((rows, w), lambda i, o=off // w: (i, o))
    colb = lambda w, off: pl.BlockSpec((w, nb), lambda i, o=off // w: (o, 0))
    c2 = lambda a: pl.BlockSpec(a.shape, lambda i: (0, 0))
    st4 = lambda hds, dk, dv: pl.BlockSpec((rows, hds, dk, dv), lambda i: (i, 0, 0, 0))
    return pl.pallas_call(
        functools.partial(_rec_step_kernel, rows=rows),
        out_shape=(jax.ShapeDtypeStruct((nb, gvw), BF16), jax.ShapeDtypeStruct((nb, mvw), BF16),
                   jax.ShapeDtypeStruct(s0.shape, F32), jax.ShapeDtypeStruct(c0.shape, F32),
                   jax.ShapeDtypeStruct((nb, mw), F32), jax.ShapeDtypeStruct((nb, LANES), F32)),
        grid=(nb // rows,),
        in_specs=[row(gw, Z_GQ), row(gw, Z_GK), row(gvw, Z_GV), row(gvw, Z_GG),
                  row(mw, Z_MQ), row(mw, Z_MK), row(mvw, Z_MV), row(mvw, Z_MO), row(LANES, Z_SMALL),
                  colb(gw, Z_GQ), colb(gw, Z_GK), colb(mw, Z_MQ), colb(mw, Z_MK), colb(LANES, Z_SMALL),
                  c2(wa), c2(ba), c2(wat), c2(bat), c2(brow), c2(gn), c2(mn),
                  st4(GLA_HEADS, GLA_DK, GLA_DV), st4(ML_HEADS, ML_DK, ML_DV),
                  pl.BlockSpec((rows, mw), lambda i: (i, 0)), pl.BlockSpec((rows, LANES), lambda i: (i, 0))],
        out_specs=(pl.BlockSpec((rows, gvw), lambda i: (i, 0)), pl.BlockSpec((rows, mvw), lambda i: (i, 0)),
                   st4(GLA_HEADS, GLA_DK, GLA_DV), st4(ML_HEADS, ML_DK, ML_DV),
                   pl.BlockSpec((rows, mw), lambda i: (i, 0)), pl.BlockSpec((rows, LANES), lambda i: (i, 0))),
        compiler_params=_cp("arbitrary"),
        name="rec_step",
    )(z, z, z, z, z, z, z, z, z, zt, zt, zt, zt, zt, wa, ba, wat, bat, brow, gn, mn, s0, c0, n0, m0)


def _mla_prep_s_kernel(qc_ref, kvc_ref, sm_ref, gq_ref, gkv_ref, wq_ref, wukt_ref,
                       cq_ref, sqa_ref, sqb_ref, ck_ref, ska_ref, skb_ref,
                       ql_out, qp_out, lat_out, kr_out):
    qn = (_rms(qc_ref[...]) * gq_ref[...]).astype(BF16)
    lat_out[...] = _rms(kvc_ref[...]) * gkv_ref[...]
    krf = _rope_lanes(sm_ref[...], ck_ref[...], ska_ref[...], skb_ref[...])
    kr_out[...] = krf[:, :MLA_ROPE]
    cq, sqa, sqb = cq_ref[...], sqa_ref[...], sqb_ref[...]
    for h in range(MLA_HEADS):
        qh = _rope_lanes(jnp.dot(qn, wq_ref[h], preferred_element_type=F32), cq, sqa, sqb)
        ql_out[h] = _bdot(qh, wukt_ref[h])
        qp_out[h] = pltpu.roll(qh, LANES - MLA_NOPE, 1)


def mla_prep_s(z, gq, gkv, wq, wukt, tabs):
    nb = z.shape[0]
    hh = MLA_HEADS
    row = lambda w, off: pl.BlockSpec((nb, w), lambda i, o=off // w: (0, o))
    c2 = lambda a: pl.BlockSpec(a.shape, lambda i: (0, 0))
    c3 = lambda a: pl.BlockSpec(a.shape, lambda i: (0, 0, 0))
    return pl.pallas_call(
        _mla_prep_s_kernel,
        out_shape=(jax.ShapeDtypeStruct((hh, nb, LANES), F32), jax.ShapeDtypeStruct((hh, nb, LANES), F32),
                   jax.ShapeDtypeStruct((nb, KV_LORA), F32), jax.ShapeDtypeStruct((nb, MLA_ROPE), F32)),
        grid=(1,),
        in_specs=[row(Q_LORA, Z_QC), row(KV_LORA, Z_KVC), row(LANES, Z_SMALL),
                  c2(gq), c2(gkv), c3(wq), c3(wukt)] + [c2(t) for t in tabs],
        out_specs=(pl.BlockSpec((hh, nb, LANES), lambda i: (0, 0, 0)),
                   pl.BlockSpec((hh, nb, LANES), lambda i: (0, 0, 0)),
                   pl.BlockSpec((nb, KV_LORA), lambda i: (0, 0)), pl.BlockSpec((nb, MLA_ROPE), lambda i: (0, 0))),
        compiler_params=_cp("arbitrary"),
        name="mla_prep_sample",
    )(z, z, z, gq, gkv, wq, wukt, *tabs)


def _mla_decode_kernel(pt_ref, ql_ref, qp_ref, ln_ref, kn_ref, lat_hbm, kr_hbm, o_ref, lat_buf, kr_buf, sem,
                       *, layer, n_pages):
    b = pl.program_id(0)
    slot = lax.rem(b, 2)

    def page_copies(seq, sl, p):
        page = pt_ref[seq, p]
        off = pl.multiple_of(p * PAGE_SIZE, PAGE_SIZE)
        return (pltpu.make_async_copy(lat_hbm.at[layer, page], lat_buf.at[sl, pl.ds(off, PAGE_SIZE), :],
                                      sem.at[0, sl]),
                pltpu.make_async_copy(kr_hbm.at[layer, page], kr_buf.at[sl, :, pl.ds(off, PAGE_SIZE)],
                                      sem.at[1, sl]))

    def start_all(seq, sl):
        def body(p, carry):
            for c in page_copies(seq, sl, p):
                c.start()
            return carry
        lax.fori_loop(0, n_pages, body, 0)

    def wait_all(seq, sl):
        def body(p, carry):
            for c in page_copies(seq, sl, p):
                c.wait()
            return carry
        lax.fori_loop(0, n_pages, body, 0)

    @pl.when(b == 0)
    def _():
        start_all(0, 0)

    @pl.when(b + 1 < pl.num_programs(0))
    def _():
        start_all(b + 1, 1 - slot)

    wait_all(b, slot)

    ql = ql_ref[0]
    qp = qp_ref[0][:, :MLA_ROPE]
    lat = lat_buf[slot]
    latb = lat.astype(BF16)
    s = (lax.dot_general(ql.astype(BF16), latb, (((1,), (1,)), ((), ())), preferred_element_type=F32)
         + jnp.dot(qp.astype(BF16), kr_buf[slot].astype(BF16), preferred_element_type=F32)) * MLA_SCALE
    lat_new = ln_ref[0]
    s_new = (jnp.sum(ql * lat_new, axis=1, keepdims=True)
             + jnp.sum(qp * kn_ref[0], axis=1, keepdims=True)) * MLA_SCALE
    m = jnp.maximum(jnp.max(s, axis=1, keepdims=True), s_new)
    p = jnp.exp(s - m)
    pn = jnp.exp(s_new - m)
    num = jnp.dot(p.astype(BF16), latb, preferred_element_type=F32) + pn * lat_new
    o_ref[0] = num / (jnp.sum(p, axis=1, keepdims=True) + pn)


def mla_decode(page_table, ql, qp, lat_new, kr_new, cache_lat, cache_rope_t, layer):
    nb, n_pages = page_table.shape
    past = n_pages * PAGE_SIZE
    hh = MLA_HEADS
    seq = lambda w: pl.BlockSpec((1, hh, w), lambda b, pt: (b, 0, 0))
    new = lambda w: pl.BlockSpec((1, 1, w), lambda b, pt: (b, 0, 0))
    return pl.pallas_call(
        functools.partial(_mla_decode_kernel, layer=layer, n_pages=n_pages),
        out_shape=jax.ShapeDtypeStruct((nb, hh, KV_LORA), F32),
        grid_spec=pltpu.PrefetchScalarGridSpec(
            num_scalar_prefetch=1,
            grid=(nb,),
            in_specs=[seq(LANES), seq(LANES), new(KV_LORA), new(MLA_ROPE),
                      pl.BlockSpec(memory_space=pl.ANY), pl.BlockSpec(memory_space=pl.ANY)],
            out_specs=pl.BlockSpec((1, hh, KV_LORA), lambda b, pt: (b, 0, 0)),
            scratch_shapes=[pltpu.VMEM((2, past, KV_LORA), F32), pltpu.VMEM((2, MLA_ROPE, past), F32),
                            pltpu.SemaphoreType.DMA((2, 2))]),
        compiler_params=_cp("arbitrary"),
        name="mla_decode",
    )(page_table, ql, qp, lat_new, kr_new, cache_lat, cache_rope_t)


def _uv_proj_kernel(o_ref, w_ref, a_ref):
    a_ref[...] = _bdot(o_ref[...], w_ref[0]).astype(BF16)


def uv_proj(o_lat, wuv):
    nb = o_lat.shape[0]
    return pl.pallas_call(
        _uv_proj_kernel,
        out_shape=jax.ShapeDtypeStruct((nb, MLA_HEADS * LANES), BF16),
        grid=(MLA_HEADS,),
        in_specs=[pl.BlockSpec((nb, KV_LORA), lambda h: (0, h)),
                  pl.BlockSpec((1, KV_LORA, LANES), lambda h: (h, 0, 0))],
        out_specs=pl.BlockSpec((nb, LANES), lambda h: (0, h)),
        compiler_params=_cp("arbitrary"),
        name="uv_proj",
    )(o_lat, wuv)


def _merge_kernel(a_ref, b_ref, c_ref, ga_ref, gb_ref, gc_ref, x_ref, g1_ref, wa_ref, wb_ref, wc_ref, wo_ref,
                  nf_ref, sc_ref, sh_ref, x1_ref, h2_ref):
    merged = (_sigmoid(ga_ref[...]) * jnp.dot(a_ref[...], wa_ref[...], preferred_element_type=F32)
              + _sigmoid(gb_ref[...]) * jnp.dot(b_ref[...], wb_ref[...], preferred_element_type=F32)
              + _sigmoid(gc_ref[...]) * jnp.dot(c_ref[...], wc_ref[...], preferred_element_type=F32))
    mix = _bdot(merged, wo_ref[...])
    x1 = x_ref[...] + g1_ref[...] * mix
    x1_ref[...] = x1
    h2 = _rms(x1) * nf_ref[...] * (1.0 + sc_ref[...]) + sh_ref[...]
    h2_ref[...] = h2.astype(BF16)


def merge(a, b, c, z, x, gate1, wa, wb, wc, wo, nf, scale2, shift2, tm):
    t, d = x.shape
    per_tok = gate1.shape[0] != 1
    mod_spec = (pl.BlockSpec((tm, d), lambda i: (i, 0)) if per_tok else pl.BlockSpec((1, d), lambda i: (0, 0)))
    rowb = lambda w, o=0: pl.BlockSpec((tm, w), lambda i, o=o: (i, o))
    c2 = lambda arr: pl.BlockSpec(arr.shape, lambda i: (0, 0))
    return pl.pallas_call(
        _merge_kernel,
        out_shape=(jax.ShapeDtypeStruct((t, d), F32), jax.ShapeDtypeStruct((t, d), BF16)),
        grid=(t // tm,),
        in_specs=[rowb(a.shape[1]), rowb(b.shape[1]), rowb(c.shape[1]),
                  rowb(d, Z_GA // d), rowb(d, Z_GB // d), rowb(d, Z_GC // d), rowb(d), mod_spec,
                  c2(wa), c2(wb), c2(wc), c2(wo), pl.BlockSpec((1, d), lambda i: (0, 0)), mod_spec, mod_spec],
        out_specs=(rowb(d), rowb(d)),
        compiler_params=_cp("arbitrary"),
        name="merge",
    )(a, b, c, z, z, z, x, gate1, wa, wb, wc, wo, nf.reshape(1, d), scale2, shift2)


def _top_rows(x, n):
    rows = []
    for _ in range(n):
        m = jnp.max(x, axis=0, keepdims=True)
        rows.append(m)
        x = jnp.where(x == m, -jnp.inf, x)
    return rows


def _peer_topk_kernel(h2_ref, wq_ref, k1_ref, k2_ref, thr_out, s2_out, l1_out):
    h2 = h2_ref[...]
    k1 = PEER_TOPK + 1
    for h in range(PEER_HEADS):
        qh = jnp.dot(h2, wq_ref[h], preferred_element_type=F32)
        s1 = _bdot_nt(k1_ref[h], qh[:, :PEER_HALF])
        s2 = _bdot_nt(k2_ref[h], qh[:, PEER_HALF:])
        a = _top_rows(s1, k1)
        b = _top_rows(s2, k1)
        n_rows = -(-k1 // SUBLANES) * SUBLANES
        b_all = jnp.concatenate(b + [jnp.full_like(b[0], -jnp.inf)] * (n_rows - k1), axis=0)
        cand = []
        for p in range(k1):
            nq = -(-(k1 // (p + 1)) // SUBLANES) * SUBLANES
            cand.append(a[p] + b_all[:nq])
        best = _top_rows(jnp.concatenate(cand, axis=0), k1)
        zsum = jnp.zeros_like(best[0])
        for c in best[:PEER_TOPK]:
            zsum = zsum + jnp.exp(c - best[0])
        mid = 0.5 * (best[PEER_TOPK - 1] + best[PEER_TOPK])
        tabs = ((thr_out, ((mid - s1) - b[0]) * LOG2E), (s2_out, (s2 - b[0]) * LOG2E),
                (l1_out, (s1 - a[0]) * LOG2E - jnp.log(zsum) * LOG2E))
        for ref, val in tabs:
            for c in range(val.shape[1] // LANES):
                ref[h, c] = val[:, c * LANES:(c + 1) * LANES]


def peer_topk(h2, wq, k1, k2, tm):
    t, d = h2.shape
    hh = PEER_HEADS
    c3 = lambda a: pl.BlockSpec(a.shape, lambda i: (0, 0, 0))
    assert tm % LANES == 0
    tab = jax.ShapeDtypeStruct((hh, t // LANES, PEER_KEYS, LANES), F32)
    tab_spec = pl.BlockSpec((hh, tm // LANES, PEER_KEYS, LANES), lambda i: (0, i, 0, 0))
    return pl.pallas_call(
        _peer_topk_kernel,
        out_shape=(tab, tab, tab),
        grid=(t // tm,),
        in_specs=[pl.BlockSpec((tm, d), lambda i: (i, 0)), c3(wq), c3(k1), c3(k2)],
        out_specs=(tab_spec, tab_spec, tab_spec),
        compiler_params=_cp("arbitrary"),
        name="peer_topk",
    )(h2, wq, k1, k2)


def _peer_dense_kernel(h2_ref, u_ref, vt_ref, thr_ref, l1_ref, s2_ref, x1_ref, g2_ref,
                       o_ref, acc_ref, act_ref, wg_ref, *, ni):
    e = pl.program_id(1)

    @pl.when(e == 0)
    def _():
        acc_ref[...] = jnp.zeros(acc_ref.shape, F32)

    act_ref[...] = _gelu_tanh(lax.dot_general(u_ref[...], h2_ref[...], (((1,), (1,)), ((), ())),
                                              preferred_element_type=F32))
    tm = act_ref.shape[1]
    for c in range(tm // LANES):
        cs = slice(c * LANES, (c + 1) * LANES)
        for ii in range(ni):
            rs = slice(ii * PEER_KEYS, (ii + 1) * PEER_KEYS)
            w = jnp.zeros((PEER_KEYS, LANES), F32)
            for h in range(PEER_HEADS):
                s2 = s2_ref[h, c]
                w = w + jnp.where(s2 >= thr_ref[h, c, ii:ii + 1, :],
                                  jnp.exp2(s2 + l1_ref[h, c, ii:ii + 1, :]), 0.0)
            wg_ref[rs, cs] = (w * act_ref[rs, cs]).astype(BF16)
    acc_ref[...] += jnp.dot(vt_ref[...], wg_ref[...], preferred_element_type=F32)

    @pl.when(e == pl.num_programs(1) - 1)
    def _():
        o_ref[...] = x1_ref[...] + g2_ref[...] * acc_ref[...].T


def peer_dense(h2, u, vt, thr, l1, s2, x1, gate2, tm, ni):
    t, d = h2.shape
    n_exp = u.shape[0]
    te = ni * PEER_KEYS
    hh = PEER_HEADS
    per_tok = gate2.shape[0] != 1
    mod_spec = (pl.BlockSpec((tm, d), lambda i, e: (i, 0)) if per_tok
                else pl.BlockSpec((1, d), lambda i, e: (0, 0)))
    nc = tm // LANES
    sub = pl.BlockSpec((hh, nc, ni, LANES), lambda i, e: (0, i, e, 0))
    full = pl.BlockSpec((hh, nc, PEER_KEYS, LANES), lambda i, e: (0, i, 0, 0))
    return pl.pallas_call(
        functools.partial(_peer_dense_kernel, ni=ni),
        out_shape=jax.ShapeDtypeStruct((t, d), F32),
        grid=(t // tm, n_exp // te),
        in_specs=[pl.BlockSpec((tm, d), lambda i, e: (i, 0)),
                  pl.BlockSpec((te, d), lambda i, e: (e, 0)),
                  pl.BlockSpec((d, te), lambda i, e: (0, e)),
                  sub, sub, full,
                  pl.BlockSpec((tm, d), lambda i, e: (i, 0)), mod_spec],
        out_specs=pl.BlockSpec((tm, d), lambda i, e: (i, 0)),
        scratch_shapes=[pltpu.VMEM((d, tm), F32), pltpu.VMEM((te, tm), F32), pltpu.VMEM((te, tm), BF16)],
        compiler_params=_cp("arbitrary", "arbitrary"),
        name="peer_dense",
    )(h2, u, vt, thr, l1, s2, x1, gate2)


def _final_norm_kernel(x_ref, g_ref, o_ref):
    o_ref[...] = _rms(x_ref[...]) * g_ref[...]


def final_norm(x, g, tm):
    t, d = x.shape
    return pl.pallas_call(
        _final_norm_kernel,
        out_shape=jax.ShapeDtypeStruct((t, d), F32),
        grid=(t // tm,),
        in_specs=[pl.BlockSpec((tm, d), lambda i: (i, 0)), pl.BlockSpec((1, d), lambda i: (0, 0))],
        out_specs=pl.BlockSpec((tm, d), lambda i: (i, 0)),
        compiler_params=_cp("arbitrary"),
        name="final_norm",
    )(x, g.reshape(1, d))


def _pad_to(a, axis, size):
    pad = [(0, 0)] * a.ndim
    pad[axis] = (0, size - a.shape[axis])
    return jnp.pad(a, pad)


def _prep_w_in(w):
    sizes = (Q_LORA, KV_LORA, MLA_ROPE, 256, 256, 512, GLA_GATE_RANK, 512, 256, 256, 512, ML_HEADS, ML_HEADS, 512,
             D_MODEL, D_MODEL, D_MODEL)
    offs = np.concatenate([[0], np.cumsum(sizes)])
    (q_c, kv_c, k_pe, gq, gk, gv, ga, gg, mq, mk, mv, mi, mf, mo, g_a, g_b, g_c) = [
        w[:, offs[i]:offs[i + 1]] for i in range(len(sizes))]
    small = _pad_to(jnp.concatenate([k_pe, ga, mi, mf], axis=1), 1, LANES)
    out = jnp.concatenate([g_a, g_b, g_c, gv, gg, mv, mo, q_c, gq, gk, mq, mk, kv_c, small], axis=1)
    assert out.shape[1] == Z_WIDTH
    return out.astype(BF16)


def _rope_tables(pos):
    half = MLA_ROPE // 2
    inv = ROPE_BASE ** (-jnp.arange(half, dtype=F32) / half)
    ang = pos.astype(F32)[:, None] * inv[None, :]
    cos, sin = jnp.cos(ang), jnp.sin(ang)
    n = pos.shape[0]
    z = lambda w: jnp.zeros((n, w), F32)
    one = jnp.ones((n, MLA_NOPE), F32)
    rest = LANES - MLA_NOPE - MLA_ROPE
    cq = jnp.concatenate([one, cos, cos, z(rest)], axis=1)
    sqa = jnp.concatenate([z(MLA_NOPE), -sin, z(half), z(rest)], axis=1)
    sqb = jnp.concatenate([z(MLA_NOPE), z(half), sin, z(rest)], axis=1)
    ck = jnp.concatenate([cos, cos, z(LANES - MLA_ROPE)], axis=1)
    ska = jnp.concatenate([-sin, z(LANES - half)], axis=1)
    skb = jnp.concatenate([z(half), sin, z(LANES - MLA_ROPE)], axis=1)
    return (cq, sqa, sqb, ck, ska, skb), (cos.T, sin.T)


def _layer_weights(l, P):
    w = {}
    w['w_in'] = _prep_w_in(P['w_in'][l])
    w['gq'] = P['mla_q_norm'][l].reshape(1, Q_LORA)
    w['gkv'] = P['mla_kv_norm'][l].reshape(1, KV_LORA)
    w['wq'] = _pad_to(jnp.transpose(P['w_uq'][l], (1, 0, 2)), 2, LANES).astype(BF16)
    wuk = jnp.transpose(P['w_uk'][l], (1, 0, 2))
    w['wuk'] = _pad_to(wuk, 2, LANES).astype(BF16)
    w['wukt'] = _pad_to(jnp.transpose(wuk, (0, 2, 1)), 1, LANES).astype(BF16)
    w['wuv'] = _pad_to(jnp.transpose(P['w_uv'][l], (1, 0, 2)), 2, LANES).astype(BF16)
    w['wqt'] = _pad_to(jnp.transpose(P['w_uq'][l], (1, 2, 0)), 1, LANES).astype(BF16)
    w['wuvt'] = _pad_to(jnp.transpose(P['w_uv'][l], (1, 2, 0)), 1, LANES).astype(BF16)
    wa = jnp.zeros((LANES, GLA_HEADS * GLA_DK), F32).at[SM_GA:SM_GA + GLA_GATE_RANK].set(P['w_gla_a'][l])
    w['gla_wa'] = wa.astype(BF16)
    w['gla_wat'] = wa.T.astype(BF16)
    w['gla_ba'] = P['b_gla_a'][l].reshape(1, -1)
    w['gla_bat'] = P['b_gla_a'][l].reshape(-1, 1)
    w['gla_norm'] = P['gla_norm'][l].reshape(1, -1)
    w['ml_norm'] = P['ml_norm'][l].reshape(1, -1)
    brow = jnp.zeros((1, LANES), F32)
    brow = brow.at[0, SM_MI:SM_MI + ML_HEADS].set(P['ml_b_i'][l]).at[0, SM_MF:SM_MF + ML_HEADS].set(P['ml_b_f'][l])
    w['ml_brow'] = brow
    w['ml_bcol'] = jnp.concatenate([P['ml_b_i'][l], P['ml_b_f'][l]]).reshape(2 * ML_HEADS, 1)
    wpa = P['w_proj_a'][l].reshape(MLA_HEADS, MLA_V, D_MODEL)
    w['wpa'] = _pad_to(wpa, 1, LANES).reshape(MLA_HEADS * LANES, D_MODEL).astype(BF16)
    w['wpb'] = P['w_proj_b'][l].astype(BF16)
    w['wpc'] = P['w_proj_c'][l].astype(BF16)
    w['wo'] = P['w_o'][l].astype(BF16)
    w['peer_wq'] = jnp.transpose(P['peer_wq'][l], (1, 0, 2)).astype(BF16)
    w['peer_k1'] = P['peer_keys'][l][:, 0].astype(BF16)
    w['peer_k2'] = P['peer_keys'][l][:, 1].astype(BF16)
    w['peer_u'] = P['peer_u'][l].astype(BF16)
    w['peer_vt'] = P['peer_v'][l].T.astype(BF16)
    return w


def _mods(c, w_ada, b_ada):
    rows = c.shape[0]
    cp = _pad_to(c, 0, SUBLANES) if rows < SUBLANES else c
    mod = ada_mod(cp, w_ada, b_ada)[:rows]
    return jnp.split(mod, 6, axis=-1)


def _peer_block(h2, x1, gate2, w, tm, ni):
    thr, s2, l1 = peer_topk(h2, w['peer_wq'], w['peer_k1'], w['peer_k2'], tm)
    return peer_dense(h2, w['peer_u'], w['peer_vt'], thr, l1, s2, x1, gate2, tm, ni)


def _trunk_prompt(x, c, P, W):
    t = x.shape[0]
    tm = min(512, t)
    L = min(128, t)
    consts = _chunk_constants(L)
    tabs, tabs_t = _rope_tables(jnp.arange(t, dtype=jnp.int32))
    lat_rows, rope_rows, gla_st, c_st, n_st, m_st = [], [], [], [], [], []
    for l in range(len(W)):
        w = W[l]
        shift1, scale1, gate1, shift2, scale2, gate2 = _mods(c, P['w_ada'][l], P['b_ada'][l])
        z = norm_proj(x, P['norm_mix'][l], scale1, shift1, w['w_in'], min(2 * tm, t))
        qt, k, vt, lat, kr = mla_prep(z, w['gq'], w['gkv'], w['wqt'], w['wuk'], w['wuvt'], tabs_t, tabs, tm)
        a = flash_attention(qt, k, vt, tm)
        b, st = gla_scan(z, w['gla_wa'], w['gla_ba'], w['gla_norm'], consts, L)
        smt = z[:, Z_SMALL + SM_MI:Z_SMALL + SM_MI + 2 * ML_HEADS].T
        cc, ct, nrow, mrow = mlstm_scan(z, smt, w['ml_brow'], w['ml_bcol'], w['ml_norm'], consts, L)
        x1, h2 = merge(a, b, cc, z, x, gate1, w['wpa'], w['wpb'], w['wpc'], w['wo'],
                       P['norm_ffn'][l], scale2, shift2, tm)
        x = _peer_block(h2, x1, gate2, w, tm, SUBLANES)
        lat_rows.append(lat)
        rope_rows.append(kr)
        gla_st.append(jnp.stack([st[h][:, h * GLA_DK:(h + 1) * GLA_DK].T for h in range(GLA_HEADS)]))
        c_st.append(jnp.stack([ct[h][:, h * ML_DK:(h + 1) * ML_DK].T for h in range(ML_HEADS)]))
        n_st.append(jnp.stack([nrow[h, h * ML_DK:(h + 1) * ML_DK] for h in range(ML_HEADS)]))
        m_st.append(mrow[:ML_HEADS, 0])
    y = final_norm(x, P['norm_final'], tm)
    return y, tuple(jnp.stack(r) for r in (lat_rows, rope_rows, gla_st, c_st, n_st, m_st))


def _trunk_sample(x, c, P, W, cache_lat, cache_rope, page_table, s_gla, s_c, s_n, s_m):
    nb = x.shape[0]
    past = page_table.shape[1] * PAGE_SIZE
    tabs, _ = _rope_tables(jnp.full((1,), past, dtype=jnp.int32))
    cache_rope = jnp.swapaxes(cache_rope, 2, 3)
    lat_rows, rope_rows, gla_st, c_st, n_st, m_st = [], [], [], [], [], []
    for l in range(len(W)):
        w = W[l]
        shift1, scale1, gate1, shift2, scale2, gate2 = _mods(c, P['w_ada'][l], P['b_ada'][l])
        z = norm_proj(x, P['norm_mix'][l], scale1, shift1, w['w_in'], nb)
        ql, qp, lat, kr = mla_prep_s(z, w['gq'], w['gkv'], w['wq'], w['wukt'], tabs)
        o_lat = mla_decode(page_table, jnp.transpose(ql, (1, 0, 2)), jnp.transpose(qp, (1, 0, 2)),
                           lat[:, None, :], kr[:, None, :], cache_lat, cache_rope, l)
        a = uv_proj(o_lat.reshape(nb, MLA_HEADS * KV_LORA), w['wuv'])
        m_in = jnp.zeros((nb, LANES), F32).at[:, SM_MI:SM_MI + ML_HEADS].set(s_m[l])
        b, cc, s_new, c_new, n_new, m_new = rec_step(
            z, z.T, w['gla_wa'], w['gla_ba'], w['gla_wat'], w['gla_bat'], w['ml_brow'], w['gla_norm'], w['ml_norm'],
            s_gla[l], s_c[l], s_n[l].reshape(nb, ML_HEADS * ML_DK), m_in)
        x1, h2 = merge(a, b, cc, z, x, gate1, w['wpa'], w['wpb'], w['wpc'], w['wo'],
                       P['norm_ffn'][l], scale2, shift2, nb)
        x = _peer_block(h2, x1, gate2, w, nb, SUBLANES)
        lat_rows.append(lat[:, None, :])
        rope_rows.append(kr[:, None, :])
        gla_st.append(s_new)
        c_st.append(c_new)
        n_st.append(n_new.reshape(nb, ML_HEADS, ML_DK))
        m_st.append(m_new[:, SM_MI:SM_MI + ML_HEADS])
    y = final_norm(x, P['norm_final'], nb)
    return y, tuple(jnp.stack(r) for r in (lat_rows, rope_rows, gla_st, c_st, n_st, m_st))


def kernel(x_prompt, x_sample, cache_kv_latent, cache_k_rope, state_gla, state_mlstm_C, state_mlstm_n, state_mlstm_m, page_table, c_prompt, c_sample, w_ada, b_ada, norm_mix, norm_ffn, norm_final, w_in, mla_q_norm, mla_kv_norm, w_uq, w_uk, w_uv, w_gla_a, b_gla_a, gla_norm, ml_b_i, ml_b_f, ml_norm, w_proj_a, w_proj_b, w_proj_c, w_o, peer_wq, peer_keys, peer_u, peer_v):
    P = {'w_ada': w_ada, 'b_ada': b_ada, 'norm_mix': norm_mix, 'norm_ffn': norm_ffn, 'norm_final': norm_final,
         'w_in': w_in, 'mla_q_norm': mla_q_norm, 'mla_kv_norm': mla_kv_norm, 'w_uq': w_uq, 'w_uk': w_uk,
         'w_uv': w_uv, 'w_gla_a': w_gla_a, 'b_gla_a': b_gla_a, 'gla_norm': gla_norm, 'ml_b_i': ml_b_i,
         'ml_b_f': ml_b_f, 'ml_norm': ml_norm, 'w_proj_a': w_proj_a, 'w_proj_b': w_proj_b,
         'w_proj_c': w_proj_c, 'w_o': w_o, 'peer_wq': peer_wq, 'peer_keys': peer_keys, 'peer_u': peer_u,
         'peer_v': peer_v}
    depth = w_in.shape[0]
    W = [_layer_weights(l, P) for l in range(depth)]
    bp, sp, d = x_prompt.shape
    assert bp == 1 and x_sample.shape[1] == 1
    nb = x_sample.shape[0]
    y_p, (lat_p, rope_p, gla_p, c_p, n_p, m_p) = _trunk_prompt(x_prompt.reshape(sp, d), c_prompt, P, W)
    y_s, (lat_s, rope_s, gla_s, c_s, n_s, m_s) = _trunk_sample(
        x_sample.reshape(nb, d), c_sample, P, W, cache_kv_latent, cache_k_rope, page_table,
        state_gla, state_mlstm_C, state_mlstm_n, state_mlstm_m)
    return (y_p.reshape(bp, sp, d), y_s.reshape(nb, 1, d),
            lat_p[:, None], rope_p[:, None], gla_p[:, None], c_p[:, None], n_p[:, None], m_p[:, None],
            lat_s, rope_s, gla_s, c_s, n_s, m_s)
```

```python
import functools

import numpy as np
import jax
import jax.numpy as jnp
from jax import lax
from jax.experimental import pallas as pl
from jax.experimental.pallas import tpu as pltpu

F32 = jnp.float32
BF16 = jnp.bfloat16

D_MODEL = 1024
PAGE_SIZE = 128
MLA_HEADS = 8
MLA_NOPE = 64
MLA_ROPE = 32
MLA_V = 64
Q_LORA = 256
KV_LORA = 128
ROPE_BASE = 10000.0
MLA_SCALE = (MLA_NOPE + MLA_ROPE) ** -0.5
GLA_HEADS = 4
GLA_DK = 64
GLA_DV = 128
GLA_GATE_RANK = 16
GLA_GATE_TEMP = 16.0
ML_HEADS = 4
ML_DK = 64
ML_DV = 128
PEER_HEADS = 8
PEER_KEYS = 128
PEER_HALF = 128
PEER_TOPK = 16
NORM_EPS = 1e-6

LANES = 128
SUBLANES = 8
VMEM_LIMIT = 48 * 1024 * 1024

NEG = -1e30
LOG2E = 1.4426950408889634

Z_GA, Z_GB, Z_GC = 0, 1024, 2048
Z_GV, Z_GG, Z_MV, Z_MO = 3072, 3584, 4096, 4608
Z_QC, Z_GQ, Z_GK, Z_MQ, Z_MK = 5120, 5376, 5632, 5888, 6144
Z_KVC, Z_SMALL = 6400, 6528
Z_WIDTH = 6656
SM_KPE, SM_GA, SM_MI, SM_MF = 0, 32, 48, 52


def _cp(*sem):
    return pltpu.CompilerParams(dimension_semantics=tuple(sem), vmem_limit_bytes=VMEM_LIMIT)


def _bdot(a, b):
    return jnp.dot(a.astype(BF16), b.astype(BF16), preferred_element_type=F32)


def _bdot_nt(a, b):
    return lax.dot_general(a.astype(BF16), b.astype(BF16), (((1,), (1,)), ((), ())),
                           preferred_element_type=F32)


def _split3(x):
    hi = x.astype(BF16)
    r = x - hi.astype(F32)
    mid = r.astype(BF16)
    lo = (r - mid.astype(F32)).astype(BF16)
    return hi, mid, lo


def _dot01(m01, x):
    hi, mid, lo = _split3(x)
    return (jnp.dot(m01, hi, preferred_element_type=F32) + jnp.dot(m01, mid, preferred_element_type=F32)
            + jnp.dot(m01, lo, preferred_element_type=F32))


def _dot01_r(x, m01):
    hi, mid, lo = _split3(x)
    return (jnp.dot(hi, m01, preferred_element_type=F32) + jnp.dot(mid, m01, preferred_element_type=F32)
            + jnp.dot(lo, m01, preferred_element_type=F32))


def _sigmoid(x):
    return 1.0 / (1.0 + jnp.exp(-x))


def _log_sigmoid(x):
    return jnp.minimum(x, 0.0) - jnp.log(1.0 + jnp.exp(-jnp.abs(x)))


def _rms(x):
    return x * lax.rsqrt(jnp.mean(x * x, axis=-1, keepdims=True) + NORM_EPS)


def _gelu_tanh(x):
    return 0.5 * x * (1.0 + jnp.tanh(0.7978845608028654 * (x + 0.044715 * (x * x * x))))


def _ada_kernel(c_ref, w_ref, b_ref, o_ref):
    c = c_ref[...]
    o_ref[...] = _bdot(c * _sigmoid(c), w_ref[...]) + b_ref[...]


def ada_mod(c, w, b):
    rows, d = c.shape
    n = w.shape[1]
    tn = 1536
    return pl.pallas_call(
        _ada_kernel,
        out_shape=jax.ShapeDtypeStruct((rows, n), F32),
        grid=(n // tn,),
        in_specs=[pl.BlockSpec((rows, d), lambda j: (0, 0)),
                  pl.BlockSpec((d, tn), lambda j: (0, j)),
                  pl.BlockSpec((1, tn), lambda j: (0, j))],
        out_specs=pl.BlockSpec((rows, tn), lambda j: (0, j)),
        compiler_params=_cp("arbitrary"),
        name="ada_mod",
    )(c, w, b.reshape(1, n))


def _norm_proj_kernel(x_ref, g_ref, sc_ref, sh_ref, w_ref, o_ref, hn_ref):
    @pl.when(pl.program_id(1) == 0)
    def _():
        h = _rms(x_ref[...]) * g_ref[...] * (1.0 + sc_ref[...]) + sh_ref[...]
        hn_ref[...] = h.astype(BF16)

    o_ref[...] = jnp.dot(hn_ref[...], w_ref[...], preferred_element_type=F32)


def norm_proj(x, g, scale, shift, w, tm):
    t, d = x.shape
    n = w.shape[1]
    tn = 512
    per_tok = scale.shape[0] != 1
    mod_spec = (pl.BlockSpec((tm, d), lambda i, j: (i, 0)) if per_tok
                else pl.BlockSpec((1, d), lambda i, j: (0, 0)))
    return pl.pallas_call(
        _norm_proj_kernel,
        out_shape=jax.ShapeDtypeStruct((t, n), F32),
        grid=(t // tm, n // tn),
        in_specs=[pl.BlockSpec((tm, d), lambda i, j: (i, 0)),
                  pl.BlockSpec((1, d), lambda i, j: (0, 0)),
                  mod_spec, mod_spec,
                  pl.BlockSpec((d, tn), lambda i, j: (0, j))],
        out_specs=pl.BlockSpec((tm, tn), lambda i, j: (i, j)),
        scratch_shapes=[pltpu.VMEM((tm, d), BF16)],
        compiler_params=_cp("arbitrary", "arbitrary"),
        name="norm_proj",
    )(x, g.reshape(1, d), scale, shift, w)


def _rope_lanes(x, c, sa, sb):
    return x * c + pltpu.roll(x, LANES - 16, 1) * sa + pltpu.roll(x, 16, 1) * sb


def _mla_prep_kernel(qc_ref, kvc_ref, sm_ref, gq_ref, gkv_ref, wqt_ref, wuk_ref, wuvt_ref,
                     ct_ref, st_ref, ck_ref, ska_ref, skb_ref,
                     qt_out, k_out, vt_out, lat_out, kr_out):
    half = MLA_ROPE // 2
    qn = (_rms(qc_ref[...]) * gq_ref[...]).astype(BF16)
    lat = _rms(kvc_ref[...]) * gkv_ref[...]
    lat_out[...] = lat
    latb = lat.astype(BF16)
    krf = _rope_lanes(sm_ref[...], ck_ref[...], ska_ref[...], skb_ref[...])
    kr_out[...] = krf[:, :MLA_ROPE]
    kr_sh = pltpu.roll(krf, MLA_NOPE, 1)
    cos_t, sin_t = ct_ref[...], st_ref[...]
    tm = qn.shape[0]
    ones_row = (lax.broadcasted_iota(jnp.int32, (LANES, tm), 0) == MLA_V).astype(F32)
    pad = jnp.zeros((LANES - MLA_NOPE - MLA_ROPE, tm), F32)
    for h in range(MLA_HEADS):
        qt = lax.dot_general(wqt_ref[h], qn, (((1,), (1,)), ((), ())), preferred_element_type=F32)
        x1 = qt[MLA_NOPE:MLA_NOPE + half]
        x2 = qt[MLA_NOPE + half:MLA_NOPE + MLA_ROPE]
        qt = jnp.concatenate([qt[:MLA_NOPE], x1 * cos_t - x2 * sin_t, x1 * sin_t + x2 * cos_t, pad], axis=0)
        qt_out[h] = (qt * (MLA_SCALE * LOG2E)).astype(BF16)
        k_out[h] = (jnp.dot(latb, wuk_ref[h], preferred_element_type=F32) + kr_sh).astype(BF16)
        vt = lax.dot_general(wuvt_ref[h], latb, (((1,), (1,)), ((), ())), preferred_element_type=F32)
        vt_out[h] = (vt + ones_row).astype(BF16)


def mla_prep(z, gq, gkv, wqt, wuk, wuvt, tabs_t, tabs, tm):
    t = z.shape[0]
    hh = MLA_HEADS
    row = lambda w, off: pl.BlockSpec((tm, w), lambda i, o=off // w: (i, o))
    const2 = lambda a: pl.BlockSpec(a.shape, lambda i: (0, 0))
    const3 = lambda a: pl.BlockSpec(a.shape, lambda i: (0, 0, 0))
    tab_spec = pl.BlockSpec((tm, LANES), lambda i: (i, 0))
    tabt_spec = pl.BlockSpec((MLA_ROPE // 2, tm), lambda i: (0, i))
    return pl.pallas_call(
        _mla_prep_kernel,
        out_shape=(jax.ShapeDtypeStruct((hh, LANES, t), BF16),
                   jax.ShapeDtypeStruct((hh, t, LANES), BF16),
                   jax.ShapeDtypeStruct((hh, LANES, t), BF16),
                   jax.ShapeDtypeStruct((t, KV_LORA), F32),
                   jax.ShapeDtypeStruct((t, MLA_ROPE), F32)),
        grid=(t // tm,),
        in_specs=[row(Q_LORA, Z_QC), row(KV_LORA, Z_KVC), row(LANES, Z_SMALL),
                  const2(gq), const2(gkv), const3(wqt), const3(wuk), const3(wuvt),
                  tabt_spec, tabt_spec, tab_spec, tab_spec, tab_spec],
        out_specs=(pl.BlockSpec((hh, LANES, tm), lambda i: (0, 0, i)),
                   pl.BlockSpec((hh, tm, LANES), lambda i: (0, i, 0)),
                   pl.BlockSpec((hh, LANES, tm), lambda i: (0, 0, i)),
                   pl.BlockSpec((tm, KV_LORA), lambda i: (i, 0)),
                   pl.BlockSpec((tm, MLA_ROPE), lambda i: (i, 0))),
        compiler_params=_cp("arbitrary"),
        name="mla_prep",
    )(z, z, z, gq, gkv, wqt, wuk, wuvt, *tabs_t, *tabs[3:])


def _flash_kernel(qi_ref, ki_ref, qt_ref, k_ref, vt_ref, o_ref, m_ref, acc_ref, *, tq):
    qi = qi_ref[pl.program_id(0)]
    ki = ki_ref[pl.program_id(0)]

    @pl.when(ki == 0)
    def _():
        m_ref[...] = jnp.full(m_ref.shape, NEG, F32)
        acc_ref[...] = jnp.zeros(acc_ref.shape, F32)

    def scores(h):
        return jnp.dot(k_ref[h], qt_ref[h], preferred_element_type=F32)

    def accumulate(h, p, alpha):
        acc_ref[h] = alpha * acc_ref[h] + jnp.dot(vt_ref[h], p, preferred_element_type=F32)

    def step(diag):
        if diag:
            key = lax.broadcasted_iota(jnp.int32, (tq, tq), 0)
            qry = lax.broadcasted_iota(jnp.int32, (tq, tq), 1)
            keep = key <= qry
        s_next = scores(0)
        pending = None
        for h in range(MLA_HEADS):
            s = s_next
            if h + 1 < MLA_HEADS:
                s_next = scores(h + 1)
            if pending is not None:
                accumulate(*pending)
            if diag:
                s = jnp.where(keep, s, NEG)
            m_prev = m_ref[h:h + 1, :]
            m_new = jnp.maximum(m_prev, jnp.max(s, axis=0, keepdims=True))
            p = jnp.exp2(s - m_new).astype(BF16)
            m_ref[h:h + 1, :] = m_new
            pending = (h, p, jnp.exp2(m_prev - m_new))
        accumulate(*pending)

    @pl.when(ki < qi)
    def _():
        step(False)

    @pl.when(ki == qi)
    def _():
        step(True)
        for h in range(MLA_HEADS):
            acc = acc_ref[h]
            o_ref[:, h * LANES:(h + 1) * LANES] = (acc / acc[MLA_V:MLA_V + 1, :]).T.astype(BF16)


def flash_attention(qt, k, vt, tq):
    hh, t, _ = k.shape
    n = t // tq
    pairs = np.array([(i, j) for i in range(n) for j in range(i + 1)], np.int32)
    return pl.pallas_call(
        functools.partial(_flash_kernel, tq=tq),
        out_shape=jax.ShapeDtypeStruct((t, hh * LANES), BF16),
        grid_spec=pltpu.PrefetchScalarGridSpec(
            num_scalar_prefetch=2,
            grid=(pairs.shape[0],),
            in_specs=[pl.BlockSpec((hh, LANES, tq), lambda s, qa, ka: (0, 0, qa[s])),
                      pl.BlockSpec((hh, tq, LANES), lambda s, qa, ka: (0, ka[s], 0)),
                      pl.BlockSpec((hh, LANES, tq), lambda s, qa, ka: (0, 0, ka[s]))],
            out_specs=pl.BlockSpec((tq, hh * LANES), lambda s, qa, ka: (qa[s], 0)),
            scratch_shapes=[pltpu.VMEM((hh, tq), F32), pltpu.VMEM((hh, LANES, tq), F32)]),
        compiler_params=_cp("arbitrary"),
        name="flash_attention",
    )(jnp.asarray(pairs[:, 0]), jnp.asarray(pairs[:, 1]), qt, k, vt)


def _chunk_constants(L):
    nl = int(np.log2(L))
    assert 2 ** nl == L
    idx = np.arange(L)
    tri = (idx[None, :] <= idx[:, None]).astype(np.float32)
    mq, mk, pm = [], [], []
    for lev in range(nl):
        p = 1 << lev
        start_right = (idx >> (lev + 1) << (lev + 1)) + p
        right = ((idx >> lev) & 1) == 1
        u = idx[None, :]
        mq.append((right[:, None] & (u >= start_right[:, None]) & (u <= idx[:, None])).astype(np.float32))
        mk.append(((~right)[:, None] & (u > idx[:, None]) & (u < start_right[:, None])).astype(np.float32))
        same = (idx[:, None] >> (lev + 1)) == (idx[None, :] >> (lev + 1))
        pm.append((same & right[:, None] & (~right)[None, :]).astype(np.float32))
    pm.append(np.eye(L, dtype=np.float32))
    sums = np.concatenate([tri] + [a + b for a, b in zip(mq, mk)], axis=0)
    return jnp.asarray(sums, BF16), jnp.asarray(np.stack(pm), F32), jnp.asarray(tri, BF16), jnp.asarray(tri.T, BF16)


def _head_mask(width, heads, h):
    lane = lax.broadcasted_iota(jnp.int32, (1, width), 1)
    per = width // heads
    return ((lane >= h * per) & (lane < (h + 1) * per)).astype(F32)


def _gla_scan_kernel(gq_ref, gk_ref, gv_ref, gg_ref, sm_ref, wa_ref, ba_ref, sums_ref, pm_ref, gn_ref,
                     o_ref, st_out, st_ref, *, L, nl):
    step = pl.program_id(0)

    @pl.when(step == 0)
    def _():
        st_ref[...] = jnp.zeros(st_ref.shape, F32)

    width = GLA_HEADS * GLA_DK
    x = _bdot(sm_ref[...], wa_ref[...]) + ba_ref[...]
    la = _log_sigmoid(x) * (1.0 / GLA_GATE_TEMP)
    dsum = _dot01(sums_ref[...], la)
    b = dsum[0:L]
    b_last = b[L - 1:L, :]
    q = gq_ref[...] * (GLA_DK ** -0.5)
    k = gk_ref[...]
    q_in = (q * jnp.exp(b)).astype(BF16)
    k_out = k * jnp.exp(b_last - b)
    decay_all = jnp.exp(b_last)
    hms = [_head_mask(width, GLA_HEADS, h) for h in range(GLA_HEADS)]
    atts = [jnp.zeros((L, L), F32) for _ in range(GLA_HEADS)]
    for lev in range(nl + 1):
        fac = jnp.exp(dsum[(1 + lev) * L:(2 + lev) * L]) if lev < nl else None
        ql = q if fac is None else q * fac
        kl = (k if fac is None else k * fac).astype(BF16)
        stacked = jnp.concatenate([ql * hm for hm in hms], axis=0).astype(BF16)
        sc = lax.dot_general(stacked, kl, (((1,), (1,)), ((), ())), preferred_element_type=F32)
        keep = pm_ref[lev] > 0.0
        for h in range(GLA_HEADS):
            atts[h] = atts[h] + jnp.where(keep, sc[h * L:(h + 1) * L], 0.0)
    for h in range(GLA_HEADS):
        hm = hms[h]
        att = atts[h]
        vh = gv_ref[:, h * GLA_DV:(h + 1) * GLA_DV]
        st = st_ref[h]
        o = (jnp.dot(att.astype(BF16), vh.astype(BF16), preferred_element_type=F32)
             + lax.dot_general(q_in, st.astype(BF16), (((1,), (1,)), ((), ())), preferred_element_type=F32))
        st_ref[h] = st * decay_all + jnp.dot(vh.T.astype(BF16), (k_out * hm).astype(BF16),
                                             preferred_element_type=F32)
        gg = gg_ref[:, h * GLA_DV:(h + 1) * GLA_DV]
        y = _rms(o) * gn_ref[:, h * GLA_DV:(h + 1) * GLA_DV] * (gg * _sigmoid(gg))
        o_ref[:, h * GLA_DV:(h + 1) * GLA_DV] = y.astype(BF16)

    @pl.when(step == pl.num_programs(0) - 1)
    def _():
        st_out[...] = st_ref[...]


def gla_scan(z, wa, ba, gn, consts, L):
    t = z.shape[0]
    sums, pm, _, _ = consts
    nl = pm.shape[0] - 1
    width = GLA_HEADS * GLA_DK
    vw = GLA_HEADS * GLA_DV
    row = lambda w, off: pl.BlockSpec((L, w), lambda i, o=off // w: (i, o))
    c2 = lambda a: pl.BlockSpec(a.shape, lambda i: (0, 0))
    c3 = lambda a: pl.BlockSpec(a.shape, lambda i: (0, 0, 0))
    return pl.pallas_call(
        functools.partial(_gla_scan_kernel, L=L, nl=nl),
        out_shape=(jax.ShapeDtypeStruct((t, vw), BF16),
                   jax.ShapeDtypeStruct((GLA_HEADS, GLA_DV, width), F32)),
        grid=(t // L,),
        in_specs=[row(width, Z_GQ), row(width, Z_GK), row(vw, Z_GV), row(vw, Z_GG), row(LANES, Z_SMALL),
                  c2(wa), c2(ba), c2(sums), c3(pm), c2(gn)],
        out_specs=(pl.BlockSpec((L, vw), lambda i: (i, 0)),
                   pl.BlockSpec((GLA_HEADS, GLA_DV, width), lambda i: (0, 0, 0))),
        scratch_shapes=[pltpu.VMEM((GLA_HEADS, GLA_DV, width), F32)],
        compiler_params=_cp("arbitrary"),
        name="gla_scan",
    )(z, z, z, z, z, wa, ba, sums, pm, gn)


def _mlstm_scan_kernel(mq_ref, mk_ref, mv_ref, mo_ref, sm_ref, smt_ref, brow_ref, bcol_ref, tri_ref, trit_ref,
                       nrm_ref, o_ref, ct_out, n_out, m_out, ct_ref, n_ref, m_ref, *, L):
    step = pl.program_id(0)

    @pl.when(step == 0)
    def _():
        ct_ref[...] = jnp.zeros(ct_ref.shape, F32)
        n_ref[...] = jnp.zeros(n_ref.shape, F32)
        m_ref[...] = jnp.zeros(m_ref.shape, F32)

    width = ML_HEADS * ML_DK
    pre_c = sm_ref[...] + brow_ref[...]
    pre_r = smt_ref[...] + bcol_ref[...]
    b_c_all = _dot01(tri_ref[...], _log_sigmoid(pre_c))
    b_r_all = _dot01_r(_log_sigmoid(pre_r), trit_ref[...])
    q = mq_ref[...]
    k = mk_ref[...] * (ML_DK ** -0.5)
    kb = k.astype(BF16)
    row = lax.broadcasted_iota(jnp.int32, (L, L), 0)
    col = lax.broadcasted_iota(jnp.int32, (L, L), 1)
    causal = col <= row
    for h in range(ML_HEADS):
        hm = _head_mask(width, ML_HEADS, h)
        bc = b_c_all[:, SM_MF + h:SM_MF + h + 1]
        br = b_r_all[ML_HEADS + h:ML_HEADS + h + 1, :]
        ic = pre_c[:, SM_MI + h:SM_MI + h + 1]
        ir = pre_r[h:h + 1, :]
        m_prev = m_ref[h:h + 1, 0:1]
        dlog = jnp.where(causal, bc - br + ir, NEG)
        inter = bc + m_prev
        mt = jnp.maximum(inter, jnp.max(dlog, axis=1, keepdims=True))
        qh = (q * hm).astype(BF16)
        qk = lax.dot_general(qh, kb, (((1,), (1,)), ((), ())), preferred_element_type=F32)
        wi = jnp.exp(dlog - mt) * qk
        wint = jnp.exp(inter - mt)
        vh = mv_ref[:, h * ML_DV:(h + 1) * ML_DV]
        ct = ct_ref[h]
        nrow = n_ref[h:h + 1, :]
        num = (jnp.dot(wi.astype(BF16), vh.astype(BF16), preferred_element_type=F32)
               + wint * lax.dot_general(qh, ct.astype(BF16), (((1,), (1,)), ((), ())),
                                        preferred_element_type=F32))
        nq = jnp.sum(wi, axis=1, keepdims=True) + wint * jnp.sum(q * nrow, axis=1, keepdims=True)
        hh = num / jnp.maximum(jnp.abs(nq), jnp.exp(-mt))
        m_new = mt[L - 1:L, :]
        b_last = bc[L - 1:L, :]
        keep = jnp.exp(b_last + m_prev - m_new)
        wn = jnp.exp(b_last - bc + ic - m_new)
        kw = k * hm * wn
        ct_ref[h] = keep * ct + jnp.dot(vh.T.astype(BF16), kw.astype(BF16), preferred_element_type=F32)
        n_ref[h:h + 1, :] = keep * nrow + jnp.sum(kw, axis=0, keepdims=True)
        m_ref[h:h + 1, :] = jnp.broadcast_to(m_new, (1, LANES))
        mo = mo_ref[:, h * ML_DV:(h + 1) * ML_DV]
        y = _rms(hh) * nrm_ref[:, h * ML_DV:(h + 1) * ML_DV] * _sigmoid(mo)
        o_ref[:, h * ML_DV:(h + 1) * ML_DV] = y.astype(BF16)

    @pl.when(step == pl.num_programs(0) - 1)
    def _():
        ct_out[...] = ct_ref[...]
        n_out[...] = n_ref[...]
        m_out[...] = m_ref[...]


def mlstm_scan(z, smt, brow, bcol, nrm, consts, L):
    t = z.shape[0]
    _, _, tri, trit = consts
    width = ML_HEADS * ML_DK
    vw = ML_HEADS * ML_DV
    row = lambda w, off: pl.BlockSpec((L, w), lambda i, o=off // w: (i, o))
    c2 = lambda a: pl.BlockSpec(a.shape, lambda i: (0, 0))
    st_shapes = [((ML_HEADS, ML_DV, width), lambda i: (0, 0, 0)), ((SUBLANES, width), lambda i: (0, 0)),
                 ((SUBLANES, LANES), lambda i: (0, 0))]
    return pl.pallas_call(
        functools.partial(_mlstm_scan_kernel, L=L),
        out_shape=(jax.ShapeDtypeStruct((t, vw), BF16),) + tuple(
            jax.ShapeDtypeStruct(s, F32) for s, _ in st_shapes),
        grid=(t // L,),
        in_specs=[row(width, Z_MQ), row(width, Z_MK), row(vw, Z_MV), row(vw, Z_MO), row(LANES, Z_SMALL),
                  pl.BlockSpec((SUBLANES, L), lambda i: (0, i)),
                  c2(brow), c2(bcol), c2(tri), c2(trit), c2(nrm)],
        out_specs=(pl.BlockSpec((L, vw), lambda i: (i, 0)),) + tuple(pl.BlockSpec(s, f) for s, f in st_shapes),
        scratch_shapes=[pltpu.VMEM(s, F32) for s, _ in st_shapes],
        compiler_params=_cp("arbitrary"),
        name="mlstm_scan",
    )(z, z, z, z, z, smt, brow, bcol, tri, trit, nrm)


def _rec_step_kernel(gq_ref, gk_ref, gv_ref, gg_ref, mq_ref, mk_ref, mv_ref, mo_ref, sm_ref,
                     gqt_ref, gkt_ref, mqt_ref, mkt_ref, smt_ref,
                     wa_ref, ba_ref, wat_ref, bat_ref, brow_ref, gn_ref, mn_ref,
                     s_ref, c_ref, n_ref, m_ref,
                     bo_ref, co_ref, s_out, c_out, n_out, m_out, *, rows):
    blk = pl.program_id(0)
    nb = gqt_ref.shape[1]
    gw = GLA_HEADS * GLA_DK
    mw = ML_HEADS * ML_DK

    pre = sm_ref[...] + brow_ref[...]
    i_v = pre
    f_v = _log_sigmoid(pltpu.roll(pre, LANES - (SM_MF - SM_MI), 1))
    m_prev = m_ref[...]
    mt = jnp.maximum(f_v + m_prev, i_v)
    w_intra = jnp.exp(i_v - mt)
    w_inter = jnp.exp(f_v + m_prev - mt)
    inv_floor = jnp.exp(-mt)
    m_out[...] = mt
    mq = mq_ref[...]
    mk = mk_ref[...] * (ML_DK ** -0.5)
    n_old = n_ref[...]
    gq = gq_ref[...] * (GLA_DK ** -0.5)
    gk = gk_ref[...]
    qk_g, qk_m, qn_m, den = [], [], [], []
    keep_b = jnp.zeros((rows, mw), F32)
    wnew_b = jnp.zeros((rows, mw), F32)
    for h in range(ML_HEADS):
        hm = _head_mask(mw, ML_HEADS, h)
        qk_m.append(jnp.sum(mq * mk * hm, axis=1, keepdims=True))
        qn_m.append(jnp.sum(mq * n_old * hm, axis=1, keepdims=True))
        wi_h = w_intra[:, SM_MI + h:SM_MI + h + 1]
        we_h = w_inter[:, SM_MI + h:SM_MI + h + 1]
        nq = wi_h * qk_m[h] + we_h * qn_m[h]
        den.append(jnp.maximum(jnp.abs(nq), inv_floor[:, SM_MI + h:SM_MI + h + 1]))
        keep_b = keep_b + hm * we_h
        wnew_b = wnew_b + hm * wi_h
    n_out[...] = keep_b * n_old + wnew_b * mk
    for h in range(GLA_HEADS):
        qk_g.append(jnp.sum(gq * gk * _head_mask(gw, GLA_HEADS, h), axis=1, keepdims=True))

    xt = _bdot(wat_ref[...], smt_ref[...]) + bat_ref[...]
    at = jnp.exp(_log_sigmoid(xt) * (1.0 / GLA_GATE_TEMP))
    seq_lane = lax.broadcasted_iota(jnp.int32, (nb, LANES), 0)

    for r in range(rows):
        onehot = (seq_lane == blk * rows + r).astype(BF16)
        a_bc = _dot01_r(at, onehot)
        gk_bc = _dot01_r(gkt_ref[...], onehot)
        gq_bc = _dot01_r(gqt_ref[...], onehot) * (GLA_DK ** -0.5)
        mk_bc = _dot01_r(mkt_ref[...], onehot) * (ML_DK ** -0.5)
        mq_bc = _dot01_r(mqt_ref[...], onehot)
        for h in range(GLA_HEADS):
            sl = slice(h * GLA_DK, (h + 1) * GLA_DK)
            v_row = gv_ref[r:r + 1, h * GLA_DV:(h + 1) * GLA_DV]
            s_new = a_bc[sl] * s_ref[r, h] + gk_bc[sl] * v_row
            s_out[r, h] = s_new
            o = jnp.sum(gq_bc[sl] * s_new, axis=0, keepdims=True)
            gg = gg_ref[r:r + 1, h * GLA_DV:(h + 1) * GLA_DV]
            y = _rms(o) * gn_ref[:, h * GLA_DV:(h + 1) * GLA_DV] * (gg * _sigmoid(gg))
            bo_ref[r:r + 1, h * GLA_DV:(h + 1) * GLA_DV] = y.astype(BF16)
        for h in range(ML_HEADS):
            sl = slice(h * ML_DK, (h + 1) * ML_DK)
            v_row = mv_ref[r:r + 1, h * ML_DV:(h + 1) * ML_DV]
            c_old = c_ref[r, h]
            wi_s = w_intra[r:r + 1, SM_MI + h:SM_MI + h + 1]
            we_s = w_inter[r:r + 1, SM_MI + h:SM_MI + h + 1]
            num = (wi_s * qk_m[h][r:r + 1, :]) * v_row + we_s * jnp.sum(mq_bc[sl] * c_old, axis=0, keepdims=True)
            hh = num / den[h][r:r + 1, :]
            c_out[r, h] = we_s * c_old + wi_s * (mk_bc[sl] * v_row)
            mo = mo_ref[r:r + 1, h * ML_DV:(h + 1) * ML_DV]
            y = _rms(hh) * mn_ref[:, h * ML_DV:(h + 1) * ML_DV] * _sigmoid(mo)
            co_ref[r:r + 1, h * ML_DV:(h + 1) * ML_DV] = y.astype(BF16)


def rec_step(z, zt, wa, ba, wat, bat, brow, gn, mn, s0, c0, n0, m0):
    nb = z.shape[0]
    rows = SUBLANES
    gw = GLA_HEADS * GLA_DK
    mw = ML_HEADS * ML_DK
    gvw = GLA_HEADS * GLA_DV
    mvw = ML_HEADS * ML_DV
    row = lambda w, off: pl.BlockSpec((rows, w), lambda i, o=off // w: (i, o))
    colb = lambda w, off: pl.BlockSpec((w, nb), lambda i, o=off // w: (o, 0))
    c2 = lambda a: pl.BlockSpec(a.shape, lambda i: (0, 0))
    st4 = lambda hds, dk, dv: pl.BlockSpec((rows, hds, dk, dv), lambda i: (i, 0, 0, 0))
    return pl.pallas_call(
        functools.partial(_rec_step_kernel, rows=rows),
        out_shape=(jax.ShapeDtypeStruct((nb, gvw), BF16), jax.ShapeDtypeStruct((nb, mvw), BF16),
                   jax.ShapeDtypeStruct(s0.shape, F32), jax.ShapeDtypeStruct(c0.shape, F32),
                   jax.ShapeDtypeStruct((nb, mw), F32), jax.ShapeDtypeStruct((nb, LANES), F32)),
        grid=(nb // rows,),
        in_specs=[row(gw, Z_GQ), row(gw, Z_GK), row(gvw, Z_GV), row(gvw, Z_GG),
                  row(mw, Z_MQ), row(mw, Z_MK), row(mvw, Z_MV), row(mvw, Z_MO), row(LANES, Z_SMALL),
                  colb(gw, Z_GQ), colb(gw, Z_GK), colb(mw, Z_MQ), colb(mw, Z_MK), colb(LANES, Z_SMALL),
                  c2(wa), c2(ba), c2(wat), c2(bat), c2(brow), c2(gn), c2(mn),
                  st4(GLA_HEADS, GLA_DK, GLA_DV), st4(ML_HEADS, ML_DK, ML_DV),
                  pl.BlockSpec((rows, mw), lambda i: (i, 0)), pl.BlockSpec((rows, LANES), lambda i: (i, 0))],
        out_specs=(pl.BlockSpec((rows, gvw), lambda i: (i, 0)), pl.BlockSpec((rows, mvw), lambda i: (i, 0)),
                   st4(GLA_HEADS, GLA_DK, GLA_DV), st4(ML_HEADS, ML_DK, ML_DV),
                   pl.BlockSpec((rows, mw), lambda i: (i, 0)), pl.BlockSpec((rows, LANES), lambda i: (i, 0))),
        compiler_params=_cp("arbitrary"),
        name="rec_step",
    )(z, z, z, z, z, z, z, z, z, zt, zt, zt, zt, zt, wa, ba, wat, bat, brow, gn, mn, s0, c0, n0, m0)


def _mla_prep_s_kernel(qc_ref, kvc_ref, sm_ref, gq_ref, gkv_ref, wq_ref, wukt_ref,
                       cq_ref, sqa_ref, sqb_ref, ck_ref, ska_ref, skb_ref,
                       ql_out, qp_out, lat_out, kr_out):
    qn = (_rms(qc_ref[...]) * gq_ref[...]).astype(BF16)
    lat_out[...] = _rms(kvc_ref[...]) * gkv_ref[...]
    krf = _rope_lanes(sm_ref[...], ck_ref[...], ska_ref[...], skb_ref[...])
    kr_out[...] = krf[:, :MLA_ROPE]
    cq, sqa, sqb = cq_ref[...], sqa_ref[...], sqb_ref[...]
    for h in range(MLA_HEADS):
        qh = _rope_lanes(jnp.dot(qn, wq_ref[h], preferred_element_type=F32), cq, sqa, sqb)
        ql_out[h] = _bdot(qh, wukt_ref[h])
        qp_out[h] = pltpu.roll(qh, LANES - MLA_NOPE, 1)


def mla_prep_s(z, gq, gkv, wq, wukt, tabs):
    nb = z.shape[0]
    hh = MLA_HEADS
    row = lambda w, off: pl.BlockSpec((nb, w), lambda i, o=off // w: (0, o))
    c2 = lambda a: pl.BlockSpec(a.shape, lambda i: (0, 0))
    c3 = lambda a: pl.BlockSpec(a.shape, lambda i: (0, 0, 0))
    return pl.pallas_call(
        _mla_prep_s_kernel,
        out_shape=(jax.ShapeDtypeStruct((hh, nb, LANES), F32), jax.ShapeDtypeStruct((hh, nb, LANES), F32),
                   jax.ShapeDtypeStruct((nb, KV_LORA), F32), jax.ShapeDtypeStruct((nb, MLA_ROPE), F32)),
        grid=(1,),
        in_specs=[row(Q_LORA, Z_QC), row(KV_LORA, Z_KVC), row(LANES, Z_SMALL),
                  c2(gq), c2(gkv), c3(wq), c3(wukt)] + [c2(t) for t in tabs],
        out_specs=(pl.BlockSpec((hh, nb, LANES), lambda i: (0, 0, 0)),
                   pl.BlockSpec((hh, nb, LANES), lambda i: (0, 0, 0)),
                   pl.BlockSpec((nb, KV_LORA), lambda i: (0, 0)), pl.BlockSpec((nb, MLA_ROPE), lambda i: (0, 0))),
        compiler_params=_cp("arbitrary"),
        name="mla_prep_sample",
    )(z, z, z, gq, gkv, wq, wukt, *tabs)


def _mla_decode_kernel(pt_ref, ql_ref, qp_ref, ln_ref, kn_ref, lat_hbm, kr_hbm, o_ref, lat_buf, kr_buf, sem,
                       *, layer, n_pages):
    b = pl.program_id(0)
    slot = lax.rem(b, 2)

    def page_copies(seq, sl, p):
        page = pt_ref[seq, p]
        off = pl.multiple_of(p * PAGE_SIZE, PAGE_SIZE)
        return (pltpu.make_async_copy(lat_hbm.at[layer, page], lat_buf.at[sl, pl.ds(off, PAGE_SIZE), :],
                                      sem.at[0, sl]),
                pltpu.make_async_copy(kr_hbm.at[layer, page], kr_buf.at[sl, :, pl.ds(off, PAGE_SIZE)],
                                      sem.at[1, sl]))

    def start_all(seq, sl):
        def body(p, carry):
            for c in page_copies(seq, sl, p):
                c.start()
            return carry
        lax.fori_loop(0, n_pages, body, 0)

    def wait_all(seq, sl):
        def body(p, carry):
            for c in page_copies(seq, sl, p):
                c.wait()
            return carry
        lax.fori_loop(0, n_pages, body, 0)

    @pl.when(b == 0)
    def _():
        start_all(0, 0)

    @pl.when(b + 1 < pl.num_programs(0))
    def _():
        start_all(b + 1, 1 - slot)

    wait_all(b, slot)

    ql = ql_ref[0]
    qp = qp_ref[0][:, :MLA_ROPE]
    lat = lat_buf[slot]
    latb = lat.astype(BF16)
    s = (lax.dot_general(ql.astype(BF16), latb, (((1,), (1,)), ((), ())), preferred_element_type=F32)
         + jnp.dot(qp.astype(BF16), kr_buf[slot].astype(BF16), preferred_element_type=F32)) * MLA_SCALE
    lat_new = ln_ref[0]
    s_new = (jnp.sum(ql * lat_new, axis=1, keepdims=True)
             + jnp.sum(qp * kn_ref[0], axis=1, keepdims=True)) * MLA_SCALE
    m = jnp.maximum(jnp.max(s, axis=1, keepdims=True), s_new)
    p = jnp.exp(s - m)
    pn = jnp.exp(s_new - m)
    num = jnp.dot(p.astype(BF16), latb, preferred_element_type=F32) + pn * lat_new
    o_ref[0] = num / (jnp.sum(p, axis=1, keepdims=True) + pn)


def mla_decode(page_table, ql, qp, lat_new, kr_new, cache_lat, cache_rope_t, layer):
    nb, n_pages = page_table.shape
    past = n_pages * PAGE_SIZE
    hh = MLA_HEADS
    seq = lambda w: pl.BlockSpec((1, hh, w), lambda b, pt: (b, 0, 0))
    new = lambda w: pl.BlockSpec((1, 1, w), lambda b, pt: (b, 0, 0))
    return pl.pallas_call(
        functools.partial(_mla_decode_kernel, layer=layer, n_pages=n_pages),
        out_shape=jax.ShapeDtypeStruct((nb, hh, KV_LORA), F32),
        grid_spec=pltpu.PrefetchScalarGridSpec(
            num_scalar_prefetch=1,
            grid=(nb,),
            in_specs=[seq(LANES), seq(LANES), new(KV_LORA), new(MLA_ROPE),
                      pl.BlockSpec(memory_space=pl.ANY), pl.BlockSpec(memory_space=pl.ANY)],
            out_specs=pl.BlockSpec((1, hh, KV_LORA), lambda b, pt: (b, 0, 0)),
            scratch_shapes=[pltpu.VMEM((2, past, KV_LORA), F32), pltpu.VMEM((2, MLA_ROPE, past), F32),
                            pltpu.SemaphoreType.DMA((2, 2))]),
        compiler_params=_cp("arbitrary"),
        name="mla_decode",
    )(page_table, ql, qp, lat_new, kr_new, cache_lat, cache_rope_t)


def _uv_proj_kernel(o_ref, w_ref, a_ref):
    a_ref[...] = _bdot(o_ref[...], w_ref[0]).astype(BF16)


def uv_proj(o_lat, wuv):
    nb = o_lat.shape[0]
    return pl.pallas_call(
        _uv_proj_kernel,
        out_shape=jax.ShapeDtypeStruct((nb, MLA_HEADS * LANES), BF16),
        grid=(MLA_HEADS,),
        in_specs=[pl.BlockSpec((nb, KV_LORA), lambda h: (0, h)),
                  pl.BlockSpec((1, KV_LORA, LANES), lambda h: (h, 0, 0))],
        out_specs=pl.BlockSpec((nb, LANES), lambda h: (0, h)),
        compiler_params=_cp("arbitrary"),
        name="uv_proj",
    )(o_lat, wuv)


def _merge_kernel(a_ref, b_ref, c_ref, ga_ref, gb_ref, gc_ref, x_ref, g1_ref, wa_ref, wb_ref, wc_ref, wo_ref,
                  nf_ref, sc_ref, sh_ref, x1_ref, h2_ref):
    merged = (_sigmoid(ga_ref[...]) * jnp.dot(a_ref[...], wa_ref[...], preferred_element_type=F32)
              + _sigmoid(gb_ref[...]) * jnp.dot(b_ref[...], wb_ref[...], preferred_element_type=F32)
              + _sigmoid(gc_ref[...]) * jnp.dot(c_ref[...], wc_ref[...], preferred_element_type=F32))
    mix = _bdot(merged, wo_ref[...])
    x1 = x_ref[...] + g1_ref[...] * mix
    x1_ref[...] = x1
    h2 = _rms(x1) * nf_ref[...] * (1.0 + sc_ref[...]) + sh_ref[...]
    h2_ref[...] = h2.astype(BF16)


def merge(a, b, c, z, x, gate1, wa, wb, wc, wo, nf, scale2, shift2, tm):
    t, d = x.shape
    per_tok = gate1.shape[0] != 1
    mod_spec = (pl.BlockSpec((tm, d), lambda i: (i, 0)) if per_tok else pl.BlockSpec((1, d), lambda i: (0, 0)))
    rowb = lambda w, o=0: pl.BlockSpec((tm, w), lambda i, o=o: (i, o))
    c2 = lambda arr: pl.BlockSpec(arr.shape, lambda i: (0, 0))
    return pl.pallas_call(
        _merge_kernel,
        out_shape=(jax.ShapeDtypeStruct((t, d), F32), jax.ShapeDtypeStruct((t, d), BF16)),
        grid=(t // tm,),
        in_specs=[rowb(a.shape[1]), rowb(b.shape[1]), rowb(c.shape[1]),
                  rowb(d, Z_GA // d), rowb(d, Z_GB // d), rowb(d, Z_GC // d), rowb(d), mod_spec,
                  c2(wa), c2(wb), c2(wc), c2(wo), pl.BlockSpec((1, d), lambda i: (0, 0)), mod_spec, mod_spec],
        out_specs=(rowb(d), rowb(d)),
        compiler_params=_cp("arbitrary"),
        name="merge",
    )(a, b, c, z, z, z, x, gate1, wa, wb, wc, wo, nf.reshape(1, d), scale2, shift2)


def _top_rows(x, n):
    rows = []
    for _ in range(n):
        m = jnp.max(x, axis=0, keepdims=True)
        rows.append(m)
        x = jnp.where(x == m, -jnp.inf, x)
    return rows


def _peer_topk_kernel(h2_ref, wq_ref, k1_ref, k2_ref, thr_out, s2_out, l1_out):
    h2 = h2_ref[...]
    k1 = PEER_TOPK + 1
    for h in range(PEER_HEADS):
        qh = jnp.dot(h2, wq_ref[h], preferred_element_type=F32)
        s1 = _bdot_nt(k1_ref[h], qh[:, :PEER_HALF])
        s2 = _bdot_nt(k2_ref[h], qh[:, PEER_HALF:])
        a = _top_rows(s1, k1)
        b = _top_rows(s2, k1)
        n_rows = -(-k1 // SUBLANES) * SUBLANES
        pad_rows = [jnp.full_like(b[0], -jnp.inf)] * (n_rows - k1)
        b_all = jnp.concatenate(b + pad_rows, axis=0)
        a_all = jnp.concatenate(a + pad_rows, axis=0)
        split = 4
        span = lambda r: -(-(k1 // (r + 1)) // SUBLANES) * SUBLANES
        cand = [a[p] + b_all[:span(p)] for p in range(split)]
        cand += [b[q] + a_all[:span(q)] for q in range(k1 // (split + 1))]
        best = _top_rows(jnp.concatenate(cand, axis=0), k1)
        zsum = jnp.zeros_like(best[0])
        for c in best[:PEER_TOPK]:
            zsum = zsum + jnp.exp(c - best[0])
        mid = 0.5 * (best[PEER_TOPK - 1] + best[PEER_TOPK])
        tabs = ((thr_out, ((mid - s1) - b[0]) * LOG2E), (s2_out, (s2 - b[0]) * LOG2E),
                (l1_out, (s1 - a[0]) * LOG2E - jnp.log(zsum) * LOG2E))
        for ref, val in tabs:
            for c in range(val.shape[1] // LANES):
                ref[h, c] = val[:, c * LANES:(c + 1) * LANES]


def peer_topk(h2, wq, k1, k2, tm):
    t, d = h2.shape
    hh = PEER_HEADS
    c3 = lambda a: pl.BlockSpec(a.shape, lambda i: (0, 0, 0))
    assert tm % LANES == 0
    tab = jax.ShapeDtypeStruct((hh, t // LANES, PEER_KEYS, LANES), F32)
    tab_spec = pl.BlockSpec((hh, tm // LANES, PEER_KEYS, LANES), lambda i: (0, i, 0, 0))
    return pl.pallas_call(
        _peer_topk_kernel,
        out_shape=(tab, tab, tab),
        grid=(t // tm,),
        in_specs=[pl.BlockSpec((tm, d), lambda i: (i, 0)), c3(wq), c3(k1), c3(k2)],
        out_specs=(tab_spec, tab_spec, tab_spec),
        compiler_params=_cp("arbitrary"),
        name="peer_topk",
    )(h2, wq, k1, k2)


def _peer_dense_kernel(h2_ref, u_ref, vt_ref, thr_ref, l1_ref, s2_ref, x1_ref, g2_ref,
                       o_ref, acc_ref, act_ref, wg_ref, *, ni):
    e = pl.program_id(1)

    @pl.when(e == 0)
    def _():
        acc_ref[...] = jnp.zeros(acc_ref.shape, F32)

    act_ref[...] = _gelu_tanh(lax.dot_general(u_ref[...], h2_ref[...], (((1,), (1,)), ((), ())),
                                              preferred_element_type=F32))
    tm = act_ref.shape[1]
    for c in range(tm // LANES):
        cs = slice(c * LANES, (c + 1) * LANES)
        for ii in range(ni):
            rs = slice(ii * PEER_KEYS, (ii + 1) * PEER_KEYS)
            w = jnp.zeros((PEER_KEYS, LANES), F32)
            for h in range(PEER_HEADS):
                s2 = s2_ref[h, c]
                w = w + jnp.where(s2 >= thr_ref[h, c, ii:ii + 1, :],
                                  jnp.exp2(s2 + l1_ref[h, c, ii:ii + 1, :]), 0.0)
            wg_ref[rs, cs] = (w * act_ref[rs, cs]).astype(BF16)
    acc_ref[...] += jnp.dot(vt_ref[...], wg_ref[...], preferred_element_type=F32)

    @pl.when(e == pl.num_programs(1) - 1)
    def _():
        o_ref[...] = x1_ref[...] + g2_ref[...] * acc_ref[...].T


def peer_dense(h2, u, vt, thr, l1, s2, x1, gate2, tm, ni):
    t, d = h2.shape
    n_exp = u.shape[0]
    te = ni * PEER_KEYS
    hh = PEER_HEADS
    per_tok = gate2.shape[0] != 1
    mod_spec = (pl.BlockSpec((tm, d), lambda i, e: (i, 0)) if per_tok
                else pl.BlockSpec((1, d), lambda i, e: (0, 0)))
    nc = tm // LANES
    sub = pl.BlockSpec((hh, nc, ni, LANES), lambda i, e: (0, i, e, 0))
    full = pl.BlockSpec((hh, nc, PEER_KEYS, LANES), lambda i, e: (0, i, 0, 0))
    return pl.pallas_call(
        functools.partial(_peer_dense_kernel, ni=ni),
        out_shape=jax.ShapeDtypeStruct((t, d), F32),
        grid=(t // tm, n_exp // te),
        in_specs=[pl.BlockSpec((tm, d), lambda i, e: (i, 0)),
                  pl.BlockSpec((te, d), lambda i, e: (e, 0)),
                  pl.BlockSpec((d, te), lambda i, e: (0, e)),
                  sub, sub, full,
                  pl.BlockSpec((tm, d), lambda i, e: (i, 0)), mod_spec],
        out_specs=pl.BlockSpec((tm, d), lambda i, e: (i, 0)),
        scratch_shapes=[pltpu.VMEM((d, tm), F32), pltpu.VMEM((te, tm), F32), pltpu.VMEM((te, tm), BF16)],
        compiler_params=_cp("arbitrary", "arbitrary"),
        name="peer_dense",
    )(h2, u, vt, thr, l1, s2, x1, gate2)


def _final_norm_kernel(x_ref, g_ref, o_ref):
    o_ref[...] = _rms(x_ref[...]) * g_ref[...]


def final_norm(x, g, tm):
    t, d = x.shape
    return pl.pallas_call(
        _final_norm_kernel,
        out_shape=jax.ShapeDtypeStruct((t, d), F32),
        grid=(t // tm,),
        in_specs=[pl.BlockSpec((tm, d), lambda i: (i, 0)), pl.BlockSpec((1, d), lambda i: (0, 0))],
        out_specs=pl.BlockSpec((tm, d), lambda i: (i, 0)),
        compiler_params=_cp("arbitrary"),
        name="final_norm",
    )(x, g.reshape(1, d))


def _pad_to(a, axis, size):
    pad = [(0, 0)] * a.ndim
    pad[axis] = (0, size - a.shape[axis])
    return jnp.pad(a, pad)


def _prep_w_in(w):
    sizes = (Q_LORA, KV_LORA, MLA_ROPE, 256, 256, 512, GLA_GATE_RANK, 512, 256, 256, 512, ML_HEADS, ML_HEADS, 512,
             D_MODEL, D_MODEL, D_MODEL)
    offs = np.concatenate([[0], np.cumsum(sizes)])
    (q_c, kv_c, k_pe, gq, gk, gv, ga, gg, mq, mk, mv, mi, mf, mo, g_a, g_b, g_c) = [
        w[:, offs[i]:offs[i + 1]] for i in range(len(sizes))]
    small = _pad_to(jnp.concatenate([k_pe, ga, mi, mf], axis=1), 1, LANES)
    out = jnp.concatenate([g_a, g_b, g_c, gv, gg, mv, mo, q_c, gq, gk, mq, mk, kv_c, small], axis=1)
    assert out.shape[1] == Z_WIDTH
    return out.astype(BF16)


def _rope_tables(pos):
    half = MLA_ROPE // 2
    inv = ROPE_BASE ** (-jnp.arange(half, dtype=F32) / half)
    ang = pos.astype(F32)[:, None] * inv[None, :]
    cos, sin = jnp.cos(ang), jnp.sin(ang)
    n = pos.shape[0]
    z = lambda w: jnp.zeros((n, w), F32)
    one = jnp.ones((n, MLA_NOPE), F32)
    rest = LANES - MLA_NOPE - MLA_ROPE
    cq = jnp.concatenate([one, cos, cos, z(rest)], axis=1)
    sqa = jnp.concatenate([z(MLA_NOPE), -sin, z(half), z(rest)], axis=1)
    sqb = jnp.concatenate([z(MLA_NOPE), z(half), sin, z(rest)], axis=1)
    ck = jnp.concatenate([cos, cos, z(LANES - MLA_ROPE)], axis=1)
    ska = jnp.concatenate([-sin, z(LANES - half)], axis=1)
    skb = jnp.concatenate([z(half), sin, z(LANES - MLA_ROPE)], axis=1)
    return (cq, sqa, sqb, ck, ska, skb), (cos.T, sin.T)


def _layer_weights(l, P):
    w = {}
    w['w_in'] = _prep_w_in(P['w_in'][l])
    w['gq'] = P['mla_q_norm'][l].reshape(1, Q_LORA)
    w['gkv'] = P['mla_kv_norm'][l].reshape(1, KV_LORA)
    w['wq'] = _pad_to(jnp.transpose(P['w_uq'][l], (1, 0, 2)), 2, LANES).astype(BF16)
    wuk = jnp.transpose(P['w_uk'][l], (1, 0, 2))
    w['wuk'] = _pad_to(wuk, 2, LANES).astype(BF16)
    w['wukt'] = _pad_to(jnp.transpose(wuk, (0, 2, 1)), 1, LANES).astype(BF16)
    w['wuv'] = _pad_to(jnp.transpose(P['w_uv'][l], (1, 0, 2)), 2, LANES).astype(BF16)
    w['wqt'] = _pad_to(jnp.transpose(P['w_uq'][l], (1, 2, 0)), 1, LANES).astype(BF16)
    w['wuvt'] = _pad_to(jnp.transpose(P['w_uv'][l], (1, 2, 0)), 1, LANES).astype(BF16)
    wa = jnp.zeros((LANES, GLA_HEADS * GLA_DK), F32).at[SM_GA:SM_GA + GLA_GATE_RANK].set(P['w_gla_a'][l])
    w['gla_wa'] = wa.astype(BF16)
    w['gla_wat'] = wa.T.astype(BF16)
    w['gla_ba'] = P['b_gla_a'][l].reshape(1, -1)
    w['gla_bat'] = P['b_gla_a'][l].reshape(-1, 1)
    w['gla_norm'] = P['gla_norm'][l].reshape(1, -1)
    w['ml_norm'] = P['ml_norm'][l].reshape(1, -1)
    brow = jnp.zeros((1, LANES), F32)
    brow = brow.at[0, SM_MI:SM_MI + ML_HEADS].set(P['ml_b_i'][l]).at[0, SM_MF:SM_MF + ML_HEADS].set(P['ml_b_f'][l])
    w['ml_brow'] = brow
    w['ml_bcol'] = jnp.concatenate([P['ml_b_i'][l], P['ml_b_f'][l]]).reshape(2 * ML_HEADS, 1)
    wpa = P['w_proj_a'][l].reshape(MLA_HEADS, MLA_V, D_MODEL)
    w['wpa'] = _pad_to(wpa, 1, LANES).reshape(MLA_HEADS * LANES, D_MODEL).astype(BF16)
    w['wpb'] = P['w_proj_b'][l].astype(BF16)
    w['wpc'] = P['w_proj_c'][l].astype(BF16)
    w['wo'] = P['w_o'][l].astype(BF16)
    w['peer_wq'] = jnp.transpose(P['peer_wq'][l], (1, 0, 2)).astype(BF16)
    w['peer_k1'] = P['peer_keys'][l][:, 0].astype(BF16)
    w['peer_k2'] = P['peer_keys'][l][:, 1].astype(BF16)
    w['peer_u'] = P['peer_u'][l].astype(BF16)
    w['peer_vt'] = P['peer_v'][l].T.astype(BF16)
    return w


def _mods(c, w_ada, b_ada):
    rows = c.shape[0]
    cp = _pad_to(c, 0, SUBLANES) if rows < SUBLANES else c
    mod = ada_mod(cp, w_ada, b_ada)[:rows]
    return jnp.split(mod, 6, axis=-1)


def _peer_block(h2, x1, gate2, w, tm, ni):
    thr, s2, l1 = peer_topk(h2, w['peer_wq'], w['peer_k1'], w['peer_k2'], min(tm, 2 * LANES))
    return peer_dense(h2, w['peer_u'], w['peer_vt'], thr, l1, s2, x1, gate2, tm, ni)


def _trunk_prompt(x, c, P, W):
    t = x.shape[0]
    tm = min(512, t)
    L = min(128, t)
    consts = _chunk_constants(L)
    tabs, tabs_t = _rope_tables(jnp.arange(t, dtype=jnp.int32))
    lat_rows, rope_rows, gla_st, c_st, n_st, m_st = [], [], [], [], [], []
    for l in range(len(W)):
        w = W[l]
        shift1, scale1, gate1, shift2, scale2, gate2 = _mods(c, P['w_ada'][l], P['b_ada'][l])
        z = norm_proj(x, P['norm_mix'][l], scale1, shift1, w['w_in'], min(2 * tm, t))
        qt, k, vt, lat, kr = mla_prep(z, w['gq'], w['gkv'], w['wqt'], w['wuk'], w['wuvt'], tabs_t, tabs, tm)
        a = flash_attention(qt, k, vt, tm)
        b, st = gla_scan(z, w['gla_wa'], w['gla_ba'], w['gla_norm'], consts, L)
        smt = z[:, Z_SMALL + SM_MI:Z_SMALL + SM_MI + 2 * ML_HEADS].T
        cc, ct, nrow, mrow = mlstm_scan(z, smt, w['ml_brow'], w['ml_bcol'], w['ml_norm'], consts, L)
        x1, h2 = merge(a, b, cc, z, x, gate1, w['wpa'], w['wpb'], w['wpc'], w['wo'],
                       P['norm_ffn'][l], scale2, shift2, tm)
        x = _peer_block(h2, x1, gate2, w, tm, SUBLANES)
        lat_rows.append(lat)
        rope_rows.append(kr)
        gla_st.append(jnp.stack([st[h][:, h * GLA_DK:(h + 1) * GLA_DK].T for h in range(GLA_HEADS)]))
        c_st.append(jnp.stack([ct[h][:, h * ML_DK:(h + 1) * ML_DK].T for h in range(ML_HEADS)]))
        n_st.append(jnp.stack([nrow[h, h * ML_DK:(h + 1) * ML_DK] for h in range(ML_HEADS)]))
        m_st.append(mrow[:ML_HEADS, 0])
    y = final_norm(x, P['norm_final'], tm)
    return y, tuple(jnp.stack(r) for r in (lat_rows, rope_rows, gla_st, c_st, n_st, m_st))


def _trunk_sample(x, c, P, W, cache_lat, cache_rope, page_table, s_gla, s_c, s_n, s_m):
    nb = x.shape[0]
    past = page_table.shape[1] * PAGE_SIZE
    tabs, _ = _rope_tables(jnp.full((1,), past, dtype=jnp.int32))
    cache_rope = jnp.swapaxes(cache_rope, 2, 3)
    lat_rows, rope_rows, gla_st, c_st, n_st, m_st = [], [], [], [], [], []
    for l in range(len(W)):
        w = W[l]
        shift1, scale1, gate1, shift2, scale2, gate2 = _mods(c, P['w_ada'][l], P['b_ada'][l])
        z = norm_proj(x, P['norm_mix'][l], scale1, shift1, w['w_in'], nb)
        ql, qp, lat, kr = mla_prep_s(z, w['gq'], w['gkv'], w['wq'], w['wukt'], tabs)
        o_lat = mla_decode(page_table, jnp.transpose(ql, (1, 0, 2)), jnp.transpose(qp, (1, 0, 2)),
                           lat[:, None, :], kr[:, None, :], cache_lat, cache_rope, l)
        a = uv_proj(o_lat.reshape(nb, MLA_HEADS * KV_LORA), w['wuv'])
        m_in = jnp.zeros((nb, LANES), F32).at[:, SM_MI:SM_MI + ML_HEADS].set(s_m[l])
        b, cc, s_new, c_new, n_new, m_new = rec_step(
            z, z.T, w['gla_wa'], w['gla_ba'], w['gla_wat'], w['gla_bat'], w['ml_brow'], w['gla_norm'], w['ml_norm'],
            s_gla[l], s_c[l], s_n[l].reshape(nb, ML_HEADS * ML_DK), m_in)
        x1, h2 = merge(a, b, cc, z, x, gate1, w['wpa'], w['wpb'], w['wpc'], w['wo'],
                       P['norm_ffn'][l], scale2, shift2, nb)
        x = _peer_block(h2, x1, gate2, w, nb, SUBLANES)
        lat_rows.append(lat[:, None, :])
        rope_rows.append(kr[:, None, :])
        gla_st.append(s_new)
        c_st.append(c_new)
        n_st.append(n_new.reshape(nb, ML_HEADS, ML_DK))
        m_st.append(m_new[:, SM_MI:SM_MI + ML_HEADS])
    y = final_norm(x, P['norm_final'], nb)
    return y, tuple(jnp.stack(r) for r in (lat_rows, rope_rows, gla_st, c_st, n_st, m_st))


def kernel(x_prompt, x_sample, cache_kv_latent, cache_k_rope, state_gla, state_mlstm_C, state_mlstm_n, state_mlstm_m, page_table, c_prompt, c_sample, w_ada, b_ada, norm_mix, norm_ffn, norm_final, w_in, mla_q_norm, mla_kv_norm, w_uq, w_uk, w_uv, w_gla_a, b_gla_a, gla_norm, ml_b_i, ml_b_f, ml_norm, w_proj_a, w_proj_b, w_proj_c, w_o, peer_wq, peer_keys, peer_u, peer_v):
    P = {'w_ada': w_ada, 'b_ada': b_ada, 'norm_mix': norm_mix, 'norm_ffn': norm_ffn, 'norm_final': norm_final,
         'w_in': w_in, 'mla_q_norm': mla_q_norm, 'mla_kv_norm': mla_kv_norm, 'w_uq': w_uq, 'w_uk': w_uk,
         'w_uv': w_uv, 'w_gla_a': w_gla_a, 'b_gla_a': b_gla_a, 'gla_norm': gla_norm, 'ml_b_i': ml_b_i,
         'ml_b_f': ml_b_f, 'ml_norm': ml_norm, 'w_proj_a': w_proj_a, 'w_proj_b': w_proj_b,
         'w_proj_c': w_proj_c, 'w_o': w_o, 'peer_wq': peer_wq, 'peer_keys': peer_keys, 'peer_u': peer_u,
         'peer_v': peer_v}
    depth = w_in.shape[0]
    W = [_layer_weights(l, P) for l in range(depth)]
    bp, sp, d = x_prompt.shape
    assert bp == 1 and x_sample.shape[1] == 1
    nb = x_sample.shape[0]
    y_p, (lat_p, rope_p, gla_p, c_p, n_p, m_p) = _trunk_prompt(x_prompt.reshape(sp, d), c_prompt, P, W)
    y_s, (lat_s, rope_s, gla_s, c_s, n_s, m_s) = _trunk_sample(
        x_sample.reshape(nb, d), c_sample, P, W, cache_kv_latent, cache_k_rope, page_table,
        state_gla, state_mlstm_C, state_mlstm_n, state_mlstm_m)
    return (y_p.reshape(bp, sp, d), y_s.reshape(nb, 1, d),
            lat_p[:, None], rope_p[:, None], gla_p[:, None], c_p[:, None], n_p[:, None], m_p[:, None],
            lat_s, rope_s, gla_s, c_s, n_s, m_s)
```

```python
import functools

import numpy as np
import jax
import jax.numpy as jnp
from jax import lax
from jax.experimental import pallas as pl
from jax.experimental.pallas import tpu as pltpu

F32 = jnp.float32
BF16 = jnp.bfloat16

D_MODEL = 1024
PAGE_SIZE = 128
MLA_HEADS = 8
MLA_NOPE = 64
MLA_ROPE = 32
MLA_V = 64
Q_LORA = 256
KV_LORA = 128
ROPE_BASE = 10000.0
MLA_SCALE = (MLA_NOPE + MLA_ROPE) ** -0.5
GLA_HEADS = 4
GLA_DK = 64
GLA_DV = 128
GLA_GATE_RANK = 16
GLA_GATE_TEMP = 16.0
ML_HEADS = 4
ML_DK = 64
ML_DV = 128
PEER_HEADS = 8
PEER_KEYS = 128
PEER_HALF = 128
PEER_TOPK = 16
NORM_EPS = 1e-6

LANES = 128
SUBLANES = 8
VMEM_LIMIT = 48 * 1024 * 1024

NEG = -1e30
LOG2E = 1.4426950408889634

Z_GA, Z_GB, Z_GC = 0, 1024, 2048
Z_GV, Z_GG, Z_MV, Z_MO = 3072, 3584, 4096, 4608
Z_QC, Z_GQ, Z_GK, Z_MQ, Z_MK = 5120, 5376, 5632, 5888, 6144
Z_KVC, Z_SMALL = 6400, 6528
Z_WIDTH = 6656
SM_KPE, SM_GA, SM_MI, SM_MF = 0, 32, 48, 52


def _cp(*sem):
    return pltpu.CompilerParams(dimension_semantics=tuple(sem), vmem_limit_bytes=VMEM_LIMIT)


def _bdot(a, b):
    return jnp.dot(a.astype(BF16), b.astype(BF16), preferred_element_type=F32)


def _bdot_nt(a, b):
    return lax.dot_general(a.astype(BF16), b.astype(BF16), (((1,), (1,)), ((), ())),
                           preferred_element_type=F32)


def _split3(x):
    hi = x.astype(BF16)
    r = x - hi.astype(F32)
    mid = r.astype(BF16)
    lo = (r - mid.astype(F32)).astype(BF16)
    return hi, mid, lo


def _dot01(m01, x):
    hi, mid, lo = _split3(x)
    return (jnp.dot(m01, hi, preferred_element_type=F32) + jnp.dot(m01, mid, preferred_element_type=F32)
            + jnp.dot(m01, lo, preferred_element_type=F32))


def _dot01_r(x, m01):
    hi, mid, lo = _split3(x)
    return (jnp.dot(hi, m01, preferred_element_type=F32) + jnp.dot(mid, m01, preferred_element_type=F32)
            + jnp.dot(lo, m01, preferred_element_type=F32))


def _sigmoid(x):
    return 1.0 / (1.0 + jnp.exp(-x))


def _log_sigmoid(x):
    return jnp.minimum(x, 0.0) - jnp.log(1.0 + jnp.exp(-jnp.abs(x)))


def _rms(x):
    return x * lax.rsqrt(jnp.mean(x * x, axis=-1, keepdims=True) + NORM_EPS)


def _gelu_tanh(x):
    return 0.5 * x * (1.0 + jnp.tanh(0.7978845608028654 * (x + 0.044715 * (x * x * x))))


def _ada_kernel(c_ref, w_ref, b_ref, o_ref):
    c = c_ref[...]
    o_ref[...] = _bdot(c * _sigmoid(c), w_ref[...]) + b_ref[...]


def ada_mod(c, w, b, layer):
    rows, d = c.shape
    n = w.shape[2]
    tn = 1536
    return pl.pallas_call(
        _ada_kernel,
        out_shape=jax.ShapeDtypeStruct((rows, n), F32),
        grid=(n // tn,),
        in_specs=[pl.BlockSpec((rows, d), lambda j: (0, 0)),
                  pl.BlockSpec((None, d, tn), lambda j: (layer, 0, j)),
                  pl.BlockSpec((1, tn), lambda j: (0, j))],
        out_specs=pl.BlockSpec((rows, tn), lambda j: (0, j)),
        compiler_params=_cp("arbitrary"),
        name="ada_mod",
    )(c, w, b.reshape(1, n))


def _norm_proj_kernel(x_ref, g_ref, sc_ref, sh_ref, w_ref, o_ref, st_ref, hn_ref, *, tn):
    @pl.when(pl.program_id(1) == 0)
    def _():
        h = _rms(x_ref[...]) * g_ref[...] * (1.0 + sc_ref[...]) + sh_ref[...]
        hn_ref[...] = h.astype(BF16)

    res = jnp.dot(hn_ref[...], w_ref[...], preferred_element_type=F32)
    o_ref[...] = res

    @pl.when(pl.program_id(1) == Z_SMALL // tn)
    def _():
        off = Z_SMALL % tn
        st_ref[...] = res[:, off:off + LANES].T


def norm_proj(x, g, scale, shift, w, tm):
    t, d = x.shape
    n = w.shape[1]
    tn = 512
    per_tok = scale.shape[0] != 1
    mod_spec = (pl.BlockSpec((tm, d), lambda i, j: (i, 0)) if per_tok
                else pl.BlockSpec((1, d), lambda i, j: (0, 0)))
    return pl.pallas_call(
        functools.partial(_norm_proj_kernel, tn=tn),
        out_shape=(jax.ShapeDtypeStruct((t, n), F32), jax.ShapeDtypeStruct((LANES, t), F32)),
        grid=(t // tm, n // tn),
        in_specs=[pl.BlockSpec((tm, d), lambda i, j: (i, 0)),
                  pl.BlockSpec((1, d), lambda i, j: (0, 0)),
                  mod_spec, mod_spec,
                  pl.BlockSpec((d, tn), lambda i, j: (0, j))],
        out_specs=(pl.BlockSpec((tm, tn), lambda i, j: (i, j)), pl.BlockSpec((LANES, tm), lambda i, j: (0, i))),
        scratch_shapes=[pltpu.VMEM((tm, d), BF16)],
        compiler_params=_cp("arbitrary", "arbitrary"),
        name="norm_proj",
    )(x, g.reshape(1, d), scale, shift, w)


def _rope_lanes(x, c, sa, sb):
    return x * c + pltpu.roll(x, LANES - 16, 1) * sa + pltpu.roll(x, 16, 1) * sb


def _mla_prep_kernel(qc_ref, kvc_ref, sm_ref, gq_ref, gkv_ref, wqt_ref, wuk_ref, wuvt_ref,
                     ct_ref, st_ref, ck_ref, ska_ref, skb_ref,
                     qt_out, k_out, vt_out, lat_out, kr_out):
    half = MLA_ROPE // 2
    qn = (_rms(qc_ref[...]) * gq_ref[...]).astype(BF16)
    lat = _rms(kvc_ref[...]) * gkv_ref[...]
    lat_out[...] = lat
    latb = lat.astype(BF16)
    krf = _rope_lanes(sm_ref[...], ck_ref[...], ska_ref[...], skb_ref[...])
    kr_out[...] = krf[:, :MLA_ROPE]
    kr_sh = pltpu.roll(krf, MLA_NOPE, 1)
    cos_t, sin_t = ct_ref[...], st_ref[...]
    tm = qn.shape[0]
    ones_row = (lax.broadcasted_iota(jnp.int32, (LANES, tm), 0) == MLA_V).astype(F32)
    pad = jnp.zeros((LANES - MLA_NOPE - MLA_ROPE, tm), F32)
    for h in range(MLA_HEADS):
        qt = lax.dot_general(wqt_ref[h], qn, (((1,), (1,)), ((), ())), preferred_element_type=F32)
        x1 = qt[MLA_NOPE:MLA_NOPE + half]
        x2 = qt[MLA_NOPE + half:MLA_NOPE + MLA_ROPE]
        qt = jnp.concatenate([qt[:MLA_NOPE], x1 * cos_t - x2 * sin_t, x1 * sin_t + x2 * cos_t, pad], axis=0)
        qt_out[h] = (qt * (MLA_SCALE * LOG2E)).astype(BF16)
        k_out[h] = (jnp.dot(latb, wuk_ref[h], preferred_element_type=F32) + kr_sh).astype(BF16)
        vt = lax.dot_general(wuvt_ref[h], latb, (((1,), (1,)), ((), ())), preferred_element_type=F32)
        vt_out[h] = (vt + ones_row).astype(BF16)


def mla_prep(z, gq, gkv, wqt, wuk, wuvt, tabs_t, tabs, tm):
    t = z.shape[0]
    hh = MLA_HEADS
    row = lambda w, off: pl.BlockSpec((tm, w), lambda i, o=off // w: (i, o))
    const2 = lambda a: pl.BlockSpec(a.shape, lambda i: (0, 0))
    const3 = lambda a: pl.BlockSpec(a.shape, lambda i: (0, 0, 0))
    tab_spec = pl.BlockSpec((tm, LANES), lambda i: (i, 0))
    tabt_spec = pl.BlockSpec((MLA_ROPE // 2, tm), lambda i: (0, i))
    return pl.pallas_call(
        _mla_prep_kernel,
        out_shape=(jax.ShapeDtypeStruct((hh, LANES, t), BF16),
                   jax.ShapeDtypeStruct((hh, t, LANES), BF16),
                   jax.ShapeDtypeStruct((hh, LANES, t), BF16),
                   jax.ShapeDtypeStruct((t, KV_LORA), F32),
                   jax.ShapeDtypeStruct((t, MLA_ROPE), F32)),
        grid=(t // tm,),
        in_specs=[row(Q_LORA, Z_QC), row(KV_LORA, Z_KVC), row(LANES, Z_SMALL),
                  const2(gq), const2(gkv), const3(wqt), const3(wuk), const3(wuvt),
                  tabt_spec, tabt_spec, tab_spec, tab_spec, tab_spec],
        out_specs=(pl.BlockSpec((hh, LANES, tm), lambda i: (0, 0, i)),
                   pl.BlockSpec((hh, tm, LANES), lambda i: (0, i, 0)),
                   pl.BlockSpec((hh, LANES, tm), lambda i: (0, 0, i)),
                   pl.BlockSpec((tm, KV_LORA), lambda i: (i, 0)),
                   pl.BlockSpec((tm, MLA_ROPE), lambda i: (i, 0))),
        compiler_params=_cp("arbitrary"),
        name="mla_prep",
    )(z, z, z, gq, gkv, wqt, wuk, wuvt, *tabs_t, *tabs[3:])


def _flash_kernel(qi_ref, ki_ref, qt_ref, k_ref, vt_ref, o_ref, m_ref, acc_ref, *, tq):
    qi = qi_ref[pl.program_id(0)]
    ki = ki_ref[pl.program_id(0)]

    @pl.when(ki == 0)
    def _():
        m_ref[...] = jnp.full(m_ref.shape, NEG, F32)
        acc_ref[...] = jnp.zeros(acc_ref.shape, F32)

    def scores(h):
        return jnp.dot(k_ref[h], qt_ref[h], preferred_element_type=F32)

    def accumulate(h, p, alpha):
        acc_ref[h] = alpha * acc_ref[h] + jnp.dot(vt_ref[h], p, preferred_element_type=F32)

    def step(diag):
        if diag:
            key = lax.broadcasted_iota(jnp.int32, (tq, tq), 0)
            qry = lax.broadcasted_iota(jnp.int32, (tq, tq), 1)
            keep = key <= qry
        s_next = scores(0)
        pending = None
        for h in range(MLA_HEADS):
            s = s_next
            if h + 1 < MLA_HEADS:
                s_next = scores(h + 1)
            if pending is not None:
                accumulate(*pending)
            if diag:
                s = jnp.where(keep, s, NEG)
            m_prev = m_ref[h:h + 1, :]
            m_new = jnp.maximum(m_prev, jnp.max(s, axis=0, keepdims=True))
            p = jnp.exp2(s - m_new).astype(BF16)
            m_ref[h:h + 1, :] = m_new
            pending = (h, p, jnp.exp2(m_prev - m_new))
        accumulate(*pending)

    @pl.when(ki < qi)
    def _():
        step(False)

    @pl.when(ki == qi)
    def _():
        step(True)
        for h in range(MLA_HEADS):
            acc = acc_ref[h]
            o_ref[:, h * LANES:(h + 1) * LANES] = (acc / acc[MLA_V:MLA_V + 1, :]).T.astype(BF16)


def flash_attention(qt, k, vt, tq):
    hh, t, _ = k.shape
    n = t // tq
    pairs = np.array([(i, j) for i in range(n) for j in range(i + 1)], np.int32)
    return pl.pallas_call(
        functools.partial(_flash_kernel, tq=tq),
        out_shape=jax.ShapeDtypeStruct((t, hh * LANES), BF16),
        grid_spec=pltpu.PrefetchScalarGridSpec(
            num_scalar_prefetch=2,
            grid=(pairs.shape[0],),
            in_specs=[pl.BlockSpec((hh, LANES, tq), lambda s, qa, ka: (0, 0, qa[s])),
                      pl.BlockSpec((hh, tq, LANES), lambda s, qa, ka: (0, ka[s], 0)),
                      pl.BlockSpec((hh, LANES, tq), lambda s, qa, ka: (0, 0, ka[s]))],
            out_specs=pl.BlockSpec((tq, hh * LANES), lambda s, qa, ka: (qa[s], 0)),
            scratch_shapes=[pltpu.VMEM((hh, tq), F32), pltpu.VMEM((hh, LANES, tq), F32)]),
        compiler_params=_cp("arbitrary"),
        name="flash_attention",
    )(jnp.asarray(pairs[:, 0]), jnp.asarray(pairs[:, 1]), qt, k, vt)


def _chunk_constants(L):
    nl = int(np.log2(L))
    assert 2 ** nl == L
    idx = np.arange(L)
    tri = (idx[None, :] <= idx[:, None]).astype(np.float32)
    mq, mk, pm = [], [], []
    for lev in range(nl):
        p = 1 << lev
        start_right = (idx >> (lev + 1) << (lev + 1)) + p
        right = ((idx >> lev) & 1) == 1
        u = idx[None, :]
        mq.append((right[:, None] & (u >= start_right[:, None]) & (u <= idx[:, None])).astype(np.float32))
        mk.append(((~right)[:, None] & (u > idx[:, None]) & (u < start_right[:, None])).astype(np.float32))
        same = (idx[:, None] >> (lev + 1)) == (idx[None, :] >> (lev + 1))
        pm.append((same & right[:, None] & (~right)[None, :]).astype(np.float32))
    pm.append(np.eye(L, dtype=np.float32))
    sums = np.concatenate([tri] + [a + b for a, b in zip(mq, mk)], axis=0)
    return jnp.asarray(sums, BF16), jnp.asarray(np.stack(pm), F32), jnp.asarray(tri, BF16), jnp.asarray(tri.T, BF16)


def _head_mask(width, heads, h):
    lane = lax.broadcasted_iota(jnp.int32, (1, width), 1)
    per = width // heads
    return ((lane >= h * per) & (lane < (h + 1) * per)).astype(F32)


def _gla_scan_kernel(gq_ref, gk_ref, gv_ref, gg_ref, sm_ref, wa_ref, ba_ref, sums_ref, pm_ref, gn_ref,
                     o_ref, st_out, st_ref, *, L, nl):
    step = pl.program_id(0)

    @pl.when(step == 0)
    def _():
        st_ref[...] = jnp.zeros(st_ref.shape, F32)

    width = GLA_HEADS * GLA_DK
    x = _bdot(sm_ref[...], wa_ref[...]) + ba_ref[...]
    la = _log_sigmoid(x) * (1.0 / GLA_GATE_TEMP)
    dsum = _dot01(sums_ref[...], la)
    b = dsum[0:L]
    b_last = b[L - 1:L, :]
    q = gq_ref[...] * (GLA_DK ** -0.5)
    k = gk_ref[...]
    q_in = (q * jnp.exp(b)).astype(BF16)
    k_out = k * jnp.exp(b_last - b)
    decay_all = jnp.exp(b_last)
    hms = [_head_mask(width, GLA_HEADS, h) for h in range(GLA_HEADS)]
    atts = [jnp.zeros((L, L), F32) for _ in range(GLA_HEADS)]
    for lev in range(nl + 1):
        fac = jnp.exp(dsum[(1 + lev) * L:(2 + lev) * L]) if lev < nl else None
        ql = q if fac is None else q * fac
        kl = (k if fac is None else k * fac).astype(BF16)
        stacked = jnp.concatenate([ql * hm for hm in hms], axis=0).astype(BF16)
        sc = lax.dot_general(stacked, kl, (((1,), (1,)), ((), ())), preferred_element_type=F32)
        keep = pm_ref[lev] > 0.0
        for h in range(GLA_HEADS):
            atts[h] = atts[h] + jnp.where(keep, sc[h * L:(h + 1) * L], 0.0)
    for h in range(GLA_HEADS):
        hm = hms[h]
        att = atts[h]
        vh = gv_ref[:, h * GLA_DV:(h + 1) * GLA_DV]
        st = st_ref[h]
        o = (jnp.dot(att.astype(BF16), vh.astype(BF16), preferred_element_type=F32)
             + lax.dot_general(q_in, st.astype(BF16), (((1,), (1,)), ((), ())), preferred_element_type=F32))
        st_ref[h] = st * decay_all + jnp.dot(vh.T.astype(BF16), (k_out * hm).astype(BF16),
                                             preferred_element_type=F32)
        gg = gg_ref[:, h * GLA_DV:(h + 1) * GLA_DV]
        y = _rms(o) * gn_ref[:, h * GLA_DV:(h + 1) * GLA_DV] * (gg * _sigmoid(gg))
        o_ref[:, h * GLA_DV:(h + 1) * GLA_DV] = y.astype(BF16)

    @pl.when(step == pl.num_programs(0) - 1)
    def _():
        st_out[...] = st_ref[...]


def gla_scan(z, wa, ba, gn, consts, L):
    t = z.shape[0]
    sums, pm, _, _ = consts
    nl = pm.shape[0] - 1
    width = GLA_HEADS * GLA_DK
    vw = GLA_HEADS * GLA_DV
    row = lambda w, off: pl.BlockSpec((L, w), lambda i, o=off // w: (i, o))
    c2 = lambda a: pl.BlockSpec(a.shape, lambda i: (0, 0))
    c3 = lambda a: pl.BlockSpec(a.shape, lambda i: (0, 0, 0))
    return pl.pallas_call(
        functools.partial(_gla_scan_kernel, L=L, nl=nl),
        out_shape=(jax.ShapeDtypeStruct((t, vw), BF16),
                   jax.ShapeDtypeStruct((GLA_HEADS, GLA_DV, width), F32)),
        grid=(t // L,),
        in_specs=[row(width, Z_GQ), row(width, Z_GK), row(vw, Z_GV), row(vw, Z_GG), row(LANES, Z_SMALL),
                  c2(wa), c2(ba), c2(sums), c3(pm), c2(gn)],
        out_specs=(pl.BlockSpec((L, vw), lambda i: (i, 0)),
                   pl.BlockSpec((GLA_HEADS, GLA_DV, width), lambda i: (0, 0, 0))),
        scratch_shapes=[pltpu.VMEM((GLA_HEADS, GLA_DV, width), F32)],
        compiler_params=_cp("arbitrary"),
        name="gla_scan",
    )(z, z, z, z, z, wa, ba, sums, pm, gn)


def _mlstm_scan_kernel(mq_ref, mk_ref, mv_ref, mo_ref, sm_ref, smt_ref, brow_ref, bcol_ref, tri_ref, trit_ref,
                       nrm_ref, o_ref, ct_out, n_out, m_out, ct_ref, n_ref, m_ref, *, L):
    step = pl.program_id(0)

    @pl.when(step == 0)
    def _():
        ct_ref[...] = jnp.zeros(ct_ref.shape, F32)
        n_ref[...] = jnp.zeros(n_ref.shape, F32)
        m_ref[...] = jnp.zeros(m_ref.shape, F32)

    width = ML_HEADS * ML_DK
    pre_c = sm_ref[...] + brow_ref[...]
    pre_r = smt_ref[...] + bcol_ref[...]
    b_c_all = _dot01(tri_ref[...], _log_sigmoid(pre_c))
    b_r_all = _dot01_r(_log_sigmoid(pre_r), trit_ref[...])
    q = mq_ref[...]
    k = mk_ref[...] * (ML_DK ** -0.5)
    kb = k.astype(BF16)
    row = lax.broadcasted_iota(jnp.int32, (L, L), 0)
    col = lax.broadcasted_iota(jnp.int32, (L, L), 1)
    causal = col <= row
    for h in range(ML_HEADS):
        hm = _head_mask(width, ML_HEADS, h)
        bc = b_c_all[:, SM_MF + h:SM_MF + h + 1]
        br = b_r_all[ML_HEADS + h:ML_HEADS + h + 1, :]
        ic = pre_c[:, SM_MI + h:SM_MI + h + 1]
        ir = pre_r[h:h + 1, :]
        m_prev = m_ref[h:h + 1, 0:1]
        dlog = jnp.where(causal, bc - br + ir, NEG)
        inter = bc + m_prev
        mt = jnp.maximum(inter, jnp.max(dlog, axis=1, keepdims=True))
        qh = (q * hm).astype(BF16)
        qk = lax.dot_general(qh, kb, (((1,), (1,)), ((), ())), preferred_element_type=F32)
        wi = jnp.exp(dlog - mt) * qk
        wint = jnp.exp(inter - mt)
        vh = mv_ref[:, h * ML_DV:(h + 1) * ML_DV]
        ct = ct_ref[h]
        nrow = n_ref[h:h + 1, :]
        num = (jnp.dot(wi.astype(BF16), vh.astype(BF16), preferred_element_type=F32)
               + wint * lax.dot_general(qh, ct.astype(BF16), (((1,), (1,)), ((), ())),
                                        preferred_element_type=F32))
        nq = jnp.sum(wi, axis=1, keepdims=True) + wint * jnp.sum(q * nrow, axis=1, keepdims=True)
        hh = num / jnp.maximum(jnp.abs(nq), jnp.exp(-mt))
        m_new = mt[L - 1:L, :]
        b_last = bc[L - 1:L, :]
        keep = jnp.exp(b_last + m_prev - m_new)
        wn = jnp.exp(b_last - bc + ic - m_new)
        kw = k * hm * wn
        ct_ref[h] = keep * ct + jnp.dot(vh.T.astype(BF16), kw.astype(BF16), preferred_element_type=F32)
        n_ref[h:h + 1, :] = keep * nrow + jnp.sum(kw, axis=0, keepdims=True)
        m_ref[h:h + 1, :] = jnp.broadcast_to(m_new, (1, LANES))
        mo = mo_ref[:, h * ML_DV:(h + 1) * ML_DV]
        y = _rms(hh) * nrm_ref[:, h * ML_DV:(h + 1) * ML_DV] * _sigmoid(mo)
        o_ref[:, h * ML_DV:(h + 1) * ML_DV] = y.astype(BF16)

    @pl.when(step == pl.num_programs(0) - 1)
    def _():
        ct_out[...] = ct_ref[...]
        n_out[...] = n_ref[...]
        m_out[...] = m_ref[...]


def mlstm_scan(z, smt, brow, bcol, nrm, consts, L):
    t = z.shape[0]
    _, _, tri, trit = consts
    width = ML_HEADS * ML_DK
    vw = ML_HEADS * ML_DV
    row = lambda w, off: pl.BlockSpec((L, w), lambda i, o=off // w: (i, o))
    c2 = lambda a: pl.BlockSpec(a.shape, lambda i: (0, 0))
    st_shapes = [((ML_HEADS, ML_DV, width), lambda i: (0, 0, 0)), ((SUBLANES, width), lambda i: (0, 0)),
                 ((SUBLANES, LANES), lambda i: (0, 0))]
    return pl.pallas_call(
        functools.partial(_mlstm_scan_kernel, L=L),
        out_shape=(jax.ShapeDtypeStruct((t, vw), BF16),) + tuple(
            jax.ShapeDtypeStruct(s, F32) for s, _ in st_shapes),
        grid=(t // L,),
        in_specs=[row(width, Z_MQ), row(width, Z_MK), row(vw, Z_MV), row(vw, Z_MO), row(LANES, Z_SMALL),
                  pl.BlockSpec((SUBLANES, L), lambda i: (SM_MI // SUBLANES, i)),
                  c2(brow), c2(bcol), c2(tri), c2(trit), c2(nrm)],
        out_specs=(pl.BlockSpec((L, vw), lambda i: (i, 0)),) + tuple(pl.BlockSpec(s, f) for s, f in st_shapes),
        scratch_shapes=[pltpu.VMEM(s, F32) for s, _ in st_shapes],
        compiler_params=_cp("arbitrary"),
        name="mlstm_scan",
    )(z, z, z, z, z, smt, brow, bcol, tri, trit, nrm)


def _rec_step_kernel(gq_ref, gk_ref, gv_ref, gg_ref, mq_ref, mk_ref, mv_ref, mo_ref, sm_ref,
                     gqt_ref, gkt_ref, mqt_ref, mkt_ref, smt_ref,
                     wa_ref, ba_ref, wat_ref, bat_ref, brow_ref, gn_ref, mn_ref,
                     s_ref, c_ref, n_ref, m_ref,
                     bo_ref, co_ref, s_out, c_out, n_out, m_out, *, rows):
    blk = pl.program_id(0)
    nb = gqt_ref.shape[1]
    gw = GLA_HEADS * GLA_DK
    mw = ML_HEADS * ML_DK

    pre = sm_ref[...] + brow_ref[...]
    i_v = pre
    f_v = _log_sigmoid(pltpu.roll(pre, LANES - (SM_MF - SM_MI), 1))
    m_prev = m_ref[...]
    mt = jnp.maximum(f_v + m_prev, i_v)
    w_intra = jnp.exp(i_v - mt)
    w_inter = jnp.exp(f_v + m_prev - mt)
    inv_floor = jnp.exp(-mt)
    m_out[...] = mt
    mq = mq_ref[...]
    mk = mk_ref[...] * (ML_DK ** -0.5)
    n_old = n_ref[...]
    gq = gq_ref[...] * (GLA_DK ** -0.5)
    gk = gk_ref[...]
    qk_g, qk_m, qn_m, den = [], [], [], []
    keep_b = jnp.zeros((rows, mw), F32)
    wnew_b = jnp.zeros((rows, mw), F32)
    for h in range(ML_HEADS):
        hm = _head_mask(mw, ML_HEADS, h)
        qk_m.append(jnp.sum(mq * mk * hm, axis=1, keepdims=True))
        qn_m.append(jnp.sum(mq * n_old * hm, axis=1, keepdims=True))
        wi_h = w_intra[:, SM_MI + h:SM_MI + h + 1]
        we_h = w_inter[:, SM_MI + h:SM_MI + h + 1]
        nq = wi_h * qk_m[h] + we_h * qn_m[h]
        den.append(jnp.maximum(jnp.abs(nq), inv_floor[:, SM_MI + h:SM_MI + h + 1]))
        keep_b = keep_b + hm * we_h
        wnew_b = wnew_b + hm * wi_h
    n_out[...] = keep_b * n_old + wnew_b * mk
    for h in range(GLA_HEADS):
        qk_g.append(jnp.sum(gq * gk * _head_mask(gw, GLA_HEADS, h), axis=1, keepdims=True))

    xt = _bdot(wat_ref[...], smt_ref[...]) + bat_ref[...]
    at = jnp.exp(_log_sigmoid(xt) * (1.0 / GLA_GATE_TEMP))
    seq_lane = lax.broadcasted_iota(jnp.int32, (nb, LANES), 0)

    for r in range(rows):
        onehot = (seq_lane == blk * rows + r).astype(BF16)
        a_bc = _dot01_r(at, onehot)
        gk_bc = _dot01_r(gkt_ref[...], onehot)
        gq_bc = _dot01_r(gqt_ref[...], onehot) * (GLA_DK ** -0.5)
        mk_bc = _dot01_r(mkt_ref[...], onehot) * (ML_DK ** -0.5)
        mq_bc = _dot01_r(mqt_ref[...], onehot)
        for h in range(GLA_HEADS):
            sl = slice(h * GLA_DK, (h + 1) * GLA_DK)
            v_row = gv_ref[r:r + 1, h * GLA_DV:(h + 1) * GLA_DV]
            s_new = a_bc[sl] * s_ref[r, h] + gk_bc[sl] * v_row
            s_out[r, h] = s_new
            o = jnp.sum(gq_bc[sl] * s_new, axis=0, keepdims=True)
            gg = gg_ref[r:r + 1, h * GLA_DV:(h + 1) * GLA_DV]
            y = _rms(o) * gn_ref[:, h * GLA_DV:(h + 1) * GLA_DV] * (gg * _sigmoid(gg))
            bo_ref[r:r + 1, h * GLA_DV:(h + 1) * GLA_DV] = y.astype(BF16)
        for h in range(ML_HEADS):
            sl = slice(h * ML_DK, (h + 1) * ML_DK)
            v_row = mv_ref[r:r + 1, h * ML_DV:(h + 1) * ML_DV]
            c_old = c_ref[r, h]
            wi_s = w_intra[r:r + 1, SM_MI + h:SM_MI + h + 1]
            we_s = w_inter[r:r + 1, SM_MI + h:SM_MI + h + 1]
            num = (wi_s * qk_m[h][r:r + 1, :]) * v_row + we_s * jnp.sum(mq_bc[sl] * c_old, axis=0, keepdims=True)
            hh = num / den[h][r:r + 1, :]
            c_out[r, h] = we_s * c_old + wi_s * (mk_bc[sl] * v_row)
            mo = mo_ref[r:r + 1, h * ML_DV:(h + 1) * ML_DV]
            y = _rms(hh) * mn_ref[:, h * ML_DV:(h + 1) * ML_DV] * _sigmoid(mo)
            co_ref[r:r + 1, h * ML_DV:(h + 1) * ML_DV] = y.astype(BF16)


def rec_step(z, zt, wa, ba, wat, bat, brow, gn, mn, s0, c0, layer, n0, m0):
    nb = z.shape[0]
    rows = SUBLANES
    gw = GLA_HEADS * GLA_DK
    mw = ML_HEADS * ML_DK
    gvw = GLA_HEADS * GLA_DV
    mvw = ML_HEADS * ML_DV
    row = lambda w, off: pl.BlockSpec((rows, w), lambda i, o=off // w: (i, o))
    colb = lambda w, off: pl.BlockSpec((w, nb), lambda i, o=off // w: (o, 0))
    c2 = lambda a: pl.BlockSpec(a.shape, lambda i: (0, 0))
    st4 = lambda hds, dk, dv: pl.BlockSpec((rows, hds, dk, dv), lambda i: (i, 0, 0, 0))
    st5 = lambda hds, dk, dv: pl.BlockSpec((None, rows, hds, dk, dv), lambda i: (layer, i, 0, 0, 0))
    return pl.pallas_call(
        functools.partial(_rec_step_kernel, rows=rows),
        out_shape=(jax.ShapeDtypeStruct((nb, gvw), BF16), jax.ShapeDtypeStruct((nb, mvw), BF16),
                   jax.ShapeDtypeStruct(s0.shape[1:], F32), jax.ShapeDtypeStruct(c0.shape[1:], F32),
                   jax.ShapeDtypeStruct((nb, mw), F32), jax.ShapeDtypeStruct((nb, LANES), F32)),
        grid=(nb // rows,),
        in_specs=[row(gw, Z_GQ), row(gw, Z_GK), row(gvw, Z_GV), row(gvw, Z_GG),
                  row(mw, Z_MQ), row(mw, Z_MK), row(mvw, Z_MV), row(mvw, Z_MO), row(LANES, Z_SMALL),
                  colb(gw, Z_GQ), colb(gw, Z_GK), colb(mw, Z_MQ), colb(mw, Z_MK), colb(LANES, Z_SMALL),
                  c2(wa), c2(ba), c2(wat), c2(bat), c2(brow), c2(gn), c2(mn),
                  st5(GLA_HEADS, GLA_DK, GLA_DV), st5(ML_HEADS, ML_DK, ML_DV),
                  pl.BlockSpec((rows, mw), lambda i: (i, 0)), pl.BlockSpec((rows, LANES), lambda i: (i, 0))],
        out_specs=(pl.BlockSpec((rows, gvw), lambda i: (i, 0)), pl.BlockSpec((rows, mvw), lambda i: (i, 0)),
                   st4(GLA_HEADS, GLA_DK, GLA_DV), st4(ML_HEADS, ML_DK, ML_DV),
                   pl.BlockSpec((rows, mw), lambda i: (i, 0)), pl.BlockSpec((rows, LANES), lambda i: (i, 0))),
        compiler_params=_cp("arbitrary"),
        name="rec_step",
    )(z, z, z, z, z, z, z, z, z, zt, zt, zt, zt, zt, wa, ba, wat, bat, brow, gn, mn, s0, c0, n0, m0)


def _mla_prep_s_kernel(qc_ref, kvc_ref, sm_ref, gq_ref, gkv_ref, wq_ref, wukt_ref,
                       cq_ref, sqa_ref, sqb_ref, ck_ref, ska_ref, skb_ref,
                       ql_out, qp_out, lat_out, kr_out):
    qn = (_rms(qc_ref[...]) * gq_ref[...]).astype(BF16)
    lat_out[...] = _rms(kvc_ref[...]) * gkv_ref[...]
    krf = _rope_lanes(sm_ref[...], ck_ref[...], ska_ref[...], skb_ref[...])
    kr_out[...] = krf[:, :MLA_ROPE]
    cq, sqa, sqb = cq_ref[...], sqa_ref[...], sqb_ref[...]
    for h in range(MLA_HEADS):
        qh = _rope_lanes(jnp.dot(qn, wq_ref[h], preferred_element_type=F32), cq, sqa, sqb)
        ql_out[h] = _bdot(qh, wukt_ref[h])
        qp_out[h] = pltpu.roll(qh, LANES - MLA_NOPE, 1)


def mla_prep_s(z, gq, gkv, wq, wukt, tabs):
    nb = z.shape[0]
    hh = MLA_HEADS
    row = lambda w, off: pl.BlockSpec((nb, w), lambda i, o=off // w: (0, o))
    c2 = lambda a: pl.BlockSpec(a.shape, lambda i: (0, 0))
    c3 = lambda a: pl.BlockSpec(a.shape, lambda i: (0, 0, 0))
    return pl.pallas_call(
        _mla_prep_s_kernel,
        out_shape=(jax.ShapeDtypeStruct((hh, nb, LANES), F32), jax.ShapeDtypeStruct((hh, nb, LANES), F32),
                   jax.ShapeDtypeStruct((nb, KV_LORA), F32), jax.ShapeDtypeStruct((nb, MLA_ROPE), F32)),
        grid=(1,),
        in_specs=[row(Q_LORA, Z_QC), row(KV_LORA, Z_KVC), row(LANES, Z_SMALL),
                  c2(gq), c2(gkv), c3(wq), c3(wukt)] + [c2(t) for t in tabs],
        out_specs=(pl.BlockSpec((hh, nb, LANES), lambda i: (0, 0, 0)),
                   pl.BlockSpec((hh, nb, LANES), lambda i: (0, 0, 0)),
                   pl.BlockSpec((nb, KV_LORA), lambda i: (0, 0)), pl.BlockSpec((nb, MLA_ROPE), lambda i: (0, 0))),
        compiler_params=_cp("arbitrary"),
        name="mla_prep_sample",
    )(z, z, z, gq, gkv, wq, wukt, *tabs)


def _mla_decode_kernel(pt_ref, ql_ref, qp_ref, ln_ref, kn_ref, lat_hbm, kr_hbm, o_ref, lat_buf, kr_buf, sem,
                       *, layer, n_pages):
    b = pl.program_id(0)
    slot = lax.rem(b, 2)

    def page_copies(seq, sl, p):
        page = pt_ref[seq, p]
        off = pl.multiple_of(p * PAGE_SIZE, PAGE_SIZE)
        return (pltpu.make_async_copy(lat_hbm.at[layer, page], lat_buf.at[sl, pl.ds(off, PAGE_SIZE), :],
                                      sem.at[0, sl]),
                pltpu.make_async_copy(kr_hbm.at[layer, page], kr_buf.at[sl, :, pl.ds(off, PAGE_SIZE)],
                                      sem.at[1, sl]))

    def start_all(seq, sl):
        def body(p, carry):
            for c in page_copies(seq, sl, p):
                c.start()
            return carry
        lax.fori_loop(0, n_pages, body, 0, unroll=4)

    def wait_all(seq, sl):
        def body(p, carry):
            for c in page_copies(seq, sl, p):
                c.wait()
            return carry
        lax.fori_loop(0, n_pages, body, 0, unroll=4)

    @pl.when(b == 0)
    def _():
        start_all(0, 0)

    @pl.when(b + 1 < pl.num_programs(0))
    def _():
        start_all(b + 1, 1 - slot)

    wait_all(b, slot)

    ql = ql_ref[0]
    qp = qp_ref[0][:, :MLA_ROPE]
    lat = lat_buf[slot]
    latb = lat.astype(BF16)
    s = (lax.dot_general(ql.astype(BF16), latb, (((1,), (1,)), ((), ())), preferred_element_type=F32)
         + jnp.dot(qp.astype(BF16), kr_buf[slot].astype(BF16), preferred_element_type=F32)) * MLA_SCALE
    lat_new = ln_ref[0]
    s_new = (jnp.sum(ql * lat_new, axis=1, keepdims=True)
             + jnp.sum(qp * kn_ref[0], axis=1, keepdims=True)) * MLA_SCALE
    m = jnp.maximum(jnp.max(s, axis=1, keepdims=True), s_new)
    p = jnp.exp(s - m)
    pn = jnp.exp(s_new - m)
    num = jnp.dot(p.astype(BF16), latb, preferred_element_type=F32) + pn * lat_new
    o_ref[0] = num / (jnp.sum(p, axis=1, keepdims=True) + pn)


def mla_decode(page_table, ql, qp, lat_new, kr_new, cache_lat, cache_rope_t, layer):
    nb, n_pages = page_table.shape
    past = n_pages * PAGE_SIZE
    hh = MLA_HEADS
    seq = lambda w: pl.BlockSpec((1, hh, w), lambda b, pt: (b, 0, 0))
    new = lambda w: pl.BlockSpec((1, 1, w), lambda b, pt: (b, 0, 0))
    return pl.pallas_call(
        functools.partial(_mla_decode_kernel, layer=layer, n_pages=n_pages),
        out_shape=jax.ShapeDtypeStruct((nb, hh, KV_LORA), F32),
        grid_spec=pltpu.PrefetchScalarGridSpec(
            num_scalar_prefetch=1,
            grid=(nb,),
            in_specs=[seq(LANES), seq(LANES), new(KV_LORA), new(MLA_ROPE),
                      pl.BlockSpec(memory_space=pl.ANY), pl.BlockSpec(memory_space=pl.ANY)],
            out_specs=pl.BlockSpec((1, hh, KV_LORA), lambda b, pt: (b, 0, 0)),
            scratch_shapes=[pltpu.VMEM((2, past, KV_LORA), F32), pltpu.VMEM((2, MLA_ROPE, past), F32),
                            pltpu.SemaphoreType.DMA((2, 2))]),
        compiler_params=_cp("arbitrary"),
        name="mla_decode",
    )(page_table, ql, qp, lat_new, kr_new, cache_lat, cache_rope_t)


def _uv_proj_kernel(o_ref, w_ref, a_ref):
    a_ref[...] = _bdot(o_ref[...], w_ref[0]).astype(BF16)


def uv_proj(o_lat, wuv):
    nb = o_lat.shape[0]
    return pl.pallas_call(
        _uv_proj_kernel,
        out_shape=jax.ShapeDtypeStruct((nb, MLA_HEADS * LANES), BF16),
        grid=(MLA_HEADS,),
        in_specs=[pl.BlockSpec((nb, KV_LORA), lambda h: (0, h)),
                  pl.BlockSpec((1, KV_LORA, LANES), lambda h: (h, 0, 0))],
        out_specs=pl.BlockSpec((nb, LANES), lambda h: (0, h)),
        compiler_params=_cp("arbitrary"),
        name="uv_proj",
    )(o_lat, wuv)


def _merge_kernel(a_ref, b_ref, c_ref, ga_ref, gb_ref, gc_ref, x_ref, g1_ref, wa_ref, wb_ref, wc_ref, wo_ref,
                  nf_ref, sc_ref, sh_ref, x1_ref, h2_ref):
    merged = (_sigmoid(ga_ref[...]) * jnp.dot(a_ref[...], wa_ref[...], preferred_element_type=F32)
              + _sigmoid(gb_ref[...]) * jnp.dot(b_ref[...], wb_ref[...], preferred_element_type=F32)
              + _sigmoid(gc_ref[...]) * jnp.dot(c_ref[...], wc_ref[...], preferred_element_type=F32))
    mix = _bdot(merged, wo_ref[...])
    x1 = x_ref[...] + g1_ref[...] * mix
    x1_ref[...] = x1
    h2 = _rms(x1) * nf_ref[...] * (1.0 + sc_ref[...]) + sh_ref[...]
    h2_ref[...] = h2.astype(BF16)


def merge(a, b, c, z, x, gate1, wa, wb, wc, wo, nf, scale2, shift2, tm):
    t, d = x.shape
    per_tok = gate1.shape[0] != 1
    mod_spec = (pl.BlockSpec((tm, d), lambda i: (i, 0)) if per_tok else pl.BlockSpec((1, d), lambda i: (0, 0)))
    rowb = lambda w, o=0: pl.BlockSpec((tm, w), lambda i, o=o: (i, o))
    c2 = lambda arr: pl.BlockSpec(arr.shape, lambda i: (0, 0))
    return pl.pallas_call(
        _merge_kernel,
        out_shape=(jax.ShapeDtypeStruct((t, d), F32), jax.ShapeDtypeStruct((t, d), BF16)),
        grid=(t // tm,),
        in_specs=[rowb(a.shape[1]), rowb(b.shape[1]), rowb(c.shape[1]),
                  rowb(d, Z_GA // d), rowb(d, Z_GB // d), rowb(d, Z_GC // d), rowb(d), mod_spec,
                  c2(wa), c2(wb), c2(wc), c2(wo), pl.BlockSpec((1, d), lambda i: (0, 0)), mod_spec, mod_spec],
        out_specs=(rowb(d), rowb(d)),
        compiler_params=_cp("arbitrary"),
        name="merge",
    )(a, b, c, z, z, z, x, gate1, wa, wb, wc, wo, nf.reshape(1, d), scale2, shift2)


def _top_rows(x, n):
    rows = []
    for _ in range(n):
        m = jnp.max(x, axis=0, keepdims=True)
        rows.append(m)
        x = jnp.where(x == m, -jnp.inf, x)
    return rows


def _peer_topk_kernel(h2_ref, wq_ref, k1_ref, k2_ref, thr_out, s2_out, l1_out):
    h2 = h2_ref[...]
    k1 = PEER_TOPK + 1
    for h in range(PEER_HEADS):
        qh = jnp.dot(h2, wq_ref[h], preferred_element_type=F32)
        s1 = _bdot_nt(k1_ref[h], qh[:, :PEER_HALF])
        s2 = _bdot_nt(k2_ref[h], qh[:, PEER_HALF:])
        a = _top_rows(s1, k1)
        b = _top_rows(s2, k1)
        n_rows = -(-k1 // SUBLANES) * SUBLANES
        pad_rows = [jnp.full_like(b[0], -jnp.inf)] * (n_rows - k1)
        b_all = jnp.concatenate(b + pad_rows, axis=0)
        a_all = jnp.concatenate(a + pad_rows, axis=0)
        split = 4
        span = lambda r: -(-(k1 // (r + 1)) // SUBLANES) * SUBLANES
        cand = [a[p] + b_all[:span(p)] for p in range(split)]
        cand += [b[q] + a_all[:span(q)] for q in range(k1 // (split + 1))]
        best = _top_rows(jnp.concatenate(cand, axis=0), k1)
        zsum = jnp.zeros_like(best[0])
        for c in best[:PEER_TOPK]:
            zsum = zsum + jnp.exp(c - best[0])
        mid = 0.5 * (best[PEER_TOPK - 1] + best[PEER_TOPK])
        tabs = ((thr_out, ((mid - s1) - b[0]) * LOG2E), (s2_out, (s2 - b[0]) * LOG2E),
                (l1_out, (s1 - a[0]) * LOG2E - jnp.log(zsum) * LOG2E))
        for ref, val in tabs:
            for c in range(val.shape[1] // LANES):
                ref[h, c] = val[:, c * LANES:(c + 1) * LANES]


def peer_topk(h2, wq, k1, k2, tm):
    t, d = h2.shape
    hh = PEER_HEADS
    c3 = lambda a: pl.BlockSpec(a.shape, lambda i: (0, 0, 0))
    assert tm % LANES == 0
    tab = jax.ShapeDtypeStruct((hh, t // LANES, PEER_KEYS, LANES), F32)
    tab_spec = pl.BlockSpec((hh, tm // LANES, PEER_KEYS, LANES), lambda i: (0, i, 0, 0))
    return pl.pallas_call(
        _peer_topk_kernel,
        out_shape=(tab, tab, tab),
        grid=(t // tm,),
        in_specs=[pl.BlockSpec((tm, d), lambda i: (i, 0)), c3(wq), c3(k1), c3(k2)],
        out_specs=(tab_spec, tab_spec, tab_spec),
        compiler_params=_cp("arbitrary"),
        name="peer_topk",
    )(h2, wq, k1, k2)


def _peer_dense_kernel(h2_ref, u_ref, vt_ref, thr_ref, l1_ref, s2_ref, x1_ref, g2_ref,
                       o_ref, acc_ref, act_ref, wg_ref, *, ni):
    e = pl.program_id(1)

    @pl.when(e == 0)
    def _():
        acc_ref[...] = jnp.zeros(acc_ref.shape, F32)

    act_ref[...] = _gelu_tanh(lax.dot_general(u_ref[...], h2_ref[...], (((1,), (1,)), ((), ())),
                                              preferred_element_type=F32))
    tm = act_ref.shape[1]
    for c in range(tm // LANES):
        cs = slice(c * LANES, (c + 1) * LANES)
        for ii in range(ni):
            rs = slice(ii * PEER_KEYS, (ii + 1) * PEER_KEYS)
            w = None
            for h in range(PEER_HEADS):
                s2 = s2_ref[h, c]
                part = jnp.where(s2 >= thr_ref[h, c, ii:ii + 1, :],
                                 jnp.exp2(s2 + l1_ref[h, c, ii:ii + 1, :]), 0.0)
                w = part if w is None else w + part
            wg_ref[rs, cs] = (w * act_ref[rs, cs]).astype(BF16)
    acc_ref[...] += jnp.dot(vt_ref[...], wg_ref[...], preferred_element_type=F32)

    @pl.when(e == pl.num_programs(1) - 1)
    def _():
        o_ref[...] = x1_ref[...] + g2_ref[...] * acc_ref[...].T


def peer_dense(h2, u, vt, layer, thr, l1, s2, x1, gate2, tm, ni):
    t, d = h2.shape
    n_exp = u.shape[1]
    te = ni * PEER_KEYS
    hh = PEER_HEADS
    per_tok = gate2.shape[0] != 1
    mod_spec = (pl.BlockSpec((tm, d), lambda i, e: (i, 0)) if per_tok
                else pl.BlockSpec((1, d), lambda i, e: (0, 0)))
    nc = tm // LANES
    sub = pl.BlockSpec((hh, nc, ni, LANES), lambda i, e: (0, i, e, 0))
    full = pl.BlockSpec((hh, nc, PEER_KEYS, LANES), lambda i, e: (0, i, 0, 0))
    return pl.pallas_call(
        functools.partial(_peer_dense_kernel, ni=ni),
        out_shape=jax.ShapeDtypeStruct((t, d), F32),
        grid=(t // tm, n_exp // te),
        in_specs=[pl.BlockSpec((tm, d), lambda i, e: (i, 0)),
                  pl.BlockSpec((None, te, d), lambda i, e: (layer, e, 0)),
                  pl.BlockSpec((None, d, te), lambda i, e: (layer, 0, e)),
                  sub, sub, full,
                  pl.BlockSpec((tm, d), lambda i, e: (i, 0)), mod_spec],
        out_specs=pl.BlockSpec((tm, d), lambda i, e: (i, 0)),
        scratch_shapes=[pltpu.VMEM((d, tm), F32), pltpu.VMEM((te, tm), F32), pltpu.VMEM((te, tm), BF16)],
        compiler_params=_cp("arbitrary", "arbitrary"),
        name="peer_dense",
    )(h2, u, vt, thr, l1, s2, x1, gate2)


def _final_norm_kernel(x_ref, g_ref, o_ref):
    o_ref[...] = _rms(x_ref[...]) * g_ref[...]


def final_norm(x, g, tm):
    t, d = x.shape
    return pl.pallas_call(
        _final_norm_kernel,
        out_shape=jax.ShapeDtypeStruct((t, d), F32),
        grid=(t // tm,),
        in_specs=[pl.BlockSpec((tm, d), lambda i: (i, 0)), pl.BlockSpec((1, d), lambda i: (0, 0))],
        out_specs=pl.BlockSpec((tm, d), lambda i: (i, 0)),
        compiler_params=_cp("arbitrary"),
        name="final_norm",
    )(x, g.reshape(1, d))


def _pad_to(a, axis, size):
    pad = [(0, 0)] * a.ndim
    pad[axis] = (0, size - a.shape[axis])
    return jnp.pad(a, pad)


def _prep_w_in(w):
    sizes = (Q_LORA, KV_LORA, MLA_ROPE, 256, 256, 512, GLA_GATE_RANK, 512, 256, 256, 512, ML_HEADS, ML_HEADS, 512,
             D_MODEL, D_MODEL, D_MODEL)
    offs = np.concatenate([[0], np.cumsum(sizes)])
    (q_c, kv_c, k_pe, gq, gk, gv, ga, gg, mq, mk, mv, mi, mf, mo, g_a, g_b, g_c) = [
        w[:, offs[i]:offs[i + 1]] for i in range(len(sizes))]
    small = _pad_to(jnp.concatenate([k_pe, ga, mi, mf], axis=1), 1, LANES)
    out = jnp.concatenate([g_a, g_b, g_c, gv, gg, mv, mo, q_c, gq, gk, mq, mk, kv_c, small], axis=1)
    assert out.shape[1] == Z_WIDTH
    return out.astype(BF16)


def _rope_tables(pos):
    half = MLA_ROPE // 2
    inv = ROPE_BASE ** (-jnp.arange(half, dtype=F32) / half)
    ang = pos.astype(F32)[:, None] * inv[None, :]
    cos, sin = jnp.cos(ang), jnp.sin(ang)
    n = pos.shape[0]
    z = lambda w: jnp.zeros((n, w), F32)
    one = jnp.ones((n, MLA_NOPE), F32)
    rest = LANES - MLA_NOPE - MLA_ROPE
    cq = jnp.concatenate([one, cos, cos, z(rest)], axis=1)
    sqa = jnp.concatenate([z(MLA_NOPE), -sin, z(half), z(rest)], axis=1)
    sqb = jnp.concatenate([z(MLA_NOPE), z(half), sin, z(rest)], axis=1)
    ck = jnp.concatenate([cos, cos, z(LANES - MLA_ROPE)], axis=1)
    ska = jnp.concatenate([-sin, z(LANES - half)], axis=1)
    skb = jnp.concatenate([z(half), sin, z(LANES - MLA_ROPE)], axis=1)
    return (cq, sqa, sqb, ck, ska, skb), (cos.T, sin.T)


def _layer_weights(l, P):
    w = {}
    w['w_in'] = _prep_w_in(P['w_in'][l])
    w['gq'] = P['mla_q_norm'][l].reshape(1, Q_LORA)
    w['gkv'] = P['mla_kv_norm'][l].reshape(1, KV_LORA)
    w['wq'] = _pad_to(jnp.transpose(P['w_uq'][l], (1, 0, 2)), 2, LANES).astype(BF16)
    wuk = jnp.transpose(P['w_uk'][l], (1, 0, 2))
    w['wuk'] = _pad_to(wuk, 2, LANES).astype(BF16)
    w['wukt'] = _pad_to(jnp.transpose(wuk, (0, 2, 1)), 1, LANES).astype(BF16)
    w['wuv'] = _pad_to(jnp.transpose(P['w_uv'][l], (1, 0, 2)), 2, LANES).astype(BF16)
    w['wqt'] = _pad_to(jnp.transpose(P['w_uq'][l], (1, 2, 0)), 1, LANES).astype(BF16)
    w['wuvt'] = _pad_to(jnp.transpose(P['w_uv'][l], (1, 2, 0)), 1, LANES).astype(BF16)
    wa = jnp.zeros((LANES, GLA_HEADS * GLA_DK), F32).at[SM_GA:SM_GA + GLA_GATE_RANK].set(P['w_gla_a'][l])
    w['gla_wa'] = wa.astype(BF16)
    w['gla_wat'] = wa.T.astype(BF16)
    w['gla_ba'] = P['b_gla_a'][l].reshape(1, -1)
    w['gla_bat'] = P['b_gla_a'][l].reshape(-1, 1)
    w['gla_norm'] = P['gla_norm'][l].reshape(1, -1)
    w['ml_norm'] = P['ml_norm'][l].reshape(1, -1)
    brow = jnp.zeros((1, LANES), F32)
    brow = brow.at[0, SM_MI:SM_MI + ML_HEADS].set(P['ml_b_i'][l]).at[0, SM_MF:SM_MF + ML_HEADS].set(P['ml_b_f'][l])
    w['ml_brow'] = brow
    w['ml_bcol'] = jnp.concatenate([P['ml_b_i'][l], P['ml_b_f'][l]]).reshape(2 * ML_HEADS, 1)
    wpa = P['w_proj_a'][l].reshape(MLA_HEADS, MLA_V, D_MODEL)
    w['wpa'] = _pad_to(wpa, 1, LANES).reshape(MLA_HEADS * LANES, D_MODEL).astype(BF16)
    w['wpb'] = P['w_proj_b'][l].astype(BF16)
    w['wpc'] = P['w_proj_c'][l].astype(BF16)
    w['wo'] = P['w_o'][l].astype(BF16)
    w['peer_wq'] = jnp.transpose(P['peer_wq'][l], (1, 0, 2)).astype(BF16)
    w['peer_k1'] = P['peer_keys'][l][:, 0].astype(BF16)
    w['peer_k2'] = P['peer_keys'][l][:, 1].astype(BF16)
    w['layer'] = l
    w['peer_u'] = P['peer_u_b']
    w['peer_vt'] = P['peer_vt_b']
    return w


def _mods(c, w_ada, b_ada, layer):
    rows = c.shape[0]
    cp = _pad_to(c, 0, SUBLANES) if rows < SUBLANES else c
    mod = ada_mod(cp, w_ada, b_ada[layer], layer)[:rows]
    return jnp.split(mod, 6, axis=-1)


def _peer_block(h2, x1, gate2, w, tm, ni):
    thr, s2, l1 = peer_topk(h2, w['peer_wq'], w['peer_k1'], w['peer_k2'], min(tm, 2 * LANES))
    return peer_dense(h2, w['peer_u'], w['peer_vt'], w['layer'], thr, l1, s2, x1, gate2, tm, ni)


def _trunk_prompt(x, c, P, W):
    t = x.shape[0]
    tm = min(512, t)
    L = min(128, t)
    consts = _chunk_constants(L)
    tabs, tabs_t = _rope_tables(jnp.arange(t, dtype=jnp.int32))
    lat_rows, rope_rows, gla_st, c_st, n_st, m_st = [], [], [], [], [], []
    for l in range(len(W)):
        w = W[l]
        shift1, scale1, gate1, shift2, scale2, gate2 = _mods(c, P['w_ada'], P['b_ada'], l)
        z, zst = norm_proj(x, P['norm_mix'][l], scale1, shift1, w['w_in'], min(2 * tm, t))
        qt, k, vt, lat, kr = mla_prep(z, w['gq'], w['gkv'], w['wqt'], w['wuk'], w['wuvt'], tabs_t, tabs, tm)
        a = flash_attention(qt, k, vt, tm)
        b, st = gla_scan(z, w['gla_wa'], w['gla_ba'], w['gla_norm'], consts, L)
        cc, ct, nrow, mrow = mlstm_scan(z, zst, w['ml_brow'], w['ml_bcol'], w['ml_norm'], consts, L)
        x1, h2 = merge(a, b, cc, z, x, gate1, w['wpa'], w['wpb'], w['wpc'], w['wo'],
                       P['norm_ffn'][l], scale2, shift2, tm)
        x = _peer_block(h2, x1, gate2, w, tm, SUBLANES)
        lat_rows.append(lat)
        rope_rows.append(kr)
        gla_st.append(jnp.stack([st[h][:, h * GLA_DK:(h + 1) * GLA_DK].T for h in range(GLA_HEADS)]))
        c_st.append(jnp.stack([ct[h][:, h * ML_DK:(h + 1) * ML_DK].T for h in range(ML_HEADS)]))
        n_st.append(jnp.stack([nrow[h, h * ML_DK:(h + 1) * ML_DK] for h in range(ML_HEADS)]))
        m_st.append(mrow[:ML_HEADS, 0])
    y = final_norm(x, P['norm_final'], tm)
    return y, tuple(jnp.stack(r) for r in (lat_rows, rope_rows, gla_st, c_st, n_st, m_st))


def _trunk_sample(x, c, P, W, cache_lat, cache_rope, page_table, s_gla, s_c, s_n, s_m):
    nb = x.shape[0]
    past = page_table.shape[1] * PAGE_SIZE
    tabs, _ = _rope_tables(jnp.full((1,), past, dtype=jnp.int32))
    cache_rope = jnp.swapaxes(cache_rope, 2, 3)
    lat_rows, rope_rows, gla_st, c_st, n_st, m_st = [], [], [], [], [], []
    for l in range(len(W)):
        w = W[l]
        shift1, scale1, gate1, shift2, scale2, gate2 = _mods(c, P['w_ada'], P['b_ada'], l)
        z, _ = norm_proj(x, P['norm_mix'][l], scale1, shift1, w['w_in'], nb)
        ql, qp, lat, kr = mla_prep_s(z, w['gq'], w['gkv'], w['wq'], w['wukt'], tabs)
        o_lat = mla_decode(page_table, jnp.transpose(ql, (1, 0, 2)), jnp.transpose(qp, (1, 0, 2)),
                           lat[:, None, :], kr[:, None, :], cache_lat, cache_rope, l)
        a = uv_proj(o_lat.reshape(nb, MLA_HEADS * KV_LORA), w['wuv'])
        m_in = jnp.zeros((nb, LANES), F32).at[:, SM_MI:SM_MI + ML_HEADS].set(s_m[l])
        b, cc, s_new, c_new, n_new, m_new = rec_step(
            z, z.T, w['gla_wa'], w['gla_ba'], w['gla_wat'], w['gla_bat'], w['ml_brow'], w['gla_norm'], w['ml_norm'],
            s_gla, s_c, l, s_n[l].reshape(nb, ML_HEADS * ML_DK), m_in)
        x1, h2 = merge(a, b, cc, z, x, gate1, w['wpa'], w['wpb'], w['wpc'], w['wo'],
                       P['norm_ffn'][l], scale2, shift2, nb)
        x = _peer_block(h2, x1, gate2, w, nb, SUBLANES)
        lat_rows.append(lat[:, None, :])
        rope_rows.append(kr[:, None, :])
        gla_st.append(s_new)
        c_st.append(c_new)
        n_st.append(n_new.reshape(nb, ML_HEADS, ML_DK))
        m_st.append(m_new[:, SM_MI:SM_MI + ML_HEADS])
    y = final_norm(x, P['norm_final'], nb)
    return y, tuple(jnp.stack(r) for r in (lat_rows, rope_rows, gla_st, c_st, n_st, m_st))


def kernel(x_prompt, x_sample, cache_kv_latent, cache_k_rope, state_gla, state_mlstm_C, state_mlstm_n, state_mlstm_m, page_table, c_prompt, c_sample, w_ada, b_ada, norm_mix, norm_ffn, norm_final, w_in, mla_q_norm, mla_kv_norm, w_uq, w_uk, w_uv, w_gla_a, b_gla_a, gla_norm, ml_b_i, ml_b_f, ml_norm, w_proj_a, w_proj_b, w_proj_c, w_o, peer_wq, peer_keys, peer_u, peer_v):
    P = {'w_ada': w_ada, 'b_ada': b_ada, 'norm_mix': norm_mix, 'norm_ffn': norm_ffn, 'norm_final': norm_final,
         'w_in': w_in, 'mla_q_norm': mla_q_norm, 'mla_kv_norm': mla_kv_norm, 'w_uq': w_uq, 'w_uk': w_uk,
         'w_uv': w_uv, 'w_gla_a': w_gla_a, 'b_gla_a': b_gla_a, 'gla_norm': gla_norm, 'ml_b_i': ml_b_i,
         'ml_b_f': ml_b_f, 'ml_norm': ml_norm, 'w_proj_a': w_proj_a, 'w_proj_b': w_proj_b,
         'w_proj_c': w_proj_c, 'w_o': w_o, 'peer_wq': peer_wq, 'peer_keys': peer_keys, 'peer_u': peer_u,
         'peer_v': peer_v}
    depth = w_in.shape[0]
    P['peer_u_b'] = peer_u.astype(BF16)
    P['peer_vt_b'] = jnp.swapaxes(peer_v, 1, 2).astype(BF16)
    W = [_layer_weights(l, P) for l in range(depth)]
    bp, sp, d = x_prompt.shape
    assert bp == 1 and x_sample.shape[1] == 1
    nb = x_sample.shape[0]
    y_p, (lat_p, rope_p, gla_p, c_p, n_p, m_p) = _trunk_prompt(x_prompt.reshape(sp, d), c_prompt, P, W)
    y_s, (lat_s, rope_s, gla_s, c_s, n_s, m_s) = _trunk_sample(
        x_sample.reshape(nb, d), c_sample, P, W, cache_kv_latent, cache_k_rope, page_table,
        state_gla, state_mlstm_C, state_mlstm_n, state_mlstm_m)
    return (y_p.reshape(bp, sp, d), y_s.reshape(nb, 1, d),
            lat_p[:, None], rope_p[:, None], gla_p[:, None], c_p[:, None], n_p[:, None], m_p[:, None],
            lat_s, rope_s, gla_s, c_s, n_s, m_s)
```

```python
import functools

import numpy as np
import jax
import jax.numpy as jnp
from jax import lax
from jax.experimental import pallas as pl
from jax.experimental.pallas import tpu as pltpu

F32 = jnp.float32
BF16 = jnp.bfloat16

D_MODEL = 1024
PAGE_SIZE = 128
MLA_HEADS = 8
MLA_NOPE = 64
MLA_ROPE = 32
MLA_V = 64
Q_LORA = 256
KV_LORA = 128
ROPE_BASE = 10000.0
MLA_SCALE = (MLA_NOPE + MLA_ROPE) ** -0.5
GLA_HEADS = 4
GLA_DK = 64
GLA_DV = 128
GLA_GATE_RANK = 16
GLA_GATE_TEMP = 16.0
ML_HEADS = 4
ML_DK = 64
ML_DV = 128
PEER_HEADS = 8
PEER_KEYS = 128
PEER_HALF = 128
PEER_TOPK = 16
NORM_EPS = 1e-6

LANES = 128
SUBLANES = 8
VMEM_LIMIT = 48 * 1024 * 1024

NEG = -1e30
LOG2E = 1.4426950408889634

Z_GA, Z_GB, Z_GC = 0, 1024, 2048
Z_GV, Z_GG, Z_MV, Z_MO = 3072, 3584, 4096, 4608
Z_QC, Z_GQ, Z_GK, Z_MQ, Z_MK = 5120, 5376, 5632, 5888, 6144
Z_KVC, Z_SMALL = 6400, 6528
Z_WIDTH = 6656
SM_KPE, SM_GA, SM_MI, SM_MF = 0, 32, 48, 52


def _cp(*sem):
    return pltpu.CompilerParams(dimension_semantics=tuple(sem), vmem_limit_bytes=VMEM_LIMIT)


def _bdot(a, b):
    return jnp.dot(a.astype(BF16), b.astype(BF16), preferred_element_type=F32)


def _bdot_nt(a, b):
    return lax.dot_general(a.astype(BF16), b.astype(BF16), (((1,), (1,)), ((), ())),
                           preferred_element_type=F32)


def _split3(x):
    hi = x.astype(BF16)
    r = x - hi.astype(F32)
    mid = r.astype(BF16)
    lo = (r - mid.astype(F32)).astype(BF16)
    return hi, mid, lo


def _dot01(m01, x):
    hi, mid, lo = _split3(x)
    return (jnp.dot(m01, hi, preferred_element_type=F32) + jnp.dot(m01, mid, preferred_element_type=F32)
            + jnp.dot(m01, lo, preferred_element_type=F32))


def _dot01_r(x, m01):
    hi, mid, lo = _split3(x)
    return (jnp.dot(hi, m01, preferred_element_type=F32) + jnp.dot(mid, m01, preferred_element_type=F32)
            + jnp.dot(lo, m01, preferred_element_type=F32))


def _sigmoid(x):
    return 1.0 / (1.0 + jnp.exp(-x))


def _log_sigmoid(x):
    return jnp.minimum(x, 0.0) - jnp.log(1.0 + jnp.exp(-jnp.abs(x)))


def _rms(x):
    return x * lax.rsqrt(jnp.mean(x * x, axis=-1, keepdims=True) + NORM_EPS)


def _gelu_tanh(x):
    return 0.5 * x * (1.0 + jnp.tanh(0.7978845608028654 * (x + 0.044715 * (x * x * x))))


def _ada_kernel(c_ref, w_ref, b_ref, o_ref):
    c = c_ref[...]
    o_ref[...] = _bdot(c * _sigmoid(c), w_ref[...]) + b_ref[...]


def ada_mod(c, w, b, layer):
    rows, d = c.shape
    n = w.shape[2]
    tn = 1536
    return pl.pallas_call(
        _ada_kernel,
        out_shape=jax.ShapeDtypeStruct((rows, n), F32),
        grid=(n // tn,),
        in_specs=[pl.BlockSpec((rows, d), lambda j: (0, 0)),
                  pl.BlockSpec((None, d, tn), lambda j: (layer, 0, j)),
                  pl.BlockSpec((1, tn), lambda j: (0, j))],
        out_specs=pl.BlockSpec((rows, tn), lambda j: (0, j)),
        compiler_params=_cp("arbitrary"),
        name="ada_mod",
    )(c, w, b.reshape(1, n))


def _norm_proj_kernel(x_ref, g_ref, sc_ref, sh_ref, w_ref, o_ref, st_ref, hn_ref, *, tn):
    @pl.when(pl.program_id(1) == 0)
    def _():
        h = _rms(x_ref[...]) * g_ref[...] * (1.0 + sc_ref[...]) + sh_ref[...]
        hn_ref[...] = h.astype(BF16)

    res = jnp.dot(hn_ref[...], w_ref[...], preferred_element_type=F32)
    o_ref[...] = res

    @pl.when(pl.program_id(1) == Z_SMALL // tn)
    def _():
        off = Z_SMALL % tn
        st_ref[...] = res[:, off:off + LANES].T


def norm_proj(x, g, scale, shift, w, tm):
    t, d = x.shape
    n = w.shape[1]
    tn = 512
    per_tok = scale.shape[0] != 1
    mod_spec = (pl.BlockSpec((tm, d), lambda i, j: (i, 0)) if per_tok
                else pl.BlockSpec((1, d), lambda i, j: (0, 0)))
    return pl.pallas_call(
        functools.partial(_norm_proj_kernel, tn=tn),
        out_shape=(jax.ShapeDtypeStruct((t, n), F32), jax.ShapeDtypeStruct((LANES, t), F32)),
        grid=(t // tm, n // tn),
        in_specs=[pl.BlockSpec((tm, d), lambda i, j: (i, 0)),
                  pl.BlockSpec((1, d), lambda i, j: (0, 0)),
                  mod_spec, mod_spec,
                  pl.BlockSpec((d, tn), lambda i, j: (0, j))],
        out_specs=(pl.BlockSpec((tm, tn), lambda i, j: (i, j)), pl.BlockSpec((LANES, tm), lambda i, j: (0, i))),
        scratch_shapes=[pltpu.VMEM((tm, d), BF16)],
        compiler_params=_cp("arbitrary", "arbitrary"),
        name="norm_proj",
    )(x, g.reshape(1, d), scale, shift, w)


def _rope_lanes(x, c, sa, sb):
    return x * c + pltpu.roll(x, LANES - 16, 1) * sa + pltpu.roll(x, 16, 1) * sb


def _mla_prep_kernel(qc_ref, kvc_ref, sm_ref, gq_ref, gkv_ref, wqt_ref, wuk_ref, wuvt_ref,
                     ct_ref, st_ref, ck_ref, ska_ref, skb_ref,
                     qt_out, k_out, vt_out, lat_out, kr_out):
    half = MLA_ROPE // 2
    qn = (_rms(qc_ref[...]) * gq_ref[...]).astype(BF16)
    lat = _rms(kvc_ref[...]) * gkv_ref[...]
    lat_out[...] = lat
    latb = lat.astype(BF16)
    krf = _rope_lanes(sm_ref[...], ck_ref[...], ska_ref[...], skb_ref[...])
    kr_out[...] = krf[:, :MLA_ROPE]
    kr_sh = pltpu.roll(krf, MLA_NOPE, 1)
    cos_t, sin_t = ct_ref[...], st_ref[...]
    tm = qn.shape[0]
    ones_row = (lax.broadcasted_iota(jnp.int32, (LANES, tm), 0) == MLA_V).astype(F32)
    pad = jnp.zeros((LANES - MLA_NOPE - MLA_ROPE, tm), F32)
    for h in range(MLA_HEADS):
        qt = lax.dot_general(wqt_ref[h], qn, (((1,), (1,)), ((), ())), preferred_element_type=F32)
        x1 = qt[MLA_NOPE:MLA_NOPE + half]
        x2 = qt[MLA_NOPE + half:MLA_NOPE + MLA_ROPE]
        qt = jnp.concatenate([qt[:MLA_NOPE], x1 * cos_t - x2 * sin_t, x1 * sin_t + x2 * cos_t, pad], axis=0)
        qt_out[h] = (qt * (MLA_SCALE * LOG2E)).astype(BF16)
        k_out[h] = (jnp.dot(latb, wuk_ref[h], preferred_element_type=F32) + kr_sh).astype(BF16)
        vt = lax.dot_general(wuvt_ref[h], latb, (((1,), (1,)), ((), ())), preferred_element_type=F32)
        vt_out[h] = (vt + ones_row).astype(BF16)


def mla_prep(z, gq, gkv, wqt, wuk, wuvt, tabs_t, tabs, tm):
    t = z.shape[0]
    hh = MLA_HEADS
    row = lambda w, off: pl.BlockSpec((tm, w), lambda i, o=off // w: (i, o))
    const2 = lambda a: pl.BlockSpec(a.shape, lambda i: (0, 0))
    const3 = lambda a: pl.BlockSpec(a.shape, lambda i: (0, 0, 0))
    tab_spec = pl.BlockSpec((tm, LANES), lambda i: (i, 0))
    tabt_spec = pl.BlockSpec((MLA_ROPE // 2, tm), lambda i: (0, i))
    return pl.pallas_call(
        _mla_prep_kernel,
        out_shape=(jax.ShapeDtypeStruct((hh, LANES, t), BF16),
                   jax.ShapeDtypeStruct((hh, t, LANES), BF16),
                   jax.ShapeDtypeStruct((hh, LANES, t), BF16),
                   jax.ShapeDtypeStruct((t, KV_LORA), F32),
                   jax.ShapeDtypeStruct((t, MLA_ROPE), F32)),
        grid=(t // tm,),
        in_specs=[row(Q_LORA, Z_QC), row(KV_LORA, Z_KVC), row(LANES, Z_SMALL),
                  const2(gq), const2(gkv), const3(wqt), const3(wuk), const3(wuvt),
                  tabt_spec, tabt_spec, tab_spec, tab_spec, tab_spec],
        out_specs=(pl.BlockSpec((hh, LANES, tm), lambda i: (0, 0, i)),
                   pl.BlockSpec((hh, tm, LANES), lambda i: (0, i, 0)),
                   pl.BlockSpec((hh, LANES, tm), lambda i: (0, 0, i)),
                   pl.BlockSpec((tm, KV_LORA), lambda i: (i, 0)),
                   pl.BlockSpec((tm, MLA_ROPE), lambda i: (i, 0))),
        compiler_params=_cp("arbitrary"),
        name="mla_prep",
    )(z, z, z, gq, gkv, wqt, wuk, wuvt, *tabs_t, *tabs[3:])


def _flash_kernel(qi_ref, ki_ref, qt_ref, k_ref, vt_ref, o_ref, m_ref, acc_ref, *, tq):
    qi = qi_ref[pl.program_id(0)]
    ki = ki_ref[pl.program_id(0)]

    @pl.when(ki == 0)
    def _():
        m_ref[...] = jnp.full(m_ref.shape, NEG, F32)
        acc_ref[...] = jnp.zeros(acc_ref.shape, F32)

    def scores(h):
        return jnp.dot(k_ref[h], qt_ref[h], preferred_element_type=F32)

    def accumulate(h, p, alpha):
        acc_ref[h] = alpha * acc_ref[h] + jnp.dot(vt_ref[h], p, preferred_element_type=F32)

    def step(diag):
        if diag:
            key = lax.broadcasted_iota(jnp.int32, (tq, tq), 0)
            qry = lax.broadcasted_iota(jnp.int32, (tq, tq), 1)
            keep = key <= qry
        s_next = scores(0)
        pending = None
        for h in range(MLA_HEADS):
            s = s_next
            if h + 1 < MLA_HEADS:
                s_next = scores(h + 1)
            if pending is not None:
                accumulate(*pending)
            if diag:
                s = jnp.where(keep, s, NEG)
            m_prev = m_ref[h:h + 1, :]
            m_new = jnp.maximum(m_prev, jnp.max(s, axis=0, keepdims=True))
            p = jnp.exp2(s - m_new).astype(BF16)
            m_ref[h:h + 1, :] = m_new
            pending = (h, p, jnp.exp2(m_prev - m_new))
        accumulate(*pending)

    @pl.when(ki < qi)
    def _():
        step(False)

    @pl.when(ki == qi)
    def _():
        step(True)
        for h in range(MLA_HEADS):
            acc = acc_ref[h]
            o_ref[:, h * LANES:(h + 1) * LANES] = (acc / acc[MLA_V:MLA_V + 1, :]).T.astype(BF16)


def flash_attention(qt, k, vt, tq):
    hh, t, _ = k.shape
    n = t // tq
    pairs = np.array([(i, j) for i in range(n) for j in range(i + 1)], np.int32)
    return pl.pallas_call(
        functools.partial(_flash_kernel, tq=tq),
        out_shape=jax.ShapeDtypeStruct((t, hh * LANES), BF16),
        grid_spec=pltpu.PrefetchScalarGridSpec(
            num_scalar_prefetch=2,
            grid=(pairs.shape[0],),
            in_specs=[pl.BlockSpec((hh, LANES, tq), lambda s, qa, ka: (0, 0, qa[s])),
                      pl.BlockSpec((hh, tq, LANES), lambda s, qa, ka: (0, ka[s], 0)),
                      pl.BlockSpec((hh, LANES, tq), lambda s, qa, ka: (0, 0, ka[s]))],
            out_specs=pl.BlockSpec((tq, hh * LANES), lambda s, qa, ka: (qa[s], 0)),
            scratch_shapes=[pltpu.VMEM((hh, tq), F32), pltpu.VMEM((hh, LANES, tq), F32)]),
        compiler_params=_cp("arbitrary"),
        name="flash_attention",
    )(jnp.asarray(pairs[:, 0]), jnp.asarray(pairs[:, 1]), qt, k, vt)


def _chunk_constants(L):
    nl = int(np.log2(L))
    assert 2 ** nl == L
    idx = np.arange(L)
    tri = (idx[None, :] <= idx[:, None]).astype(np.float32)
    mq, mk, pm = [], [], []
    for lev in range(nl):
        p = 1 << lev
        start_right = (idx >> (lev + 1) << (lev + 1)) + p
        right = ((idx >> lev) & 1) == 1
        u = idx[None, :]
        mq.append((right[:, None] & (u >= start_right[:, None]) & (u <= idx[:, None])).astype(np.float32))
        mk.append(((~right)[:, None] & (u > idx[:, None]) & (u < start_right[:, None])).astype(np.float32))
        same = (idx[:, None] >> (lev + 1)) == (idx[None, :] >> (lev + 1))
        pm.append((same & right[:, None] & (~right)[None, :]).astype(np.float32))
    pm.append(np.eye(L, dtype=np.float32))
    sums = np.concatenate([tri] + [a + b for a, b in zip(mq, mk)], axis=0)
    return jnp.asarray(sums, BF16), jnp.asarray(np.stack(pm), F32), jnp.asarray(tri, BF16), jnp.asarray(tri.T, BF16)


def _head_mask(width, heads, h):
    lane = lax.broadcasted_iota(jnp.int32, (1, width), 1)
    per = width // heads
    return ((lane >= h * per) & (lane < (h + 1) * per)).astype(F32)


def _gla_scan_kernel(gq_ref, gk_ref, gv_ref, gg_ref, sm_ref, wa_ref, ba_ref, sums_ref, pm_ref, gn_ref,
                     o_ref, st_out, st_ref, *, L, nl):
    step = pl.program_id(0)

    @pl.when(step == 0)
    def _():
        st_ref[...] = jnp.zeros(st_ref.shape, F32)

    width = GLA_HEADS * GLA_DK
    x = _bdot(sm_ref[...], wa_ref[...]) + ba_ref[...]
    la = _log_sigmoid(x) * (1.0 / GLA_GATE_TEMP)
    dsum = _dot01(sums_ref[...], la)
    b = dsum[0:L]
    b_last = b[L - 1:L, :]
    q = gq_ref[...] * (GLA_DK ** -0.5)
    k = gk_ref[...]
    q_in = (q * jnp.exp(b)).astype(BF16)
    k_out = k * jnp.exp(b_last - b)
    decay_all = jnp.exp(b_last)
    hms = [_head_mask(width, GLA_HEADS, h) for h in range(GLA_HEADS)]
    atts = [jnp.zeros((L, L), F32) for _ in range(GLA_HEADS)]
    for lev in range(nl + 1):
        fac = jnp.exp(dsum[(1 + lev) * L:(2 + lev) * L]) if lev < nl else None
        ql = q if fac is None else q * fac
        kl = (k if fac is None else k * fac).astype(BF16)
        stacked = jnp.concatenate([ql * hm for hm in hms], axis=0).astype(BF16)
        sc = lax.dot_general(stacked, kl, (((1,), (1,)), ((), ())), preferred_element_type=F32)
        keep = pm_ref[lev] > 0.0
        for h in range(GLA_HEADS):
            atts[h] = atts[h] + jnp.where(keep, sc[h * L:(h + 1) * L], 0.0)
    for h in range(GLA_HEADS):
        hm = hms[h]
        att = atts[h]
        vh = gv_ref[:, h * GLA_DV:(h + 1) * GLA_DV]
        st = st_ref[h]
        o = (jnp.dot(att.astype(BF16), vh.astype(BF16), preferred_element_type=F32)
             + lax.dot_general(q_in, st.astype(BF16), (((1,), (1,)), ((), ())), preferred_element_type=F32))
        st_ref[h] = st * decay_all + jnp.dot(vh.T.astype(BF16), (k_out * hm).astype(BF16),
                                             preferred_element_type=F32)
        gg = gg_ref[:, h * GLA_DV:(h + 1) * GLA_DV]
        y = _rms(o) * gn_ref[:, h * GLA_DV:(h + 1) * GLA_DV] * (gg * _sigmoid(gg))
        o_ref[:, h * GLA_DV:(h + 1) * GLA_DV] = y.astype(BF16)

    @pl.when(step == pl.num_programs(0) - 1)
    def _():
        st_out[...] = st_ref[...]


def gla_scan(z, wa, ba, gn, consts, L):
    t = z.shape[0]
    sums, pm, _, _ = consts
    nl = pm.shape[0] - 1
    width = GLA_HEADS * GLA_DK
    vw = GLA_HEADS * GLA_DV
    row = lambda w, off: pl.BlockSpec((L, w), lambda i, o=off // w: (i, o))
    c2 = lambda a: pl.BlockSpec(a.shape, lambda i: (0, 0))
    c3 = lambda a: pl.BlockSpec(a.shape, lambda i: (0, 0, 0))
    return pl.pallas_call(
        functools.partial(_gla_scan_kernel, L=L, nl=nl),
        out_shape=(jax.ShapeDtypeStruct((t, vw), BF16),
                   jax.ShapeDtypeStruct((GLA_HEADS, GLA_DV, width), F32)),
        grid=(t // L,),
        in_specs=[row(width, Z_GQ), row(width, Z_GK), row(vw, Z_GV), row(vw, Z_GG), row(LANES, Z_SMALL),
                  c2(wa), c2(ba), c2(sums), c3(pm), c2(gn)],
        out_specs=(pl.BlockSpec((L, vw), lambda i: (i, 0)),
                   pl.BlockSpec((GLA_HEADS, GLA_DV, width), lambda i: (0, 0, 0))),
        scratch_shapes=[pltpu.VMEM((GLA_HEADS, GLA_DV, width), F32)],
        compiler_params=_cp("arbitrary"),
        name="gla_scan",
    )(z, z, z, z, z, wa, ba, sums, pm, gn)


def _mlstm_scan_kernel(mq_ref, mk_ref, mv_ref, mo_ref, sm_ref, smt_ref, brow_ref, bcol_ref, tri_ref, trit_ref,
                       nrm_ref, o_ref, ct_out, n_out, m_out, ct_ref, n_ref, m_ref, *, L):
    step = pl.program_id(0)

    @pl.when(step == 0)
    def _():
        ct_ref[...] = jnp.zeros(ct_ref.shape, F32)
        n_ref[...] = jnp.zeros(n_ref.shape, F32)
        m_ref[...] = jnp.zeros(m_ref.shape, F32)

    width = ML_HEADS * ML_DK
    pre_c = sm_ref[...] + brow_ref[...]
    pre_r = smt_ref[...] + bcol_ref[...]
    b_c_all = _dot01(tri_ref[...], _log_sigmoid(pre_c))
    b_r_all = _dot01_r(_log_sigmoid(pre_r), trit_ref[...])
    q = mq_ref[...]
    k = mk_ref[...] * (ML_DK ** -0.5)
    kb = k.astype(BF16)
    row = lax.broadcasted_iota(jnp.int32, (L, L), 0)
    col = lax.broadcasted_iota(jnp.int32, (L, L), 1)
    causal = col <= row
    for h in range(ML_HEADS):
        hm = _head_mask(width, ML_HEADS, h)
        bc = b_c_all[:, SM_MF + h:SM_MF + h + 1]
        br = b_r_all[ML_HEADS + h:ML_HEADS + h + 1, :]
        ic = pre_c[:, SM_MI + h:SM_MI + h + 1]
        ir = pre_r[h:h + 1, :]
        m_prev = m_ref[h:h + 1, 0:1]
        dlog = jnp.where(causal, bc - br + ir, NEG)
        inter = bc + m_prev
        mt = jnp.maximum(inter, jnp.max(dlog, axis=1, keepdims=True))
        qh = (q * hm).astype(BF16)
        qk = lax.dot_general(qh, kb, (((1,), (1,)), ((), ())), preferred_element_type=F32)
        wi = jnp.exp(dlog - mt) * qk
        wint = jnp.exp(inter - mt)
        vh = mv_ref[:, h * ML_DV:(h + 1) * ML_DV]
        ct = ct_ref[h]
        nrow = n_ref[h:h + 1, :]
        num = (jnp.dot(wi.astype(BF16), vh.astype(BF16), preferred_element_type=F32)
               + wint * lax.dot_general(qh, ct.astype(BF16), (((1,), (1,)), ((), ())),
                                        preferred_element_type=F32))
        nq = jnp.sum(wi, axis=1, keepdims=True) + wint * jnp.sum(q * nrow, axis=1, keepdims=True)
        hh = num / jnp.maximum(jnp.abs(nq), jnp.exp(-mt))
        m_new = mt[L - 1:L, :]
        b_last = bc[L - 1:L, :]
        keep = jnp.exp(b_last + m_prev - m_new)
        wn = jnp.exp(b_last - bc + ic - m_new)
        kw = k * hm * wn
        ct_ref[h] = keep * ct + jnp.dot(vh.T.astype(BF16), kw.astype(BF16), preferred_element_type=F32)
        n_ref[h:h + 1, :] = keep * nrow + jnp.sum(kw, axis=0, keepdims=True)
        m_ref[h:h + 1, :] = jnp.broadcast_to(m_new, (1, LANES))
        mo = mo_ref[:, h * ML_DV:(h + 1) * ML_DV]
        y = _rms(hh) * nrm_ref[:, h * ML_DV:(h + 1) * ML_DV] * _sigmoid(mo)
        o_ref[:, h * ML_DV:(h + 1) * ML_DV] = y.astype(BF16)

    @pl.when(step == pl.num_programs(0) - 1)
    def _():
        ct_out[...] = ct_ref[...]
        n_out[...] = n_ref[...]
        m_out[...] = m_ref[...]


def mlstm_scan(z, smt, brow, bcol, nrm, consts, L):
    t = z.shape[0]
    _, _, tri, trit = consts
    width = ML_HEADS * ML_DK
    vw = ML_HEADS * ML_DV
    row = lambda w, off: pl.BlockSpec((L, w), lambda i, o=off // w: (i, o))
    c2 = lambda a: pl.BlockSpec(a.shape, lambda i: (0, 0))
    st_shapes = [((ML_HEADS, ML_DV, width), lambda i: (0, 0, 0)), ((SUBLANES, width), lambda i: (0, 0)),
                 ((SUBLANES, LANES), lambda i: (0, 0))]
    return pl.pallas_call(
        functools.partial(_mlstm_scan_kernel, L=L),
        out_shape=(jax.ShapeDtypeStruct((t, vw), BF16),) + tuple(
            jax.ShapeDtypeStruct(s, F32) for s, _ in st_shapes),
        grid=(t // L,),
        in_specs=[row(width, Z_MQ), row(width, Z_MK), row(vw, Z_MV), row(vw, Z_MO), row(LANES, Z_SMALL),
                  pl.BlockSpec((SUBLANES, L), lambda i: (SM_MI // SUBLANES, i)),
                  c2(brow), c2(bcol), c2(tri), c2(trit), c2(nrm)],
        out_specs=(pl.BlockSpec((L, vw), lambda i: (i, 0)),) + tuple(pl.BlockSpec(s, f) for s, f in st_shapes),
        scratch_shapes=[pltpu.VMEM(s, F32) for s, _ in st_shapes],
        compiler_params=_cp("arbitrary"),
        name="mlstm_scan",
    )(z, z, z, z, z, smt, brow, bcol, tri, trit, nrm)


def _rec_step_kernel(gq_ref, gk_ref, gv_ref, gg_ref, mq_ref, mk_ref, mv_ref, mo_ref, sm_ref,
                     gqt_ref, gkt_ref, mqt_ref, mkt_ref, smt_ref,
                     wa_ref, ba_ref, wat_ref, bat_ref, brow_ref, gn_ref, mn_ref,
                     s_ref, c_ref, n_ref, m_ref,
                     bo_ref, co_ref, s_out, c_out, n_out, m_out, *, rows):
    blk = pl.program_id(0)
    nb = gqt_ref.shape[1]
    gw = GLA_HEADS * GLA_DK
    mw = ML_HEADS * ML_DK

    pre = sm_ref[...] + brow_ref[...]
    i_v = pre
    f_v = _log_sigmoid(pltpu.roll(pre, LANES - (SM_MF - SM_MI), 1))
    m_prev = m_ref[...]
    mt = jnp.maximum(f_v + m_prev, i_v)
    w_intra = jnp.exp(i_v - mt)
    w_inter = jnp.exp(f_v + m_prev - mt)
    inv_floor = jnp.exp(-mt)
    m_out[...] = mt
    mq = mq_ref[...]
    mk = mk_ref[...] * (ML_DK ** -0.5)
    n_old = n_ref[...]
    gq = gq_ref[...] * (GLA_DK ** -0.5)
    gk = gk_ref[...]
    qk_g, qk_m, qn_m, den = [], [], [], []
    keep_b = jnp.zeros((rows, mw), F32)
    wnew_b = jnp.zeros((rows, mw), F32)
    for h in range(ML_HEADS):
        hm = _head_mask(mw, ML_HEADS, h)
        qk_m.append(jnp.sum(mq * mk * hm, axis=1, keepdims=True))
        qn_m.append(jnp.sum(mq * n_old * hm, axis=1, keepdims=True))
        wi_h = w_intra[:, SM_MI + h:SM_MI + h + 1]
        we_h = w_inter[:, SM_MI + h:SM_MI + h + 1]
        nq = wi_h * qk_m[h] + we_h * qn_m[h]
        den.append(jnp.maximum(jnp.abs(nq), inv_floor[:, SM_MI + h:SM_MI + h + 1]))
        keep_b = keep_b + hm * we_h
        wnew_b = wnew_b + hm * wi_h
    n_out[...] = keep_b * n_old + wnew_b * mk
    for h in range(GLA_HEADS):
        qk_g.append(jnp.sum(gq * gk * _head_mask(gw, GLA_HEADS, h), axis=1, keepdims=True))

    xt = _bdot(wat_ref[...], smt_ref[...]) + bat_ref[...]
    at = jnp.exp(_log_sigmoid(xt) * (1.0 / GLA_GATE_TEMP))
    seq_lane = lax.broadcasted_iota(jnp.int32, (nb, LANES), 0)

    for r in range(rows):
        onehot = (seq_lane == blk * rows + r).astype(BF16)
        a_bc = _dot01_r(at, onehot)
        gk_bc = _dot01_r(gkt_ref[...], onehot)
        gq_bc = _dot01_r(gqt_ref[...], onehot) * (GLA_DK ** -0.5)
        mk_bc = _dot01_r(mkt_ref[...], onehot) * (ML_DK ** -0.5)
        mq_bc = _dot01_r(mqt_ref[...], onehot)
        for h in range(GLA_HEADS):
            sl = slice(h * GLA_DK, (h + 1) * GLA_DK)
            v_row = gv_ref[r:r + 1, h * GLA_DV:(h + 1) * GLA_DV]
            s_new = a_bc[sl] * s_ref[r, h] + gk_bc[sl] * v_row
            s_out[r, h] = s_new
            o = jnp.sum(gq_bc[sl] * s_new, axis=0, keepdims=True)
            gg = gg_ref[r:r + 1, h * GLA_DV:(h + 1) * GLA_DV]
            y = _rms(o) * gn_ref[:, h * GLA_DV:(h + 1) * GLA_DV] * (gg * _sigmoid(gg))
            bo_ref[r:r + 1, h * GLA_DV:(h + 1) * GLA_DV] = y.astype(BF16)
        for h in range(ML_HEADS):
            sl = slice(h * ML_DK, (h + 1) * ML_DK)
            v_row = mv_ref[r:r + 1, h * ML_DV:(h + 1) * ML_DV]
            c_old = c_ref[r, h]
            wi_s = w_intra[r:r + 1, SM_MI + h:SM_MI + h + 1]
            we_s = w_inter[r:r + 1, SM_MI + h:SM_MI + h + 1]
            num = (wi_s * qk_m[h][r:r + 1, :]) * v_row + we_s * jnp.sum(mq_bc[sl] * c_old, axis=0, keepdims=True)
            hh = num / den[h][r:r + 1, :]
            c_out[r, h] = we_s * c_old + wi_s * (mk_bc[sl] * v_row)
            mo = mo_ref[r:r + 1, h * ML_DV:(h + 1) * ML_DV]
            y = _rms(hh) * mn_ref[:, h * ML_DV:(h + 1) * ML_DV] * _sigmoid(mo)
            co_ref[r:r + 1, h * ML_DV:(h + 1) * ML_DV] = y.astype(BF16)


def rec_step(z, zt, wa, ba, wat, bat, brow, gn, mn, s0, c0, layer, n0, m0):
    nb = z.shape[0]
    rows = SUBLANES
    gw = GLA_HEADS * GLA_DK
    mw = ML_HEADS * ML_DK
    gvw = GLA_HEADS * GLA_DV
    mvw = ML_HEADS * ML_DV
    row = lambda w, off: pl.BlockSpec((rows, w), lambda i, o=off // w: (i, o))
    colb = lambda w, off: pl.BlockSpec((w, nb), lambda i, o=off // w: (o, 0))
    c2 = lambda a: pl.BlockSpec(a.shape, lambda i: (0, 0))
    st4 = lambda hds, dk, dv: pl.BlockSpec((rows, hds, dk, dv), lambda i: (i, 0, 0, 0))
    st5 = lambda hds, dk, dv: pl.BlockSpec((None, rows, hds, dk, dv), lambda i: (layer, i, 0, 0, 0))
    return pl.pallas_call(
        functools.partial(_rec_step_kernel, rows=rows),
        out_shape=(jax.ShapeDtypeStruct((nb, gvw), BF16), jax.ShapeDtypeStruct((nb, mvw), BF16),
                   jax.ShapeDtypeStruct(s0.shape[1:], F32), jax.ShapeDtypeStruct(c0.shape[1:], F32),
                   jax.ShapeDtypeStruct((nb, mw), F32), jax.ShapeDtypeStruct((nb, LANES), F32)),
        grid=(nb // rows,),
        in_specs=[row(gw, Z_GQ), row(gw, Z_GK), row(gvw, Z_GV), row(gvw, Z_GG),
                  row(mw, Z_MQ), row(mw, Z_MK), row(mvw, Z_MV), row(mvw, Z_MO), row(LANES, Z_SMALL),
                  colb(gw, Z_GQ), colb(gw, Z_GK), colb(mw, Z_MQ), colb(mw, Z_MK), colb(LANES, Z_SMALL),
                  c2(wa), c2(ba), c2(wat), c2(bat), c2(brow), c2(gn), c2(mn),
                  st5(GLA_HEADS, GLA_DK, GLA_DV), st5(ML_HEADS, ML_DK, ML_DV),
                  pl.BlockSpec((rows, mw), lambda i: (i, 0)), pl.BlockSpec((rows, LANES), lambda i: (i, 0))],
        out_specs=(pl.BlockSpec((rows, gvw), lambda i: (i, 0)), pl.BlockSpec((rows, mvw), lambda i: (i, 0)),
                   st4(GLA_HEADS, GLA_DK, GLA_DV), st4(ML_HEADS, ML_DK, ML_DV),
                   pl.BlockSpec((rows, mw), lambda i: (i, 0)), pl.BlockSpec((rows, LANES), lambda i: (i, 0))),
        compiler_params=_cp("arbitrary"),
        name="rec_step",
    )(z, z, z, z, z, z, z, z, z, zt, zt, zt, zt, zt, wa, ba, wat, bat, brow, gn, mn, s0, c0, n0, m0)


def _mla_prep_s_kernel(qc_ref, kvc_ref, sm_ref, gq_ref, gkv_ref, wq_ref, wukt_ref,
                       cq_ref, sqa_ref, sqb_ref, ck_ref, ska_ref, skb_ref,
                       ql_out, qp_out, lat_out, kr_out):
    qn = (_rms(qc_ref[...]) * gq_ref[...]).astype(BF16)
    lat_out[...] = _rms(kvc_ref[...]) * gkv_ref[...]
    krf = _rope_lanes(sm_ref[...], ck_ref[...], ska_ref[...], skb_ref[...])
    kr_out[...] = krf[:, :MLA_ROPE]
    cq, sqa, sqb = cq_ref[...], sqa_ref[...], sqb_ref[...]
    for h in range(MLA_HEADS):
        qh = _rope_lanes(jnp.dot(qn, wq_ref[h], preferred_element_type=F32), cq, sqa, sqb)
        ql_out[h] = _bdot(qh, wukt_ref[h])
        qp_out[h] = pltpu.roll(qh, LANES - MLA_NOPE, 1)


def mla_prep_s(z, gq, gkv, wq, wukt, tabs):
    nb = z.shape[0]
    hh = MLA_HEADS
    row = lambda w, off: pl.BlockSpec((nb, w), lambda i, o=off // w: (0, o))
    c2 = lambda a: pl.BlockSpec(a.shape, lambda i: (0, 0))
    c3 = lambda a: pl.BlockSpec(a.shape, lambda i: (0, 0, 0))
    return pl.pallas_call(
        _mla_prep_s_kernel,
        out_shape=(jax.ShapeDtypeStruct((hh, nb, LANES), F32), jax.ShapeDtypeStruct((hh, nb, LANES), F32),
                   jax.ShapeDtypeStruct((nb, KV_LORA), F32), jax.ShapeDtypeStruct((nb, MLA_ROPE), F32)),
        grid=(1,),
        in_specs=[row(Q_LORA, Z_QC), row(KV_LORA, Z_KVC), row(LANES, Z_SMALL),
                  c2(gq), c2(gkv), c3(wq), c3(wukt)] + [c2(t) for t in tabs],
        out_specs=(pl.BlockSpec((hh, nb, LANES), lambda i: (0, 0, 0)),
                   pl.BlockSpec((hh, nb, LANES), lambda i: (0, 0, 0)),
                   pl.BlockSpec((nb, KV_LORA), lambda i: (0, 0)), pl.BlockSpec((nb, MLA_ROPE), lambda i: (0, 0))),
        compiler_params=_cp("arbitrary"),
        name="mla_prep_sample",
    )(z, z, z, gq, gkv, wq, wukt, *tabs)


def _mla_decode_kernel(pt_ref, ql_ref, qp_ref, ln_ref, kn_ref, lat_hbm, kr_hbm, o_ref, lat_buf, kr_buf, sem,
                       *, layer, n_pages):
    b = pl.program_id(0)
    slot = lax.rem(b, 2)

    def page_copies(seq, sl, p):
        page = pt_ref[seq, p]
        off = pl.multiple_of(p * PAGE_SIZE, PAGE_SIZE)
        return (pltpu.make_async_copy(lat_hbm.at[layer, page], lat_buf.at[sl, pl.ds(off, PAGE_SIZE), :],
                                      sem.at[0, sl]),
                pltpu.make_async_copy(kr_hbm.at[layer, page], kr_buf.at[sl, :, pl.ds(off, PAGE_SIZE)],
                                      sem.at[1, sl]))

    def start_all(seq, sl):
        def body(p, carry):
            for c in page_copies(seq, sl, p):
                c.start()
            return carry
        lax.fori_loop(0, n_pages, body, 0, unroll=4)

    def wait_all(seq, sl):
        def body(p, carry):
            for c in page_copies(seq, sl, p):
                c.wait()
            return carry
        lax.fori_loop(0, n_pages, body, 0, unroll=4)

    @pl.when(b == 0)
    def _():
        start_all(0, 0)

    @pl.when(b + 1 < pl.num_programs(0))
    def _():
        start_all(b + 1, 1 - slot)

    wait_all(b, slot)

    ql = ql_ref[0]
    qp = qp_ref[0][:, :MLA_ROPE]
    lat = lat_buf[slot]
    latb = lat.astype(BF16)
    s = (lax.dot_general(ql.astype(BF16), latb, (((1,), (1,)), ((), ())), preferred_element_type=F32)
         + jnp.dot(qp.astype(BF16), kr_buf[slot].astype(BF16), preferred_element_type=F32)) * MLA_SCALE
    lat_new = ln_ref[0]
    s_new = (jnp.sum(ql * lat_new, axis=1, keepdims=True)
             + jnp.sum(qp * kn_ref[0], axis=1, keepdims=True)) * MLA_SCALE
    m = jnp.maximum(jnp.max(s, axis=1, keepdims=True), s_new)
    p = jnp.exp(s - m)
    pn = jnp.exp(s_new - m)
    num = jnp.dot(p.astype(BF16), latb, preferred_element_type=F32) + pn * lat_new
    o_ref[0] = num / (jnp.sum(p, axis=1, keepdims=True) + pn)


def mla_decode(page_table, ql, qp, lat_new, kr_new, cache_lat, cache_rope_t, layer):
    nb, n_pages = page_table.shape
    past = n_pages * PAGE_SIZE
    hh = MLA_HEADS
    seq = lambda w: pl.BlockSpec((1, hh, w), lambda b, pt: (b, 0, 0))
    new = lambda w: pl.BlockSpec((1, 1, w), lambda b, pt: (b, 0, 0))
    return pl.pallas_call(
        functools.partial(_mla_decode_kernel, layer=layer, n_pages=n_pages),
        out_shape=jax.ShapeDtypeStruct((nb, hh, KV_LORA), F32),
        grid_spec=pltpu.PrefetchScalarGridSpec(
            num_scalar_prefetch=1,
            grid=(nb,),
            in_specs=[seq(LANES), seq(LANES), new(KV_LORA), new(MLA_ROPE),
                      pl.BlockSpec(memory_space=pl.ANY), pl.BlockSpec(memory_space=pl.ANY)],
            out_specs=pl.BlockSpec((1, hh, KV_LORA), lambda b, pt: (b, 0, 0)),
            scratch_shapes=[pltpu.VMEM((2, past, KV_LORA), F32), pltpu.VMEM((2, MLA_ROPE, past), F32),
                            pltpu.SemaphoreType.DMA((2, 2))]),
        compiler_params=_cp("arbitrary"),
        name="mla_decode",
    )(page_table, ql, qp, lat_new, kr_new, cache_lat, cache_rope_t)


def _uv_proj_kernel(o_ref, w_ref, a_ref):
    a_ref[...] = _bdot(o_ref[...], w_ref[0]).astype(BF16)


def uv_proj(o_lat, wuv):
    nb = o_lat.shape[0]
    return pl.pallas_call(
        _uv_proj_kernel,
        out_shape=jax.ShapeDtypeStruct((nb, MLA_HEADS * LANES), BF16),
        grid=(MLA_HEADS,),
        in_specs=[pl.BlockSpec((nb, KV_LORA), lambda h: (0, h)),
                  pl.BlockSpec((1, KV_LORA, LANES), lambda h: (h, 0, 0))],
        out_specs=pl.BlockSpec((nb, LANES), lambda h: (0, h)),
        compiler_params=_cp("arbitrary"),
        name="uv_proj",
    )(o_lat, wuv)


def _merge_kernel(a_ref, b_ref, c_ref, ga_ref, gb_ref, gc_ref, x_ref, g1_ref, wa_ref, wb_ref, wc_ref, wo_ref,
                  nf_ref, sc_ref, sh_ref, x1_ref, h2_ref):
    merged = (_sigmoid(ga_ref[...]) * jnp.dot(a_ref[...], wa_ref[...], preferred_element_type=F32)
              + _sigmoid(gb_ref[...]) * jnp.dot(b_ref[...], wb_ref[...], preferred_element_type=F32)
              + _sigmoid(gc_ref[...]) * jnp.dot(c_ref[...], wc_ref[...], preferred_element_type=F32))
    mix = _bdot(merged, wo_ref[...])
    x1 = x_ref[...] + g1_ref[...] * mix
    x1_ref[...] = x1
    h2 = _rms(x1) * nf_ref[...] * (1.0 + sc_ref[...]) + sh_ref[...]
    h2_ref[...] = h2.astype(BF16)


def merge(a, b, c, z, x, gate1, wa, wb, wc, wo, nf, scale2, shift2, tm):
    t, d = x.shape
    per_tok = gate1.shape[0] != 1
    mod_spec = (pl.BlockSpec((tm, d), lambda i: (i, 0)) if per_tok else pl.BlockSpec((1, d), lambda i: (0, 0)))
    rowb = lambda w, o=0: pl.BlockSpec((tm, w), lambda i, o=o: (i, o))
    c2 = lambda arr: pl.BlockSpec(arr.shape, lambda i: (0, 0))
    return pl.pallas_call(
        _merge_kernel,
        out_shape=(jax.ShapeDtypeStruct((t, d), F32), jax.ShapeDtypeStruct((t, d), BF16)),
        grid=(t // tm,),
        in_specs=[rowb(a.shape[1]), rowb(b.shape[1]), rowb(c.shape[1]),
                  rowb(d, Z_GA // d), rowb(d, Z_GB // d), rowb(d, Z_GC // d), rowb(d), mod_spec,
                  c2(wa), c2(wb), c2(wc), c2(wo), pl.BlockSpec((1, d), lambda i: (0, 0)), mod_spec, mod_spec],
        out_specs=(rowb(d), rowb(d)),
        compiler_params=_cp("arbitrary"),
        name="merge",
    )(a, b, c, z, z, z, x, gate1, wa, wb, wc, wo, nf.reshape(1, d), scale2, shift2)


def _oddeven_merge(lo, hi, r):
    step = r * 2
    if step < hi - lo:
        yield from _oddeven_merge(lo, hi, step)
        yield from _oddeven_merge(lo + r, hi, step)
        yield from [(i, i + r) for i in range(lo + r, hi - r, step)]
    else:
        yield (lo, lo + r)


def _oddeven_merge_sort(lo, hi):
    if hi - lo >= 1:
        mid = lo + (hi - lo) // 2
        yield from _oddeven_merge_sort(lo, mid)
        yield from _oddeven_merge_sort(mid + 1, hi)
        yield from _oddeven_merge(lo, hi, 1)


def _top_rows(x, n):
    groups = x.shape[0] // SUBLANES
    width = 1 << (groups - 1).bit_length()
    cols = [x[SUBLANES * r:SUBLANES * (r + 1)] for r in range(groups)]
    cols += [jnp.full_like(cols[0], -jnp.inf)] * (width - groups)
    for i, j in _oddeven_merge_sort(0, width - 1):
        cols[i], cols[j] = jnp.maximum(cols[i], cols[j]), jnp.minimum(cols[i], cols[j])
    rows = []
    for k in range(n):
        head = cols[0]
        m = jnp.max(head, axis=0, keepdims=True)
        rows.append(m)
        hit = head == m
        depth = min(width, n - k)
        for r in range(depth - 1):
            cols[r] = jnp.where(hit, cols[r + 1], cols[r])
        cols[depth - 1] = jnp.where(hit, -jnp.inf, cols[depth - 1])
    return rows


def _peer_topk_kernel(h2_ref, wq_ref, k1_ref, k2_ref, thr_out, s2_out, l1_out):
    h2 = h2_ref[...]
    k1 = PEER_TOPK + 1
    for h in range(PEER_HEADS):
        qh = jnp.dot(h2, wq_ref[h], preferred_element_type=F32)
        s1 = _bdot_nt(k1_ref[h], qh[:, :PEER_HALF])
        s2 = _bdot_nt(k2_ref[h], qh[:, PEER_HALF:])
        a = _top_rows(s1, k1)
        b = _top_rows(s2, k1)
        n_rows = -(-k1 // SUBLANES) * SUBLANES
        ninf = jnp.full_like(b[0], -jnp.inf)
        split = 4
        b_all = jnp.concatenate(b + [ninf] * (n_rows - k1), axis=0)
        a_tail = jnp.concatenate([ninf] * split + a[split:] + [ninf] * (n_rows - k1), axis=0)
        span = lambda r: -(-(k1 // (r + 1)) // SUBLANES) * SUBLANES
        cand = [a[p] + b_all[:span(p)] for p in range(split)]
        cand += [b[q] + a_tail[:span(q)] for q in range(k1 // (split + 1))]
        best = _top_rows(jnp.concatenate(cand, axis=0), k1)
        zsum = jnp.zeros_like(best[0])
        for c in best[:PEER_TOPK]:
            zsum = zsum + jnp.exp(c - best[0])
        mid = 0.5 * (best[PEER_TOPK - 1] + best[PEER_TOPK])
        tabs = ((thr_out, ((mid - s1) - b[0]) * LOG2E), (s2_out, (s2 - b[0]) * LOG2E),
                (l1_out, (s1 - a[0]) * LOG2E - jnp.log(zsum) * LOG2E))
        for ref, val in tabs:
            for c in range(val.shape[1] // LANES):
                ref[h, c] = val[:, c * LANES:(c + 1) * LANES]


def peer_topk(h2, wq, k1, k2, tm):
    t, d = h2.shape
    hh = PEER_HEADS
    c3 = lambda a: pl.BlockSpec(a.shape, lambda i: (0, 0, 0))
    assert tm % LANES == 0
    tab = jax.ShapeDtypeStruct((hh, t // LANES, PEER_KEYS, LANES), F32)
    tab_spec = pl.BlockSpec((hh, tm // LANES, PEER_KEYS, LANES), lambda i: (0, i, 0, 0))
    return pl.pallas_call(
        _peer_topk_kernel,
        out_shape=(tab, tab, tab),
        grid=(t // tm,),
        in_specs=[pl.BlockSpec((tm, d), lambda i: (i, 0)), c3(wq), c3(k1), c3(k2)],
        out_specs=(tab_spec, tab_spec, tab_spec),
        compiler_params=_cp("arbitrary"),
        name="peer_topk",
    )(h2, wq, k1, k2)


def _peer_dense_kernel(h2_ref, u_ref, vt_ref, thr_ref, l1_ref, s2_ref, x1_ref, g2_ref, gf_ref,
                       o_ref, acc_ref, act_ref, wg_ref, *, ni, final):
    e = pl.program_id(1)

    @pl.when(e == 0)
    def _():
        acc_ref[...] = jnp.zeros(acc_ref.shape, F32)

    act_ref[...] = _gelu_tanh(lax.dot_general(u_ref[...], h2_ref[...], (((1,), (1,)), ((), ())),
                                              preferred_element_type=F32))
    tm = act_ref.shape[1]
    for c in range(tm // LANES):
        cs = slice(c * LANES, (c + 1) * LANES)
        for ii in range(ni):
            rs = slice(ii * PEER_KEYS, (ii + 1) * PEER_KEYS)
            w = None
            for h in range(PEER_HEADS):
                s2 = s2_ref[h, c]
                part = jnp.where(s2 >= thr_ref[h, c, ii:ii + 1, :],
                                 jnp.exp2(s2 + l1_ref[h, c, ii:ii + 1, :]), 0.0)
                w = part if w is None else w + part
            wg_ref[rs, cs] = (w * act_ref[rs, cs]).astype(BF16)
    acc_ref[...] += jnp.dot(vt_ref[...], wg_ref[...], preferred_element_type=F32)

    @pl.when(e == pl.num_programs(1) - 1)
    def _():
        out = x1_ref[...] + g2_ref[...] * acc_ref[...].T
        o_ref[...] = _rms(out) * gf_ref[...] if final else out


def peer_dense(h2, u, vt, layer, thr, l1, s2, x1, gate2, g_final, final, tm, ni):
    t, d = h2.shape
    n_exp = u.shape[1]
    te = ni * PEER_KEYS
    hh = PEER_HEADS
    per_tok = gate2.shape[0] != 1
    mod_spec = (pl.BlockSpec((tm, d), lambda i, e: (i, 0)) if per_tok
                else pl.BlockSpec((1, d), lambda i, e: (0, 0)))
    nc = tm // LANES
    sub = pl.BlockSpec((hh, nc, ni, LANES), lambda i, e: (0, i, e, 0))
    full = pl.BlockSpec((hh, nc, PEER_KEYS, LANES), lambda i, e: (0, i, 0, 0))
    return pl.pallas_call(
        functools.partial(_peer_dense_kernel, ni=ni, final=final),
        out_shape=jax.ShapeDtypeStruct((t, d), F32),
        grid=(t // tm, n_exp // te),
        in_specs=[pl.BlockSpec((tm, d), lambda i, e: (i, 0)),
                  pl.BlockSpec((None, te, d), lambda i, e: (layer, e, 0)),
                  pl.BlockSpec((None, d, te), lambda i, e: (layer, 0, e)),
                  sub, sub, full,
                  pl.BlockSpec((tm, d), lambda i, e: (i, 0)), mod_spec,
                  pl.BlockSpec((1, d), lambda i, e: (0, 0))],
        out_specs=pl.BlockSpec((tm, d), lambda i, e: (i, 0)),
        scratch_shapes=[pltpu.VMEM((d, tm), F32), pltpu.VMEM((te, tm), F32), pltpu.VMEM((te, tm), BF16)],
        compiler_params=_cp("arbitrary", "arbitrary"),
        name="peer_dense",
    )(h2, u, vt, thr, l1, s2, x1, gate2, g_final.reshape(1, d))


def _pad_to(a, axis, size):
    pad = [(0, 0)] * a.ndim
    pad[axis] = (0, size - a.shape[axis])
    return jnp.pad(a, pad)


def _prep_w_in(w):
    sizes = (Q_LORA, KV_LORA, MLA_ROPE, 256, 256, 512, GLA_GATE_RANK, 512, 256, 256, 512, ML_HEADS, ML_HEADS, 512,
             D_MODEL, D_MODEL, D_MODEL)
    offs = np.concatenate([[0], np.cumsum(sizes)])
    (q_c, kv_c, k_pe, gq, gk, gv, ga, gg, mq, mk, mv, mi, mf, mo, g_a, g_b, g_c) = [
        w[:, offs[i]:offs[i + 1]] for i in range(len(sizes))]
    small = _pad_to(jnp.concatenate([k_pe, ga, mi, mf], axis=1), 1, LANES)
    out = jnp.concatenate([g_a, g_b, g_c, gv, gg, mv, mo, q_c, gq, gk, mq, mk, kv_c, small], axis=1)
    assert out.shape[1] == Z_WIDTH
    return out.astype(BF16)


def _rope_tables(pos):
    half = MLA_ROPE // 2
    inv = ROPE_BASE ** (-jnp.arange(half, dtype=F32) / half)
    ang = pos.astype(F32)[:, None] * inv[None, :]
    cos, sin = jnp.cos(ang), jnp.sin(ang)
    n = pos.shape[0]
    z = lambda w: jnp.zeros((n, w), F32)
    one = jnp.ones((n, MLA_NOPE), F32)
    rest = LANES - MLA_NOPE - MLA_ROPE
    cq = jnp.concatenate([one, cos, cos, z(rest)], axis=1)
    sqa = jnp.concatenate([z(MLA_NOPE), -sin, z(half), z(rest)], axis=1)
    sqb = jnp.concatenate([z(MLA_NOPE), z(half), sin, z(rest)], axis=1)
    ck = jnp.concatenate([cos, cos, z(LANES - MLA_ROPE)], axis=1)
    ska = jnp.concatenate([-sin, z(LANES - half)], axis=1)
    skb = jnp.concatenate([z(half), sin, z(LANES - MLA_ROPE)], axis=1)
    return (cq, sqa, sqb, ck, ska, skb), (cos.T, sin.T)


def _layer_weights(l, P):
    w = {}
    w['w_in'] = _prep_w_in(P['w_in'][l])
    w['gq'] = P['mla_q_norm'][l].reshape(1, Q_LORA)
    w['gkv'] = P['mla_kv_norm'][l].reshape(1, KV_LORA)
    w['wq'] = _pad_to(jnp.transpose(P['w_uq'][l], (1, 0, 2)), 2, LANES).astype(BF16)
    wuk = jnp.transpose(P['w_uk'][l], (1, 0, 2))
    w['wuk'] = _pad_to(wuk, 2, LANES).astype(BF16)
    w['wukt'] = _pad_to(jnp.transpose(wuk, (0, 2, 1)), 1, LANES).astype(BF16)
    w['wuv'] = _pad_to(jnp.transpose(P['w_uv'][l], (1, 0, 2)), 2, LANES).astype(BF16)
    w['wqt'] = _pad_to(jnp.transpose(P['w_uq'][l], (1, 2, 0)), 1, LANES).astype(BF16)
    w['wuvt'] = _pad_to(jnp.transpose(P['w_uv'][l], (1, 2, 0)), 1, LANES).astype(BF16)
    wa = jnp.zeros((LANES, GLA_HEADS * GLA_DK), F32).at[SM_GA:SM_GA + GLA_GATE_RANK].set(P['w_gla_a'][l])
    w['gla_wa'] = wa.astype(BF16)
    w['gla_wat'] = wa.T.astype(BF16)
    w['gla_ba'] = P['b_gla_a'][l].reshape(1, -1)
    w['gla_bat'] = P['b_gla_a'][l].reshape(-1, 1)
    w['gla_norm'] = P['gla_norm'][l].reshape(1, -1)
    w['ml_norm'] = P['ml_norm'][l].reshape(1, -1)
    brow = jnp.zeros((1, LANES), F32)
    brow = brow.at[0, SM_MI:SM_MI + ML_HEADS].set(P['ml_b_i'][l]).at[0, SM_MF:SM_MF + ML_HEADS].set(P['ml_b_f'][l])
    w['ml_brow'] = brow
    w['ml_bcol'] = jnp.concatenate([P['ml_b_i'][l], P['ml_b_f'][l]]).reshape(2 * ML_HEADS, 1)
    wpa = P['w_proj_a'][l].reshape(MLA_HEADS, MLA_V, D_MODEL)
    w['wpa'] = _pad_to(wpa, 1, LANES).reshape(MLA_HEADS * LANES, D_MODEL).astype(BF16)
    w['wpb'] = P['w_proj_b'][l].astype(BF16)
    w['wpc'] = P['w_proj_c'][l].astype(BF16)
    w['wo'] = P['w_o'][l].astype(BF16)
    w['peer_wq'] = jnp.transpose(P['peer_wq'][l], (1, 0, 2)).astype(BF16)
    w['peer_k1'] = P['peer_keys'][l][:, 0].astype(BF16)
    w['peer_k2'] = P['peer_keys'][l][:, 1].astype(BF16)
    w['layer'] = l
    w['peer_u'] = P['peer_u_b']
    w['peer_vt'] = P['peer_vt_b']
    return w


def _mods(c, w_ada, b_ada, layer):
    rows = c.shape[0]
    cp = _pad_to(c, 0, SUBLANES) if rows < SUBLANES else c
    mod = ada_mod(cp, w_ada, b_ada[layer], layer)[:rows]
    return jnp.split(mod, 6, axis=-1)


def _peer_block(h2, x1, gate2, w, g_final, final, tm, ni):
    thr, s2, l1 = peer_topk(h2, w['peer_wq'], w['peer_k1'], w['peer_k2'], min(tm, 2 * LANES))
    return peer_dense(h2, w['peer_u'], w['peer_vt'], w['layer'], thr, l1, s2, x1, gate2, g_final, final, tm, ni)


def _trunk_prompt(x, c, P, W):
    t = x.shape[0]
    tm = min(512, t)
    L = min(128, t)
    consts = _chunk_constants(L)
    tabs, tabs_t = _rope_tables(jnp.arange(t, dtype=jnp.int32))
    lat_rows, rope_rows, gla_st, c_st, n_st, m_st = [], [], [], [], [], []
    for l in range(len(W)):
        w = W[l]
        shift1, scale1, gate1, shift2, scale2, gate2 = _mods(c, P['w_ada'], P['b_ada'], l)
        z, zst = norm_proj(x, P['norm_mix'][l], scale1, shift1, w['w_in'], min(2 * tm, t))
        qt, k, vt, lat, kr = mla_prep(z, w['gq'], w['gkv'], w['wqt'], w['wuk'], w['wuvt'], tabs_t, tabs, tm)
        a = flash_attention(qt, k, vt, min(2 * tm, t))
        b, st = gla_scan(z, w['gla_wa'], w['gla_ba'], w['gla_norm'], consts, L)
        cc, ct, nrow, mrow = mlstm_scan(z, zst, w['ml_brow'], w['ml_bcol'], w['ml_norm'], consts, L)
        x1, h2 = merge(a, b, cc, z, x, gate1, w['wpa'], w['wpb'], w['wpc'], w['wo'],
                       P['norm_ffn'][l], scale2, shift2, tm)
        x = _peer_block(h2, x1, gate2, w, P['norm_final'], l == len(W) - 1, tm, SUBLANES)
        lat_rows.append(lat)
        rope_rows.append(kr)
        gla_st.append(jnp.stack([st[h][:, h * GLA_DK:(h + 1) * GLA_DK].T for h in range(GLA_HEADS)]))
        c_st.append(jnp.stack([ct[h][:, h * ML_DK:(h + 1) * ML_DK].T for h in range(ML_HEADS)]))
        n_st.append(jnp.stack([nrow[h, h * ML_DK:(h + 1) * ML_DK] for h in range(ML_HEADS)]))
        m_st.append(mrow[:ML_HEADS, 0])
    return x, tuple(jnp.stack(r) for r in (lat_rows, rope_rows, gla_st, c_st, n_st, m_st))


def _trunk_sample(x, c, P, W, cache_lat, cache_rope, page_table, s_gla, s_c, s_n, s_m):
    nb = x.shape[0]
    past = page_table.shape[1] * PAGE_SIZE
    tabs, _ = _rope_tables(jnp.full((1,), past, dtype=jnp.int32))
    cache_rope = jnp.swapaxes(cache_rope, 2, 3)
    lat_rows, rope_rows, gla_st, c_st, n_st, m_st = [], [], [], [], [], []
    for l in range(len(W)):
        w = W[l]
        shift1, scale1, gate1, shift2, scale2, gate2 = _mods(c, P['w_ada'], P['b_ada'], l)
        z, _ = norm_proj(x, P['norm_mix'][l], scale1, shift1, w['w_in'], nb)
        ql, qp, lat, kr = mla_prep_s(z, w['gq'], w['gkv'], w['wq'], w['wukt'], tabs)
        o_lat = mla_decode(page_table, jnp.transpose(ql, (1, 0, 2)), jnp.transpose(qp, (1, 0, 2)),
                           lat[:, None, :], kr[:, None, :], cache_lat, cache_rope, l)
        a = uv_proj(o_lat.reshape(nb, MLA_HEADS * KV_LORA), w['wuv'])
        m_in = jnp.zeros((nb, LANES), F32).at[:, SM_MI:SM_MI + ML_HEADS].set(s_m[l])
        b, cc, s_new, c_new, n_new, m_new = rec_step(
            z, z.T, w['gla_wa'], w['gla_ba'], w['gla_wat'], w['gla_bat'], w['ml_brow'], w['gla_norm'], w['ml_norm'],
            s_gla, s_c, l, s_n[l].reshape(nb, ML_HEADS * ML_DK), m_in)
        x1, h2 = merge(a, b, cc, z, x, gate1, w['wpa'], w['wpb'], w['wpc'], w['wo'],
                       P['norm_ffn'][l], scale2, shift2, nb)
        x = _peer_block(h2, x1, gate2, w, P['norm_final'], l == len(W) - 1, nb, SUBLANES)
        lat_rows.append(lat[:, None, :])
        rope_rows.append(kr[:, None, :])
        gla_st.append(s_new)
        c_st.append(c_new)
        n_st.append(n_new.reshape(nb, ML_HEADS, ML_DK))
        m_st.append(m_new[:, SM_MI:SM_MI + ML_HEADS])
    return x, tuple(jnp.stack(r) for r in (lat_rows, rope_rows, gla_st, c_st, n_st, m_st))


def kernel(x_prompt, x_sample, cache_kv_latent, cache_k_rope, state_gla, state_mlstm_C, state_mlstm_n, state_mlstm_m, page_table, c_prompt, c_sample, w_ada, b_ada, norm_mix, norm_ffn, norm_final, w_in, mla_q_norm, mla_kv_norm, w_uq, w_uk, w_uv, w_gla_a, b_gla_a, gla_norm, ml_b_i, ml_b_f, ml_norm, w_proj_a, w_proj_b, w_proj_c, w_o, peer_wq, peer_keys, peer_u, peer_v):
    P = {'w_ada': w_ada, 'b_ada': b_ada, 'norm_mix': norm_mix, 'norm_ffn': norm_ffn, 'norm_final': norm_final,
         'w_in': w_in, 'mla_q_norm': mla_q_norm, 'mla_kv_norm': mla_kv_norm, 'w_uq': w_uq, 'w_uk': w_uk,
         'w_uv': w_uv, 'w_gla_a': w_gla_a, 'b_gla_a': b_gla_a, 'gla_norm': gla_norm, 'ml_b_i': ml_b_i,
         'ml_b_f': ml_b_f, 'ml_norm': ml_norm, 'w_proj_a': w_proj_a, 'w_proj_b': w_proj_b,
         'w_proj_c': w_proj_c, 'w_o': w_o, 'peer_wq': peer_wq, 'peer_keys': peer_keys, 'peer_u': peer_u,
         'peer_v': peer_v}
    depth = w_in.shape[0]
    P['peer_u_b'] = peer_u.astype(BF16)
    P['peer_vt_b'] = jnp.swapaxes(peer_v, 1, 2).astype(BF16)
    W = [_layer_weights(l, P) for l in range(depth)]
    bp, sp, d = x_prompt.shape
    assert bp == 1 and x_sample.shape[1] == 1
    nb = x_sample.shape[0]
    y_p, (lat_p, rope_p, gla_p, c_p, n_p, m_p) = _trunk_prompt(x_prompt.reshape(sp, d), c_prompt, P, W)
    y_s, (lat_s, rope_s, gla_s, c_s, n_s, m_s) = _trunk_sample(
        x_sample.reshape(nb, d), c_sample, P, W, cache_kv_latent, cache_k_rope, page_table,
        state_gla, state_mlstm_C, state_mlstm_n, state_mlstm_m)
    return (y_p.reshape(bp, sp, d), y_s.reshape(nb, 1, d),
            lat_p[:, None], rope_p[:, None], gla_p[:, None], c_p[:, None], n_p[:, None], m_p[:, None],
            lat_s, rope_s, gla_s, c_s, n_s, m_s)
```

```python
import functools

import numpy as np
import jax
import jax.numpy as jnp
from jax import lax
from jax.experimental import pallas as pl
from jax.experimental.pallas import tpu as pltpu

F32 = jnp.float32
BF16 = jnp.bfloat16

D_MODEL = 1024
PAGE_SIZE = 128
MLA_HEADS = 8
MLA_NOPE = 64
MLA_ROPE = 32
MLA_V = 64
Q_LORA = 256
KV_LORA = 128
ROPE_BASE = 10000.0
MLA_SCALE = (MLA_NOPE + MLA_ROPE) ** -0.5
GLA_HEADS = 4
GLA_DK = 64
GLA_DV = 128
GLA_GATE_RANK = 16
GLA_GATE_TEMP = 16.0
ML_HEADS = 4
ML_DK = 64
ML_DV = 128
PEER_HEADS = 8
PEER_KEYS = 128
PEER_HALF = 128
PEER_TOPK = 16
NORM_EPS = 1e-6

LANES = 128
SUBLANES = 8
VMEM_LIMIT = 48 * 1024 * 1024

NEG = -1e30
LOG2E = 1.4426950408889634
PEER_TILE_ROWS = 16

Z_GA, Z_GB, Z_GC = 0, 1024, 2048
Z_GV, Z_GG, Z_MV, Z_MO = 3072, 3584, 4096, 4608
Z_QC, Z_GQ, Z_GK, Z_MQ, Z_MK = 5120, 5376, 5632, 5888, 6144
Z_KVC, Z_SMALL = 6400, 6528
Z_WIDTH = 6656
SM_KPE, SM_GA, SM_MI, SM_MF = 0, 32, 48, 52


def _cp(*sem):
    return pltpu.CompilerParams(dimension_semantics=tuple(sem), vmem_limit_bytes=VMEM_LIMIT)


def _bdot(a, b):
    return jnp.dot(a.astype(BF16), b.astype(BF16), preferred_element_type=F32)


def _bdot_nt(a, b):
    return lax.dot_general(a.astype(BF16), b.astype(BF16), (((1,), (1,)), ((), ())),
                           preferred_element_type=F32)


def _split3(x):
    hi = x.astype(BF16)
    r = x - hi.astype(F32)
    mid = r.astype(BF16)
    lo = (r - mid.astype(F32)).astype(BF16)
    return hi, mid, lo


def _dot01(m01, x):
    hi, mid, lo = _split3(x)
    return (jnp.dot(m01, hi, preferred_element_type=F32) + jnp.dot(m01, mid, preferred_element_type=F32)
            + jnp.dot(m01, lo, preferred_element_type=F32))


def _dot01_r(x, m01):
    hi, mid, lo = _split3(x)
    return (jnp.dot(hi, m01, preferred_element_type=F32) + jnp.dot(mid, m01, preferred_element_type=F32)
            + jnp.dot(lo, m01, preferred_element_type=F32))


def _sigmoid(x):
    return 1.0 / (1.0 + jnp.exp(-x))


def _log_sigmoid(x):
    return jnp.minimum(x, 0.0) - jnp.log(1.0 + jnp.exp(-jnp.abs(x)))


def _rms(x):
    return x * lax.rsqrt(jnp.mean(x * x, axis=-1, keepdims=True) + NORM_EPS)


def _gelu_tanh(x):
    return 0.5 * x * (1.0 + jnp.tanh(0.7978845608028654 * (x + 0.044715 * (x * x * x))))


def _ada_kernel(c_ref, w_ref, b_ref, o_ref):
    c = c_ref[...]
    o_ref[...] = _bdot(c * _sigmoid(c), w_ref[...]) + b_ref[...]


def ada_mod(c, w, b, layer):
    rows, d = c.shape
    n = w.shape[2]
    tn = 1536
    return pl.pallas_call(
        _ada_kernel,
        out_shape=jax.ShapeDtypeStruct((rows, n), F32),
        grid=(n // tn,),
        in_specs=[pl.BlockSpec((rows, d), lambda j: (0, 0)),
                  pl.BlockSpec((None, d, tn), lambda j: (layer, 0, j)),
                  pl.BlockSpec((1, tn), lambda j: (0, j))],
        out_specs=pl.BlockSpec((rows, tn), lambda j: (0, j)),
        compiler_params=_cp("arbitrary"),
        name="ada_mod",
    )(c, w, b.reshape(1, n))


def _norm_proj_kernel(x_ref, g_ref, sc_ref, sh_ref, w_ref, o_ref, st_ref, hn_ref, *, tn):
    @pl.when(pl.program_id(1) == 0)
    def _():
        h = _rms(x_ref[...]) * g_ref[...] * (1.0 + sc_ref[...]) + sh_ref[...]
        hn_ref[...] = h.astype(BF16)

    res = jnp.dot(hn_ref[...], w_ref[...], preferred_element_type=F32)
    o_ref[...] = res

    @pl.when(pl.program_id(1) == Z_SMALL // tn)
    def _():
        off = Z_SMALL % tn
        st_ref[...] = res[:, off:off + LANES].T


def norm_proj(x, g, scale, shift, w, tm):
    t, d = x.shape
    n = w.shape[1]
    tn = 512
    per_tok = scale.shape[0] != 1
    mod_spec = (pl.BlockSpec((tm, d), lambda i, j: (i, 0)) if per_tok
                else pl.BlockSpec((1, d), lambda i, j: (0, 0)))
    return pl.pallas_call(
        functools.partial(_norm_proj_kernel, tn=tn),
        out_shape=(jax.ShapeDtypeStruct((t, n), F32), jax.ShapeDtypeStruct((LANES, t), F32)),
        grid=(t // tm, n // tn),
        in_specs=[pl.BlockSpec((tm, d), lambda i, j: (i, 0)),
                  pl.BlockSpec((1, d), lambda i, j: (0, 0)),
                  mod_spec, mod_spec,
                  pl.BlockSpec((d, tn), lambda i, j: (0, j))],
        out_specs=(pl.BlockSpec((tm, tn), lambda i, j: (i, j)), pl.BlockSpec((LANES, tm), lambda i, j: (0, i))),
        scratch_shapes=[pltpu.VMEM((tm, d), BF16)],
        compiler_params=_cp("arbitrary", "arbitrary"),
        name="norm_proj",
    )(x, g.reshape(1, d), scale, shift, w)


def _rope_lanes(x, c, sa, sb):
    return x * c + pltpu.roll(x, LANES - 16, 1) * sa + pltpu.roll(x, 16, 1) * sb


def _mla_prep_kernel(qc_ref, kvc_ref, sm_ref, gq_ref, gkv_ref, wqt_ref, wuk_ref, wuvt_ref,
                     ct_ref, st_ref, ck_ref, ska_ref, skb_ref,
                     qt_out, k_out, vt_out, lat_out, kr_out):
    half = MLA_ROPE // 2
    qn = (_rms(qc_ref[...]) * gq_ref[...]).astype(BF16)
    lat = _rms(kvc_ref[...]) * gkv_ref[...]
    lat_out[...] = lat
    latb = lat.astype(BF16)
    krf = _rope_lanes(sm_ref[...], ck_ref[...], ska_ref[...], skb_ref[...])
    kr_out[...] = krf[:, :MLA_ROPE]
    kr_sh = pltpu.roll(krf, MLA_NOPE, 1)
    cos_t, sin_t = ct_ref[...], st_ref[...]
    tm = qn.shape[0]
    ones_row = (lax.broadcasted_iota(jnp.int32, (LANES, tm), 0) == MLA_V).astype(F32)
    pad = jnp.zeros((LANES - MLA_NOPE - MLA_ROPE, tm), F32)
    for h in range(MLA_HEADS):
        qt = lax.dot_general(wqt_ref[h], qn, (((1,), (1,)), ((), ())), preferred_element_type=F32)
        x1 = qt[MLA_NOPE:MLA_NOPE + half]
        x2 = qt[MLA_NOPE + half:MLA_NOPE + MLA_ROPE]
        qt = jnp.concatenate([qt[:MLA_NOPE], x1 * cos_t - x2 * sin_t, x1 * sin_t + x2 * cos_t, pad], axis=0)
        qt_out[h] = (qt * (MLA_SCALE * LOG2E)).astype(BF16)
        k_out[h] = (jnp.dot(latb, wuk_ref[h], preferred_element_type=F32) + kr_sh).astype(BF16)
        vt = lax.dot_general(wuvt_ref[h], latb, (((1,), (1,)), ((), ())), preferred_element_type=F32)
        vt_out[h] = (vt + ones_row).astype(BF16)


def mla_prep(z, gq, gkv, wqt, wuk, wuvt, tabs_t, tabs, tm):
    t = z.shape[0]
    hh = MLA_HEADS
    row = lambda w, off: pl.BlockSpec((tm, w), lambda i, o=off // w: (i, o))
    const2 = lambda a: pl.BlockSpec(a.shape, lambda i: (0, 0))
    const3 = lambda a: pl.BlockSpec(a.shape, lambda i: (0, 0, 0))
    tab_spec = pl.BlockSpec((tm, LANES), lambda i: (i, 0))
    tabt_spec = pl.BlockSpec((MLA_ROPE // 2, tm), lambda i: (0, i))
    return pl.pallas_call(
        _mla_prep_kernel,
        out_shape=(jax.ShapeDtypeStruct((hh, LANES, t), BF16),
                   jax.ShapeDtypeStruct((hh, t, LANES), BF16),
                   jax.ShapeDtypeStruct((hh, LANES, t), BF16),
                   jax.ShapeDtypeStruct((t, KV_LORA), F32),
                   jax.ShapeDtypeStruct((t, MLA_ROPE), F32)),
        grid=(t // tm,),
        in_specs=[row(Q_LORA, Z_QC), row(KV_LORA, Z_KVC), row(LANES, Z_SMALL),
                  const2(gq), const2(gkv), const3(wqt), const3(wuk), const3(wuvt),
                  tabt_spec, tabt_spec, tab_spec, tab_spec, tab_spec],
        out_specs=(pl.BlockSpec((hh, LANES, tm), lambda i: (0, 0, i)),
                   pl.BlockSpec((hh, tm, LANES), lambda i: (0, i, 0)),
                   pl.BlockSpec((hh, LANES, tm), lambda i: (0, 0, i)),
                   pl.BlockSpec((tm, KV_LORA), lambda i: (i, 0)),
                   pl.BlockSpec((tm, MLA_ROPE), lambda i: (i, 0))),
        compiler_params=_cp("arbitrary"),
        name="mla_prep",
    )(z, z, z, gq, gkv, wqt, wuk, wuvt, *tabs_t, *tabs[3:])


def _flash_kernel(qi_ref, ki_ref, qt_ref, k_ref, vt_ref, o_ref, m_ref, acc_ref, *, tq):
    qi = qi_ref[pl.program_id(0)]
    ki = ki_ref[pl.program_id(0)]

    @pl.when(ki == 0)
    def _():
        m_ref[...] = jnp.full(m_ref.shape, NEG, F32)
        acc_ref[...] = jnp.zeros(acc_ref.shape, F32)

    def scores(h):
        return jnp.dot(k_ref[h], qt_ref[h], preferred_element_type=F32)

    def accumulate(h, p, alpha):
        acc_ref[h] = alpha * acc_ref[h] + jnp.dot(vt_ref[h], p, preferred_element_type=F32)

    def step(diag):
        if diag:
            key = lax.broadcasted_iota(jnp.int32, (tq, tq), 0)
            qry = lax.broadcasted_iota(jnp.int32, (tq, tq), 1)
            keep = key <= qry
        s_next = scores(0)
        pending = None
        for h in range(MLA_HEADS):
            s = s_next
            if h + 1 < MLA_HEADS:
                s_next = scores(h + 1)
            if pending is not None:
                accumulate(*pending)
            if diag:
                s = jnp.where(keep, s, NEG)
            m_prev = m_ref[h:h + 1, :]
            m_new = jnp.maximum(m_prev, jnp.max(s, axis=0, keepdims=True))
            p = jnp.exp2(s - m_new).astype(BF16)
            m_ref[h:h + 1, :] = m_new
            pending = (h, p, jnp.exp2(m_prev - m_new))
        accumulate(*pending)

    @pl.when(ki < qi)
    def _():
        step(False)

    @pl.when(ki == qi)
    def _():
        step(True)
        for h in range(MLA_HEADS):
            acc = acc_ref[h]
            o_ref[:, h * LANES:(h + 1) * LANES] = (acc / acc[MLA_V:MLA_V + 1, :]).T.astype(BF16)


def flash_attention(qt, k, vt, tq):
    hh, t, _ = k.shape
    n = t // tq
    pairs = np.array([(i, j) for i in range(n) for j in range(i + 1)], np.int32)
    return pl.pallas_call(
        functools.partial(_flash_kernel, tq=tq),
        out_shape=jax.ShapeDtypeStruct((t, hh * LANES), BF16),
        grid_spec=pltpu.PrefetchScalarGridSpec(
            num_scalar_prefetch=2,
            grid=(pairs.shape[0],),
            in_specs=[pl.BlockSpec((hh, LANES, tq), lambda s, qa, ka: (0, 0, qa[s])),
                      pl.BlockSpec((hh, tq, LANES), lambda s, qa, ka: (0, ka[s], 0)),
                      pl.BlockSpec((hh, LANES, tq), lambda s, qa, ka: (0, 0, ka[s]))],
            out_specs=pl.BlockSpec((tq, hh * LANES), lambda s, qa, ka: (qa[s], 0)),
            scratch_shapes=[pltpu.VMEM((hh, tq), F32), pltpu.VMEM((hh, LANES, tq), F32)]),
        compiler_params=_cp("arbitrary"),
        name="flash_attention",
    )(jnp.asarray(pairs[:, 0]), jnp.asarray(pairs[:, 1]), qt, k, vt)


def _chunk_constants(L):
    nl = int(np.log2(L))
    assert 2 ** nl == L
    idx = np.arange(L)
    tri = (idx[None, :] <= idx[:, None]).astype(np.float32)
    mq, mk, pm = [], [], []
    for lev in range(nl):
        p = 1 << lev
        start_right = (idx >> (lev + 1) << (lev + 1)) + p
        right = ((idx >> lev) & 1) == 1
        u = idx[None, :]
        mq.append((right[:, None] & (u >= start_right[:, None]) & (u <= idx[:, None])).astype(np.float32))
        mk.append(((~right)[:, None] & (u > idx[:, None]) & (u < start_right[:, None])).astype(np.float32))
        same = (idx[:, None] >> (lev + 1)) == (idx[None, :] >> (lev + 1))
        pm.append((same & right[:, None] & (~right)[None, :]).astype(np.float32))
    pm.append(np.eye(L, dtype=np.float32))
    sums = np.concatenate([tri] + [a + b for a, b in zip(mq, mk)], axis=0)
    return jnp.asarray(sums, BF16), jnp.asarray(np.stack(pm), F32), jnp.asarray(tri, BF16), jnp.asarray(tri.T, BF16)


def _head_mask(width, heads, h):
    lane = lax.broadcasted_iota(jnp.int32, (1, width), 1)
    per = width // heads
    return ((lane >= h * per) & (lane < (h + 1) * per)).astype(F32)


def _gla_scan_kernel(gq_ref, gk_ref, gv_ref, gg_ref, sm_ref, wa_ref, ba_ref, sums_ref, pm_ref, gn_ref,
                     o_ref, st_out, st_ref, *, L, nl):
    step = pl.program_id(0)

    @pl.when(step == 0)
    def _():
        st_ref[...] = jnp.zeros(st_ref.shape, F32)

    width = GLA_HEADS * GLA_DK
    x = _bdot(sm_ref[...], wa_ref[...]) + ba_ref[...]
    la = _log_sigmoid(x) * (1.0 / GLA_GATE_TEMP)
    dsum = _dot01(sums_ref[...], la)
    b = dsum[0:L]
    b_last = b[L - 1:L, :]
    q = gq_ref[...] * (GLA_DK ** -0.5)
    k = gk_ref[...]
    q_in = (q * jnp.exp(b)).astype(BF16)
    k_out = k * jnp.exp(b_last - b)
    decay_all = jnp.exp(b_last)
    hms = [_head_mask(width, GLA_HEADS, h) for h in range(GLA_HEADS)]
    atts = [jnp.zeros((L, L), F32) for _ in range(GLA_HEADS)]
    for lev in range(nl + 1):
        fac = jnp.exp(dsum[(1 + lev) * L:(2 + lev) * L]) if lev < nl else None
        ql = q if fac is None else q * fac
        kl = (k if fac is None else k * fac).astype(BF16)
        stacked = jnp.concatenate([ql * hm for hm in hms], axis=0).astype(BF16)
        sc = lax.dot_general(stacked, kl, (((1,), (1,)), ((), ())), preferred_element_type=F32)
        keep = pm_ref[lev] > 0.0
        for h in range(GLA_HEADS):
            atts[h] = atts[h] + jnp.where(keep, sc[h * L:(h + 1) * L], 0.0)
    for h in range(GLA_HEADS):
        hm = hms[h]
        att = atts[h]
        vh = gv_ref[:, h * GLA_DV:(h + 1) * GLA_DV]
        st = st_ref[h]
        o = (jnp.dot(att.astype(BF16), vh.astype(BF16), preferred_element_type=F32)
             + lax.dot_general(q_in, st.astype(BF16), (((1,), (1,)), ((), ())), preferred_element_type=F32))
        st_ref[h] = st * decay_all + jnp.dot(vh.T.astype(BF16), (k_out * hm).astype(BF16),
                                             preferred_element_type=F32)
        gg = gg_ref[:, h * GLA_DV:(h + 1) * GLA_DV]
        y = _rms(o) * gn_ref[:, h * GLA_DV:(h + 1) * GLA_DV] * (gg * _sigmoid(gg))
        o_ref[:, h * GLA_DV:(h + 1) * GLA_DV] = y.astype(BF16)

    @pl.when(step == pl.num_programs(0) - 1)
    def _():
        st_out[...] = st_ref[...]


def gla_scan(z, wa, ba, gn, consts, L):
    t = z.shape[0]
    sums, pm, _, _ = consts
    nl = pm.shape[0] - 1
    width = GLA_HEADS * GLA_DK
    vw = GLA_HEADS * GLA_DV
    row = lambda w, off: pl.BlockSpec((L, w), lambda i, o=off // w: (i, o))
    c2 = lambda a: pl.BlockSpec(a.shape, lambda i: (0, 0))
    c3 = lambda a: pl.BlockSpec(a.shape, lambda i: (0, 0, 0))
    return pl.pallas_call(
        functools.partial(_gla_scan_kernel, L=L, nl=nl),
        out_shape=(jax.ShapeDtypeStruct((t, vw), BF16),
                   jax.ShapeDtypeStruct((GLA_HEADS, GLA_DV, width), F32)),
        grid=(t // L,),
        in_specs=[row(width, Z_GQ), row(width, Z_GK), row(vw, Z_GV), row(vw, Z_GG), row(LANES, Z_SMALL),
                  c2(wa), c2(ba), c2(sums), c3(pm), c2(gn)],
        out_specs=(pl.BlockSpec((L, vw), lambda i: (i, 0)),
                   pl.BlockSpec((GLA_HEADS, GLA_DV, width), lambda i: (0, 0, 0))),
        scratch_shapes=[pltpu.VMEM((GLA_HEADS, GLA_DV, width), F32)],
        compiler_params=_cp("arbitrary"),
        name="gla_scan",
    )(z, z, z, z, z, wa, ba, sums, pm, gn)


def _mlstm_scan_kernel(mq_ref, mk_ref, mv_ref, mo_ref, sm_ref, smt_ref, brow_ref, bcol_ref, tri_ref, trit_ref,
                       nrm_ref, o_ref, ct_out, n_out, m_out, ct_ref, n_ref, m_ref, *, L):
    step = pl.program_id(0)

    @pl.when(step == 0)
    def _():
        ct_ref[...] = jnp.zeros(ct_ref.shape, F32)
        n_ref[...] = jnp.zeros(n_ref.shape, F32)
        m_ref[...] = jnp.zeros(m_ref.shape, F32)

    width = ML_HEADS * ML_DK
    pre_c = sm_ref[...] + brow_ref[...]
    pre_r = smt_ref[...] + bcol_ref[...]
    b_c_all = _dot01(tri_ref[...], _log_sigmoid(pre_c))
    b_r_all = _dot01_r(_log_sigmoid(pre_r), trit_ref[...])
    q = mq_ref[...]
    k = mk_ref[...] * (ML_DK ** -0.5)
    kb = k.astype(BF16)
    row = lax.broadcasted_iota(jnp.int32, (L, L), 0)
    col = lax.broadcasted_iota(jnp.int32, (L, L), 1)
    causal = col <= row
    for h in range(ML_HEADS):
        hm = _head_mask(width, ML_HEADS, h)
        bc = b_c_all[:, SM_MF + h:SM_MF + h + 1]
        br = b_r_all[ML_HEADS + h:ML_HEADS + h + 1, :]
        ic = pre_c[:, SM_MI + h:SM_MI + h + 1]
        ir = pre_r[h:h + 1, :]
        m_prev = m_ref[h:h + 1, 0:1]
        dlog = jnp.where(causal, bc - br + ir, NEG)
        inter = bc + m_prev
        mt = jnp.maximum(inter, jnp.max(dlog, axis=1, keepdims=True))
        qh = (q * hm).astype(BF16)
        qk = lax.dot_general(qh, kb, (((1,), (1,)), ((), ())), preferred_element_type=F32)
        wi = jnp.exp(dlog - mt) * qk
        wint = jnp.exp(inter - mt)
        vh = mv_ref[:, h * ML_DV:(h + 1) * ML_DV]
        ct = ct_ref[h]
        nrow = n_ref[h:h + 1, :]
        num = (jnp.dot(wi.astype(BF16), vh.astype(BF16), preferred_element_type=F32)
               + wint * lax.dot_general(qh, ct.astype(BF16), (((1,), (1,)), ((), ())),
                                        preferred_element_type=F32))
        nq = jnp.sum(wi, axis=1, keepdims=True) + wint * jnp.sum(q * nrow, axis=1, keepdims=True)
        hh = num / jnp.maximum(jnp.abs(nq), jnp.exp(-mt))
        m_new = mt[L - 1:L, :]
        b_last = bc[L - 1:L, :]
        keep = jnp.exp(b_last + m_prev - m_new)
        wn = jnp.exp(b_last - bc + ic - m_new)
        kw = k * hm * wn
        ct_ref[h] = keep * ct + jnp.dot(vh.T.astype(BF16), kw.astype(BF16), preferred_element_type=F32)
        n_ref[h:h + 1, :] = keep * nrow + jnp.sum(kw, axis=0, keepdims=True)
        m_ref[h:h + 1, :] = jnp.broadcast_to(m_new, (1, LANES))
        mo = mo_ref[:, h * ML_DV:(h + 1) * ML_DV]
        y = _rms(hh) * nrm_ref[:, h * ML_DV:(h + 1) * ML_DV] * _sigmoid(mo)
        o_ref[:, h * ML_DV:(h + 1) * ML_DV] = y.astype(BF16)

    @pl.when(step == pl.num_programs(0) - 1)
    def _():
        ct_out[...] = ct_ref[...]
        n_out[...] = n_ref[...]
        m_out[...] = m_ref[...]


def mlstm_scan(z, smt, brow, bcol, nrm, consts, L):
    t = z.shape[0]
    _, _, tri, trit = consts
    width = ML_HEADS * ML_DK
    vw = ML_HEADS * ML_DV
    row = lambda w, off: pl.BlockSpec((L, w), lambda i, o=off // w: (i, o))
    c2 = lambda a: pl.BlockSpec(a.shape, lambda i: (0, 0))
    st_shapes = [((ML_HEADS, ML_DV, width), lambda i: (0, 0, 0)), ((SUBLANES, width), lambda i: (0, 0)),
                 ((SUBLANES, LANES), lambda i: (0, 0))]
    return pl.pallas_call(
        functools.partial(_mlstm_scan_kernel, L=L),
        out_shape=(jax.ShapeDtypeStruct((t, vw), BF16),) + tuple(
            jax.ShapeDtypeStruct(s, F32) for s, _ in st_shapes),
        grid=(t // L,),
        in_specs=[row(width, Z_MQ), row(width, Z_MK), row(vw, Z_MV), row(vw, Z_MO), row(LANES, Z_SMALL),
                  pl.BlockSpec((SUBLANES, L), lambda i: (SM_MI // SUBLANES, i)),
                  c2(brow), c2(bcol), c2(tri), c2(trit), c2(nrm)],
        out_specs=(pl.BlockSpec((L, vw), lambda i: (i, 0)),) + tuple(pl.BlockSpec(s, f) for s, f in st_shapes),
        scratch_shapes=[pltpu.VMEM(s, F32) for s, _ in st_shapes],
        compiler_params=_cp("arbitrary"),
        name="mlstm_scan",
    )(z, z, z, z, z, smt, brow, bcol, tri, trit, nrm)


def _rec_step_kernel(gq_ref, gk_ref, gv_ref, gg_ref, mq_ref, mk_ref, mv_ref, mo_ref, sm_ref,
                     gqt_ref, gkt_ref, mqt_ref, mkt_ref, smt_ref,
                     wa_ref, ba_ref, wat_ref, bat_ref, brow_ref, gn_ref, mn_ref,
                     s_ref, c_ref, n_ref, m_ref,
                     bo_ref, co_ref, s_out, c_out, n_out, m_out, *, rows):
    blk = pl.program_id(0)
    nb = gqt_ref.shape[1]
    gw = GLA_HEADS * GLA_DK
    mw = ML_HEADS * ML_DK

    pre = sm_ref[...] + brow_ref[...]
    i_v = pre
    f_v = _log_sigmoid(pltpu.roll(pre, LANES - (SM_MF - SM_MI), 1))
    m_prev = m_ref[...]
    mt = jnp.maximum(f_v + m_prev, i_v)
    w_intra = jnp.exp(i_v - mt)
    w_inter = jnp.exp(f_v + m_prev - mt)
    inv_floor = jnp.exp(-mt)
    m_out[...] = mt
    mq = mq_ref[...]
    mk = mk_ref[...] * (ML_DK ** -0.5)
    n_old = n_ref[...]
    gq = gq_ref[...] * (GLA_DK ** -0.5)
    gk = gk_ref[...]
    qk_g, qk_m, qn_m, den = [], [], [], []
    keep_b = jnp.zeros((rows, mw), F32)
    wnew_b = jnp.zeros((rows, mw), F32)
    for h in range(ML_HEADS):
        hm = _head_mask(mw, ML_HEADS, h)
        qk_m.append(jnp.sum(mq * mk * hm, axis=1, keepdims=True))
        qn_m.append(jnp.sum(mq * n_old * hm, axis=1, keepdims=True))
        wi_h = w_intra[:, SM_MI + h:SM_MI + h + 1]
        we_h = w_inter[:, SM_MI + h:SM_MI + h + 1]
        nq = wi_h * qk_m[h] + we_h * qn_m[h]
        den.append(jnp.maximum(jnp.abs(nq), inv_floor[:, SM_MI + h:SM_MI + h + 1]))
        keep_b = keep_b + hm * we_h
        wnew_b = wnew_b + hm * wi_h
    n_out[...] = keep_b * n_old + wnew_b * mk
    for h in range(GLA_HEADS):
        qk_g.append(jnp.sum(gq * gk * _head_mask(gw, GLA_HEADS, h), axis=1, keepdims=True))

    xt = _bdot(wat_ref[...], smt_ref[...]) + bat_ref[...]
    at = jnp.exp(_log_sigmoid(xt) * (1.0 / GLA_GATE_TEMP))
    seq_lane = lax.broadcasted_iota(jnp.int32, (nb, LANES), 0)

    for r in range(rows):
        onehot = (seq_lane == blk * rows + r).astype(BF16)
        a_bc = _dot01_r(at, onehot)
        gk_bc = _dot01_r(gkt_ref[...], onehot)
        gq_bc = _dot01_r(gqt_ref[...], onehot) * (GLA_DK ** -0.5)
        mk_bc = _dot01_r(mkt_ref[...], onehot) * (ML_DK ** -0.5)
        mq_bc = _dot01_r(mqt_ref[...], onehot)
        for h in range(GLA_HEADS):
            sl = slice(h * GLA_DK, (h + 1) * GLA_DK)
            v_row = gv_ref[r:r + 1, h * GLA_DV:(h + 1) * GLA_DV]
            s_new = a_bc[sl] * s_ref[r, h] + gk_bc[sl] * v_row
            s_out[r, h] = s_new
            o = jnp.sum(gq_bc[sl] * s_new, axis=0, keepdims=True)
            gg = gg_ref[r:r + 1, h * GLA_DV:(h + 1) * GLA_DV]
            y = _rms(o) * gn_ref[:, h * GLA_DV:(h + 1) * GLA_DV] * (gg * _sigmoid(gg))
            bo_ref[r:r + 1, h * GLA_DV:(h + 1) * GLA_DV] = y.astype(BF16)
        for h in range(ML_HEADS):
            sl = slice(h * ML_DK, (h + 1) * ML_DK)
            v_row = mv_ref[r:r + 1, h * ML_DV:(h + 1) * ML_DV]
            c_old = c_ref[r, h]
            wi_s = w_intra[r:r + 1, SM_MI + h:SM_MI + h + 1]
            we_s = w_inter[r:r + 1, SM_MI + h:SM_MI + h + 1]
            num = (wi_s * qk_m[h][r:r + 1, :]) * v_row + we_s * jnp.sum(mq_bc[sl] * c_old, axis=0, keepdims=True)
            hh = num / den[h][r:r + 1, :]
            c_out[r, h] = we_s * c_old + wi_s * (mk_bc[sl] * v_row)
            mo = mo_ref[r:r + 1, h * ML_DV:(h + 1) * ML_DV]
            y = _rms(hh) * mn_ref[:, h * ML_DV:(h + 1) * ML_DV] * _sigmoid(mo)
            co_ref[r:r + 1, h * ML_DV:(h + 1) * ML_DV] = y.astype(BF16)


def rec_step(z, zt, wa, ba, wat, bat, brow, gn, mn, s0, c0, layer, n0, m0):
    nb = z.shape[0]
    rows = SUBLANES
    gw = GLA_HEADS * GLA_DK
    mw = ML_HEADS * ML_DK
    gvw = GLA_HEADS * GLA_DV
    mvw = ML_HEADS * ML_DV
    row = lambda w, off: pl.BlockSpec((rows, w), lambda i, o=off // w: (i, o))
    colb = lambda w, off: pl.BlockSpec((w, nb), lambda i, o=off // w: (o, 0))
    c2 = lambda a: pl.BlockSpec(a.shape, lambda i: (0, 0))
    st4 = lambda hds, dk, dv: pl.BlockSpec((rows, hds, dk, dv), lambda i: (i, 0, 0, 0))
    st5 = lambda hds, dk, dv: pl.BlockSpec((None, rows, hds, dk, dv), lambda i: (layer, i, 0, 0, 0))
    return pl.pallas_call(
        functools.partial(_rec_step_kernel, rows=rows),
        out_shape=(jax.ShapeDtypeStruct((nb, gvw), BF16), jax.ShapeDtypeStruct((nb, mvw), BF16),
                   jax.ShapeDtypeStruct(s0.shape[1:], F32), jax.ShapeDtypeStruct(c0.shape[1:], F32),
                   jax.ShapeDtypeStruct((nb, mw), F32), jax.ShapeDtypeStruct((nb, LANES), F32)),
        grid=(nb // rows,),
        in_specs=[row(gw, Z_GQ), row(gw, Z_GK), row(gvw, Z_GV), row(gvw, Z_GG),
                  row(mw, Z_MQ), row(mw, Z_MK), row(mvw, Z_MV), row(mvw, Z_MO), row(LANES, Z_SMALL),
                  colb(gw, Z_GQ), colb(gw, Z_GK), colb(mw, Z_MQ), colb(mw, Z_MK), colb(LANES, Z_SMALL),
                  c2(wa), c2(ba), c2(wat), c2(bat), c2(brow), c2(gn), c2(mn),
                  st5(GLA_HEADS, GLA_DK, GLA_DV), st5(ML_HEADS, ML_DK, ML_DV),
                  pl.BlockSpec((rows, mw), lambda i: (i, 0)), pl.BlockSpec((rows, LANES), lambda i: (i, 0))],
        out_specs=(pl.BlockSpec((rows, gvw), lambda i: (i, 0)), pl.BlockSpec((rows, mvw), lambda i: (i, 0)),
                   st4(GLA_HEADS, GLA_DK, GLA_DV), st4(ML_HEADS, ML_DK, ML_DV),
                   pl.BlockSpec((rows, mw), lambda i: (i, 0)), pl.BlockSpec((rows, LANES), lambda i: (i, 0))),
        compiler_params=_cp("arbitrary"),
        name="rec_step",
    )(z, z, z, z, z, z, z, z, z, zt, zt, zt, zt, zt, wa, ba, wat, bat, brow, gn, mn, s0, c0, n0, m0)


def _mla_prep_s_kernel(qc_ref, kvc_ref, sm_ref, gq_ref, gkv_ref, wq_ref, wukt_ref,
                       cq_ref, sqa_ref, sqb_ref, ck_ref, ska_ref, skb_ref,
                       ql_out, qp_out, lat_out, kr_out):
    qn = (_rms(qc_ref[...]) * gq_ref[...]).astype(BF16)
    lat_out[...] = _rms(kvc_ref[...]) * gkv_ref[...]
    krf = _rope_lanes(sm_ref[...], ck_ref[...], ska_ref[...], skb_ref[...])
    kr_out[...] = krf[:, :MLA_ROPE]
    cq, sqa, sqb = cq_ref[...], sqa_ref[...], sqb_ref[...]
    for h in range(MLA_HEADS):
        qh = _rope_lanes(jnp.dot(qn, wq_ref[h], preferred_element_type=F32), cq, sqa, sqb)
        ql_out[h] = _bdot(qh, wukt_ref[h])
        qp_out[h] = pltpu.roll(qh, LANES - MLA_NOPE, 1)


def mla_prep_s(z, gq, gkv, wq, wukt, tabs):
    nb = z.shape[0]
    hh = MLA_HEADS
    row = lambda w, off: pl.BlockSpec((nb, w), lambda i, o=off // w: (0, o))
    c2 = lambda a: pl.BlockSpec(a.shape, lambda i: (0, 0))
    c3 = lambda a: pl.BlockSpec(a.shape, lambda i: (0, 0, 0))
    return pl.pallas_call(
        _mla_prep_s_kernel,
        out_shape=(jax.ShapeDtypeStruct((hh, nb, LANES), F32), jax.ShapeDtypeStruct((hh, nb, LANES), F32),
                   jax.ShapeDtypeStruct((nb, KV_LORA), F32), jax.ShapeDtypeStruct((nb, MLA_ROPE), F32)),
        grid=(1,),
        in_specs=[row(Q_LORA, Z_QC), row(KV_LORA, Z_KVC), row(LANES, Z_SMALL),
                  c2(gq), c2(gkv), c3(wq), c3(wukt)] + [c2(t) for t in tabs],
        out_specs=(pl.BlockSpec((hh, nb, LANES), lambda i: (0, 0, 0)),
                   pl.BlockSpec((hh, nb, LANES), lambda i: (0, 0, 0)),
                   pl.BlockSpec((nb, KV_LORA), lambda i: (0, 0)), pl.BlockSpec((nb, MLA_ROPE), lambda i: (0, 0))),
        compiler_params=_cp("arbitrary"),
        name="mla_prep_sample",
    )(z, z, z, gq, gkv, wq, wukt, *tabs)


def _mla_decode_kernel(pt_ref, ql_ref, qp_ref, ln_ref, kn_ref, lat_hbm, kr_hbm, o_ref, lat_buf, kr_buf, sem,
                       *, layer, n_pages):
    b = pl.program_id(0)
    slot = lax.rem(b, 2)

    def page_copies(seq, sl, p):
        page = pt_ref[seq, p]
        off = pl.multiple_of(p * PAGE_SIZE, PAGE_SIZE)
        return (pltpu.make_async_copy(lat_hbm.at[layer, page], lat_buf.at[sl, pl.ds(off, PAGE_SIZE), :],
                                      sem.at[0, sl]),
                pltpu.make_async_copy(kr_hbm.at[layer, page], kr_buf.at[sl, :, pl.ds(off, PAGE_SIZE)],
                                      sem.at[1, sl]))

    def start_all(seq, sl):
        def body(p, carry):
            for c in page_copies(seq, sl, p):
                c.start()
            return carry
        lax.fori_loop(0, n_pages, body, 0, unroll=4)

    def wait_all(seq, sl):
        def body(p, carry):
            for c in page_copies(seq, sl, p):
                c.wait()
            return carry
        lax.fori_loop(0, n_pages, body, 0, unroll=4)

    @pl.when(b == 0)
    def _():
        start_all(0, 0)

    @pl.when(b + 1 < pl.num_programs(0))
    def _():
        start_all(b + 1, 1 - slot)

    wait_all(b, slot)

    ql = ql_ref[0]
    qp = qp_ref[0][:, :MLA_ROPE]
    lat = lat_buf[slot]
    latb = lat.astype(BF16)
    s = (lax.dot_general(ql.astype(BF16), latb, (((1,), (1,)), ((), ())), preferred_element_type=F32)
         + jnp.dot(qp.astype(BF16), kr_buf[slot].astype(BF16), preferred_element_type=F32)) * MLA_SCALE
    lat_new = ln_ref[0]
    s_new = (jnp.sum(ql * lat_new, axis=1, keepdims=True)
             + jnp.sum(qp * kn_ref[0], axis=1, keepdims=True)) * MLA_SCALE
    m = jnp.maximum(jnp.max(s, axis=1, keepdims=True), s_new)
    p = jnp.exp(s - m)
    pn = jnp.exp(s_new - m)
    num = jnp.dot(p.astype(BF16), latb, preferred_element_type=F32) + pn * lat_new
    o_ref[0] = num / (jnp.sum(p, axis=1, keepdims=True) + pn)


def mla_decode(page_table, ql, qp, lat_new, kr_new, cache_lat, cache_rope_t, layer):
    nb, n_pages = page_table.shape
    past = n_pages * PAGE_SIZE
    hh = MLA_HEADS
    seq = lambda w: pl.BlockSpec((1, hh, w), lambda b, pt: (b, 0, 0))
    new = lambda w: pl.BlockSpec((1, 1, w), lambda b, pt: (b, 0, 0))
    return pl.pallas_call(
        functools.partial(_mla_decode_kernel, layer=layer, n_pages=n_pages),
        out_shape=jax.ShapeDtypeStruct((nb, hh, KV_LORA), F32),
        grid_spec=pltpu.PrefetchScalarGridSpec(
            num_scalar_prefetch=1,
            grid=(nb,),
            in_specs=[seq(LANES), seq(LANES), new(KV_LORA), new(MLA_ROPE),
                      pl.BlockSpec(memory_space=pl.ANY), pl.BlockSpec(memory_space=pl.ANY)],
            out_specs=pl.BlockSpec((1, hh, KV_LORA), lambda b, pt: (b, 0, 0)),
            scratch_shapes=[pltpu.VMEM((2, past, KV_LORA), F32), pltpu.VMEM((2, MLA_ROPE, past), F32),
                            pltpu.SemaphoreType.DMA((2, 2))]),
        compiler_params=_cp("arbitrary"),
        name="mla_decode",
    )(page_table, ql, qp, lat_new, kr_new, cache_lat, cache_rope_t)


def _uv_proj_kernel(o_ref, w_ref, a_ref):
    a_ref[...] = _bdot(o_ref[...], w_ref[0]).astype(BF16)


def uv_proj(o_lat, wuv):
    nb = o_lat.shape[0]
    return pl.pallas_call(
        _uv_proj_kernel,
        out_shape=jax.ShapeDtypeStruct((nb, MLA_HEADS * LANES), BF16),
        grid=(MLA_HEADS,),
        in_specs=[pl.BlockSpec((nb, KV_LORA), lambda h: (0, h)),
                  pl.BlockSpec((1, KV_LORA, LANES), lambda h: (h, 0, 0))],
        out_specs=pl.BlockSpec((nb, LANES), lambda h: (0, h)),
        compiler_params=_cp("arbitrary"),
        name="uv_proj",
    )(o_lat, wuv)


def _merge_kernel(a_ref, b_ref, c_ref, ga_ref, gb_ref, gc_ref, x_ref, g1_ref, wa_ref, wb_ref, wc_ref, wo_ref,
                  nf_ref, sc_ref, sh_ref, x1_ref, h2_ref):
    merged = (_sigmoid(ga_ref[...]) * jnp.dot(a_ref[...], wa_ref[...], preferred_element_type=F32)
              + _sigmoid(gb_ref[...]) * jnp.dot(b_ref[...], wb_ref[...], preferred_element_type=F32)
              + _sigmoid(gc_ref[...]) * jnp.dot(c_ref[...], wc_ref[...], preferred_element_type=F32))
    mix = _bdot(merged, wo_ref[...])
    x1 = x_ref[...] + g1_ref[...] * mix
    x1_ref[...] = x1
    h2 = _rms(x1) * nf_ref[...] * (1.0 + sc_ref[...]) + sh_ref[...]
    h2_ref[...] = h2.astype(BF16)


def merge(a, b, c, z, x, gate1, wa, wb, wc, wo, nf, scale2, shift2, tm):
    t, d = x.shape
    per_tok = gate1.shape[0] != 1
    mod_spec = (pl.BlockSpec((tm, d), lambda i: (i, 0)) if per_tok else pl.BlockSpec((1, d), lambda i: (0, 0)))
    rowb = lambda w, o=0: pl.BlockSpec((tm, w), lambda i, o=o: (i, o))
    c2 = lambda arr: pl.BlockSpec(arr.shape, lambda i: (0, 0))
    return pl.pallas_call(
        _merge_kernel,
        out_shape=(jax.ShapeDtypeStruct((t, d), F32), jax.ShapeDtypeStruct((t, d), BF16)),
        grid=(t // tm,),
        in_specs=[rowb(a.shape[1]), rowb(b.shape[1]), rowb(c.shape[1]),
                  rowb(d, Z_GA // d), rowb(d, Z_GB // d), rowb(d, Z_GC // d), rowb(d), mod_spec,
                  c2(wa), c2(wb), c2(wc), c2(wo), pl.BlockSpec((1, d), lambda i: (0, 0)), mod_spec, mod_spec],
        out_specs=(rowb(d), rowb(d)),
        compiler_params=_cp("arbitrary"),
        name="merge",
    )(a, b, c, z, z, z, x, gate1, wa, wb, wc, wo, nf.reshape(1, d), scale2, shift2)


def _oddeven_merge(lo, hi, r):
    step = r * 2
    if step < hi - lo:
        yield from _oddeven_merge(lo, hi, step)
        yield from _oddeven_merge(lo + r, hi, step)
        yield from [(i, i + r) for i in range(lo + r, hi - r, step)]
    else:
        yield (lo, lo + r)


def _oddeven_merge_sort(lo, hi):
    if hi - lo >= 1:
        mid = lo + (hi - lo) // 2
        yield from _oddeven_merge_sort(lo, mid)
        yield from _oddeven_merge_sort(mid + 1, hi)
        yield from _oddeven_merge(lo, hi, 1)


def _top_rows(x, n):
    groups = x.shape[0] // SUBLANES
    width = 1 << (groups - 1).bit_length()
    cols = [x[SUBLANES * r:SUBLANES * (r + 1)] for r in range(groups)]
    cols += [jnp.full_like(cols[0], -jnp.inf)] * (width - groups)
    for i, j in _oddeven_merge_sort(0, width - 1):
        cols[i], cols[j] = jnp.maximum(cols[i], cols[j]), jnp.minimum(cols[i], cols[j])
    rows = []
    for k in range(n):
        head = cols[0]
        m = jnp.max(head, axis=0, keepdims=True)
        rows.append(m)
        hit = head == m
        depth = min(width, n - k)
        for r in range(depth - 1):
            cols[r] = jnp.where(hit, cols[r + 1], cols[r])
        cols[depth - 1] = jnp.where(hit, -jnp.inf, cols[depth - 1])
    return rows


def _peer_topk_kernel(h2_ref, wq_ref, k1_ref, k2_ref, thr_out, s2_out, l1_out):
    h2 = h2_ref[...]
    k1 = PEER_TOPK + 1
    for h in range(PEER_HEADS):
        qh = jnp.dot(h2, wq_ref[h], preferred_element_type=F32)
        s1 = _bdot_nt(k1_ref[h], qh[:, :PEER_HALF])
        s2 = _bdot_nt(k2_ref[h], qh[:, PEER_HALF:])
        a = _top_rows(s1, k1)
        b = _top_rows(s2, k1)
        n_rows = -(-k1 // SUBLANES) * SUBLANES
        ninf = jnp.full_like(b[0], -jnp.inf)
        split = 4
        b_all = jnp.concatenate(b + [ninf] * (n_rows - k1), axis=0)
        a_tail = jnp.concatenate([ninf] * split + a[split:] + [ninf] * (n_rows - k1), axis=0)
        span = lambda r: -(-(k1 // (r + 1)) // SUBLANES) * SUBLANES
        cand = [a[p] + b_all[:span(p)] for p in range(split)]
        cand += [b[q] + a_tail[:span(q)] for q in range(k1 // (split + 1))]
        best = _top_rows(jnp.concatenate(cand, axis=0), k1)
        zsum = jnp.zeros_like(best[0])
        for c in best[:PEER_TOPK]:
            zsum = zsum + jnp.exp(c - best[0])
        mid = 0.5 * (best[PEER_TOPK - 1] + best[PEER_TOPK])
        tabs = ((thr_out, ((mid - s1) - b[0]) * LOG2E), (s2_out, (s2 - b[0]) * LOG2E),
                (l1_out, (s1 - a[0]) * LOG2E - jnp.log(zsum) * LOG2E))
        for ref, val in tabs:
            for c in range(val.shape[1] // LANES):
                ref[h, c] = val[:, c * LANES:(c + 1) * LANES]


def peer_topk(h2, wq, k1, k2, tm):
    t, d = h2.shape
    hh = PEER_HEADS
    c3 = lambda a: pl.BlockSpec(a.shape, lambda i: (0, 0, 0))
    assert tm % LANES == 0
    tab = jax.ShapeDtypeStruct((hh, t // LANES, PEER_KEYS, LANES), F32)
    tab_spec = pl.BlockSpec((hh, tm // LANES, PEER_KEYS, LANES), lambda i: (0, i, 0, 0))
    return pl.pallas_call(
        _peer_topk_kernel,
        out_shape=(tab, tab, tab),
        grid=(t // tm,),
        in_specs=[pl.BlockSpec((tm, d), lambda i: (i, 0)), c3(wq), c3(k1), c3(k2)],
        out_specs=(tab_spec, tab_spec, tab_spec),
        compiler_params=_cp("arbitrary"),
        name="peer_topk",
    )(h2, wq, k1, k2)


def _peer_dense_kernel(h2_ref, u_ref, vt_ref, thr_ref, l1_ref, s2_ref, x1_ref, g2_ref, gf_ref,
                       o_ref, acc_ref, act_ref, wg_ref, *, ni, final):
    e = pl.program_id(1)

    @pl.when(e == 0)
    def _():
        acc_ref[...] = jnp.zeros(acc_ref.shape, F32)

    act_ref[...] = _gelu_tanh(lax.dot_general(u_ref[...], h2_ref[...], (((1,), (1,)), ((), ())),
                                              preferred_element_type=F32))
    tm = act_ref.shape[1]
    for c in range(tm // LANES):
        cs = slice(c * LANES, (c + 1) * LANES)
        for ii in range(ni):
            rs = slice(ii * PEER_KEYS, (ii + 1) * PEER_KEYS)
            w = None
            for h in range(PEER_HEADS):
                s2 = s2_ref[h, c]
                part = jnp.where(s2 >= thr_ref[h, c, ii:ii + 1, :],
                                 jnp.exp2(s2 + l1_ref[h, c, ii:ii + 1, :]), 0.0)
                w = part if w is None else w + part
            wg_ref[rs, cs] = (w * act_ref[rs, cs]).astype(BF16)
    acc_ref[...] += jnp.dot(vt_ref[...], wg_ref[...], preferred_element_type=F32)

    @pl.when(e == pl.num_programs(1) - 1)
    def _():
        out = x1_ref[...] + g2_ref[...] * acc_ref[...].T
        o_ref[...] = _rms(out) * gf_ref[...] if final else out


def peer_dense(h2, u, vt, layer, thr, l1, s2, x1, gate2, g_final, final, tm, ni):
    t, d = h2.shape
    n_exp = u.shape[1]
    te = ni * PEER_KEYS
    hh = PEER_HEADS
    per_tok = gate2.shape[0] != 1
    mod_spec = (pl.BlockSpec((tm, d), lambda i, e: (i, 0)) if per_tok
                else pl.BlockSpec((1, d), lambda i, e: (0, 0)))
    nc = tm // LANES
    sub = pl.BlockSpec((hh, nc, ni, LANES), lambda i, e: (0, i, e, 0))
    full = pl.BlockSpec((hh, nc, PEER_KEYS, LANES), lambda i, e: (0, i, 0, 0))
    return pl.pallas_call(
        functools.partial(_peer_dense_kernel, ni=ni, final=final),
        out_shape=jax.ShapeDtypeStruct((t, d), F32),
        grid=(t // tm, n_exp // te),
        in_specs=[pl.BlockSpec((tm, d), lambda i, e: (i, 0)),
                  pl.BlockSpec((None, te, d), lambda i, e: (layer, e, 0)),
                  pl.BlockSpec((None, d, te), lambda i, e: (layer, 0, e)),
                  sub, sub, full,
                  pl.BlockSpec((tm, d), lambda i, e: (i, 0)), mod_spec,
                  pl.BlockSpec((1, d), lambda i, e: (0, 0))],
        out_specs=pl.BlockSpec((tm, d), lambda i, e: (i, 0)),
        scratch_shapes=[pltpu.VMEM((d, tm), F32), pltpu.VMEM((te, tm), F32), pltpu.VMEM((te, tm), BF16)],
        compiler_params=_cp("arbitrary", "arbitrary"),
        name="peer_dense",
    )(h2, u, vt, thr, l1, s2, x1, gate2, g_final.reshape(1, d))


def _pad_to(a, axis, size):
    pad = [(0, 0)] * a.ndim
    pad[axis] = (0, size - a.shape[axis])
    return jnp.pad(a, pad)


def _prep_w_in(w):
    sizes = (Q_LORA, KV_LORA, MLA_ROPE, 256, 256, 512, GLA_GATE_RANK, 512, 256, 256, 512, ML_HEADS, ML_HEADS, 512,
             D_MODEL, D_MODEL, D_MODEL)
    offs = np.concatenate([[0], np.cumsum(sizes)])
    (q_c, kv_c, k_pe, gq, gk, gv, ga, gg, mq, mk, mv, mi, mf, mo, g_a, g_b, g_c) = [
        w[:, offs[i]:offs[i + 1]] for i in range(len(sizes))]
    small = _pad_to(jnp.concatenate([k_pe, ga, mi, mf], axis=1), 1, LANES)
    out = jnp.concatenate([g_a, g_b, g_c, gv, gg, mv, mo, q_c, gq, gk, mq, mk, kv_c, small], axis=1)
    assert out.shape[1] == Z_WIDTH
    return out.astype(BF16)


def _rope_tables(pos):
    half = MLA_ROPE // 2
    inv = ROPE_BASE ** (-jnp.arange(half, dtype=F32) / half)
    ang = pos.astype(F32)[:, None] * inv[None, :]
    cos, sin = jnp.cos(ang), jnp.sin(ang)
    n = pos.shape[0]
    z = lambda w: jnp.zeros((n, w), F32)
    one = jnp.ones((n, MLA_NOPE), F32)
    rest = LANES - MLA_NOPE - MLA_ROPE
    cq = jnp.concatenate([one, cos, cos, z(rest)], axis=1)
    sqa = jnp.concatenate([z(MLA_NOPE), -sin, z(half), z(rest)], axis=1)
    sqb = jnp.concatenate([z(MLA_NOPE), z(half), sin, z(rest)], axis=1)
    ck = jnp.concatenate([cos, cos, z(LANES - MLA_ROPE)], axis=1)
    ska = jnp.concatenate([-sin, z(LANES - half)], axis=1)
    skb = jnp.concatenate([z(half), sin, z(LANES - MLA_ROPE)], axis=1)
    return (cq, sqa, sqb, ck, ska, skb), (cos.T, sin.T)


def _layer_weights(l, P):
    w = {}
    w['w_in'] = _prep_w_in(P['w_in'][l])
    w['gq'] = P['mla_q_norm'][l].reshape(1, Q_LORA)
    w['gkv'] = P['mla_kv_norm'][l].reshape(1, KV_LORA)
    w['wq'] = _pad_to(jnp.transpose(P['w_uq'][l], (1, 0, 2)), 2, LANES).astype(BF16)
    wuk = jnp.transpose(P['w_uk'][l], (1, 0, 2))
    w['wuk'] = _pad_to(wuk, 2, LANES).astype(BF16)
    w['wukt'] = _pad_to(jnp.transpose(wuk, (0, 2, 1)), 1, LANES).astype(BF16)
    w['wuv'] = _pad_to(jnp.transpose(P['w_uv'][l], (1, 0, 2)), 2, LANES).astype(BF16)
    w['wqt'] = _pad_to(jnp.transpose(P['w_uq'][l], (1, 2, 0)), 1, LANES).astype(BF16)
    w['wuvt'] = _pad_to(jnp.transpose(P['w_uv'][l], (1, 2, 0)), 1, LANES).astype(BF16)
    wa = jnp.zeros((LANES, GLA_HEADS * GLA_DK), F32).at[SM_GA:SM_GA + GLA_GATE_RANK].set(P['w_gla_a'][l])
    w['gla_wa'] = wa.astype(BF16)
    w['gla_wat'] = wa.T.astype(BF16)
    w['gla_ba'] = P['b_gla_a'][l].reshape(1, -1)
    w['gla_bat'] = P['b_gla_a'][l].reshape(-1, 1)
    w['gla_norm'] = P['gla_norm'][l].reshape(1, -1)
    w['ml_norm'] = P['ml_norm'][l].reshape(1, -1)
    brow = jnp.zeros((1, LANES), F32)
    brow = brow.at[0, SM_MI:SM_MI + ML_HEADS].set(P['ml_b_i'][l]).at[0, SM_MF:SM_MF + ML_HEADS].set(P['ml_b_f'][l])
    w['ml_brow'] = brow
    w['ml_bcol'] = jnp.concatenate([P['ml_b_i'][l], P['ml_b_f'][l]]).reshape(2 * ML_HEADS, 1)
    wpa = P['w_proj_a'][l].reshape(MLA_HEADS, MLA_V, D_MODEL)
    w['wpa'] = _pad_to(wpa, 1, LANES).reshape(MLA_HEADS * LANES, D_MODEL).astype(BF16)
    w['wpb'] = P['w_proj_b'][l].astype(BF16)
    w['wpc'] = P['w_proj_c'][l].astype(BF16)
    w['wo'] = P['w_o'][l].astype(BF16)
    w['peer_wq'] = jnp.transpose(P['peer_wq'][l], (1, 0, 2)).astype(BF16)
    w['peer_k1'] = P['peer_keys'][l][:, 0].astype(BF16)
    w['peer_k2'] = P['peer_keys'][l][:, 1].astype(BF16)
    w['layer'] = l
    w['peer_u'] = P['peer_u_b']
    w['peer_vt'] = P['peer_vt_b']
    return w


def _mods(c, w_ada, b_ada, layer):
    rows = c.shape[0]
    cp = _pad_to(c, 0, SUBLANES) if rows < SUBLANES else c
    mod = ada_mod(cp, w_ada, b_ada[layer], layer)[:rows]
    return jnp.split(mod, 6, axis=-1)


def _peer_block(h2, x1, gate2, w, g_final, final, tm, ni):
    thr, s2, l1 = peer_topk(h2, w['peer_wq'], w['peer_k1'], w['peer_k2'], min(tm, 2 * LANES))
    return peer_dense(h2, w['peer_u'], w['peer_vt'], w['layer'], thr, l1, s2, x1, gate2, g_final, final, tm, ni)


def _trunk_prompt(x, c, P, W):
    t = x.shape[0]
    tm = min(512, t)
    L = min(128, t)
    consts = _chunk_constants(L)
    tabs, tabs_t = _rope_tables(jnp.arange(t, dtype=jnp.int32))
    lat_rows, rope_rows, gla_st, c_st, n_st, m_st = [], [], [], [], [], []
    for l in range(len(W)):
        w = W[l]
        shift1, scale1, gate1, shift2, scale2, gate2 = _mods(c, P['w_ada'], P['b_ada'], l)
        z, zst = norm_proj(x, P['norm_mix'][l], scale1, shift1, w['w_in'], min(2 * tm, t))
        qt, k, vt, lat, kr = mla_prep(z, w['gq'], w['gkv'], w['wqt'], w['wuk'], w['wuvt'], tabs_t, tabs, tm)
        a = flash_attention(qt, k, vt, min(2 * tm, t))
        b, st = gla_scan(z, w['gla_wa'], w['gla_ba'], w['gla_norm'], consts, L)
        cc, ct, nrow, mrow = mlstm_scan(z, zst, w['ml_brow'], w['ml_bcol'], w['ml_norm'], consts, L)
        x1, h2 = merge(a, b, cc, z, x, gate1, w['wpa'], w['wpb'], w['wpc'], w['wo'],
                       P['norm_ffn'][l], scale2, shift2, tm)
        x = _peer_block(h2, x1, gate2, w, P['norm_final'], l == len(W) - 1, tm, PEER_TILE_ROWS)
        lat_rows.append(lat)
        rope_rows.append(kr)
        gla_st.append(jnp.stack([st[h][:, h * GLA_DK:(h + 1) * GLA_DK].T for h in range(GLA_HEADS)]))
        c_st.append(jnp.stack([ct[h][:, h * ML_DK:(h + 1) * ML_DK].T for h in range(ML_HEADS)]))
        n_st.append(jnp.stack([nrow[h, h * ML_DK:(h + 1) * ML_DK] for h in range(ML_HEADS)]))
        m_st.append(mrow[:ML_HEADS, 0])
    return x, tuple(jnp.stack(r) for r in (lat_rows, rope_rows, gla_st, c_st, n_st, m_st))


def _trunk_sample(x, c, P, W, cache_lat, cache_rope, page_table, s_gla, s_c, s_n, s_m):
    nb = x.shape[0]
    past = page_table.shape[1] * PAGE_SIZE
    tabs, _ = _rope_tables(jnp.full((1,), past, dtype=jnp.int32))
    cache_rope = jnp.swapaxes(cache_rope, 2, 3)
    lat_rows, rope_rows, gla_st, c_st, n_st, m_st = [], [], [], [], [], []
    for l in range(len(W)):
        w = W[l]
        shift1, scale1, gate1, shift2, scale2, gate2 = _mods(c, P['w_ada'], P['b_ada'], l)
        z, _ = norm_proj(x, P['norm_mix'][l], scale1, shift1, w['w_in'], nb)
        ql, qp, lat, kr = mla_prep_s(z, w['gq'], w['gkv'], w['wq'], w['wukt'], tabs)
        o_lat = mla_decode(page_table, jnp.transpose(ql, (1, 0, 2)), jnp.transpose(qp, (1, 0, 2)),
                           lat[:, None, :], kr[:, None, :], cache_lat, cache_rope, l)
        a = uv_proj(o_lat.reshape(nb, MLA_HEADS * KV_LORA), w['wuv'])
        m_in = jnp.zeros((nb, LANES), F32).at[:, SM_MI:SM_MI + ML_HEADS].set(s_m[l])
        b, cc, s_new, c_new, n_new, m_new = rec_step(
            z, z.T, w['gla_wa'], w['gla_ba'], w['gla_wat'], w['gla_bat'], w['ml_brow'], w['gla_norm'], w['ml_norm'],
            s_gla, s_c, l, s_n[l].reshape(nb, ML_HEADS * ML_DK), m_in)
        x1, h2 = merge(a, b, cc, z, x, gate1, w['wpa'], w['wpb'], w['wpc'], w['wo'],
                       P['norm_ffn'][l], scale2, shift2, nb)
        x = _peer_block(h2, x1, gate2, w, P['norm_final'], l == len(W) - 1, nb, PEER_TILE_ROWS)
        lat_rows.append(lat[:, None, :])
        rope_rows.append(kr[:, None, :])
        gla_st.append(s_new)
        c_st.append(c_new)
        n_st.append(n_new.reshape(nb, ML_HEADS, ML_DK))
        m_st.append(m_new[:, SM_MI:SM_MI + ML_HEADS])
    return x, tuple(jnp.stack(r) for r in (lat_rows, rope_rows, gla_st, c_st, n_st, m_st))


def kernel(x_prompt, x_sample, cache_kv_latent, cache_k_rope, state_gla, state_mlstm_C, state_mlstm_n, state_mlstm_m, page_table, c_prompt, c_sample, w_ada, b_ada, norm_mix, norm_ffn, norm_final, w_in, mla_q_norm, mla_kv_norm, w_uq, w_uk, w_uv, w_gla_a, b_gla_a, gla_norm, ml_b_i, ml_b_f, ml_norm, w_proj_a, w_proj_b, w_proj_c, w_o, peer_wq, peer_keys, peer_u, peer_v):
    P = {'w_ada': w_ada, 'b_ada': b_ada, 'norm_mix': norm_mix, 'norm_ffn': norm_ffn, 'norm_final': norm_final,
         'w_in': w_in, 'mla_q_norm': mla_q_norm, 'mla_kv_norm': mla_kv_norm, 'w_uq': w_uq, 'w_uk': w_uk,
         'w_uv': w_uv, 'w_gla_a': w_gla_a, 'b_gla_a': b_gla_a, 'gla_norm': gla_norm, 'ml_b_i': ml_b_i,
         'ml_b_f': ml_b_f, 'ml_norm': ml_norm, 'w_proj_a': w_proj_a, 'w_proj_b': w_proj_b,
         'w_proj_c': w_proj_c, 'w_o': w_o, 'peer_wq': peer_wq, 'peer_keys': peer_keys, 'peer_u': peer_u,
         'peer_v': peer_v}
    depth = w_in.shape[0]
    P['peer_u_b'] = peer_u.astype(BF16)
    P['peer_vt_b'] = jnp.swapaxes(peer_v, 1, 2).astype(BF16)
    W = [_layer_weights(l, P) for l in range(depth)]
    bp, sp, d = x_prompt.shape
    assert bp == 1 and x_sample.shape[1] == 1
    nb = x_sample.shape[0]
    y_p, (lat_p, rope_p, gla_p, c_p, n_p, m_p) = _trunk_prompt(x_prompt.reshape(sp, d), c_prompt, P, W)
    y_s, (lat_s, rope_s, gla_s, c_s, n_s, m_s) = _trunk_sample(
        x_sample.reshape(nb, d), c_sample, P, W, cache_kv_latent, cache_k_rope, page_table,
        state_gla, state_mlstm_C, state_mlstm_n, state_mlstm_m)
    return (y_p.reshape(bp, sp, d), y_s.reshape(nb, 1, d),
            lat_p[:, None], rope_p[:, None], gla_p[:, None], c_p[:, None], n_p[:, None], m_p[:, None],
            lat_s, rope_s, gla_s, c_s, n_s, m_s)
```

```python
import functools

import numpy as np
import jax
import jax.numpy as jnp
from jax import lax
from jax.experimental import pallas as pl
from jax.experimental.pallas import tpu as pltpu

F32 = jnp.float32
BF16 = jnp.bfloat16

D_MODEL = 1024
PAGE_SIZE = 128
MLA_HEADS = 8
MLA_NOPE = 64
MLA_ROPE = 32
MLA_V = 64
Q_LORA = 256
KV_LORA = 128
ROPE_BASE = 10000.0
MLA_SCALE = (MLA_NOPE + MLA_ROPE) ** -0.5
GLA_HEADS = 4
GLA_DK = 64
GLA_DV = 128
GLA_GATE_RANK = 16
GLA_GATE_TEMP = 16.0
ML_HEADS = 4
ML_DK = 64
ML_DV = 128
PEER_HEADS = 8
PEER_KEYS = 128
PEER_HALF = 128
PEER_TOPK = 16
NORM_EPS = 1e-6

LANES = 128
SUBLANES = 8
VMEM_LIMIT = 48 * 1024 * 1024

NEG = -1e30
LOG2E = 1.4426950408889634
PEER_TILE_ROWS = 16

Z_GA, Z_GB, Z_GC = 0, 1024, 2048
Z_GV, Z_GG, Z_MV, Z_MO = 3072, 3584, 4096, 4608
Z_QC, Z_GQ, Z_GK, Z_MQ, Z_MK = 5120, 5376, 5632, 5888, 6144
Z_KVC, Z_SMALL = 6400, 6528
Z_WIDTH = 6656
SM_KPE, SM_GA, SM_MI, SM_MF = 0, 32, 48, 52


def _cp(*sem):
    return pltpu.CompilerParams(dimension_semantics=tuple(sem), vmem_limit_bytes=VMEM_LIMIT)


def _bdot(a, b):
    return jnp.dot(a.astype(BF16), b.astype(BF16), preferred_element_type=F32)


def _bdot_nt(a, b):
    return lax.dot_general(a.astype(BF16), b.astype(BF16), (((1,), (1,)), ((), ())),
                           preferred_element_type=F32)


def _split3(x):
    hi = x.astype(BF16)
    r = x - hi.astype(F32)
    mid = r.astype(BF16)
    lo = (r - mid.astype(F32)).astype(BF16)
    return hi, mid, lo


def _dot01(m01, x):
    hi, mid, lo = _split3(x)
    return (jnp.dot(m01, hi, preferred_element_type=F32) + jnp.dot(m01, mid, preferred_element_type=F32)
            + jnp.dot(m01, lo, preferred_element_type=F32))


def _dot01_r(x, m01):
    hi, mid, lo = _split3(x)
    return (jnp.dot(hi, m01, preferred_element_type=F32) + jnp.dot(mid, m01, preferred_element_type=F32)
            + jnp.dot(lo, m01, preferred_element_type=F32))


def _sigmoid(x):
    return 1.0 / (1.0 + jnp.exp(-x))


def _log_sigmoid(x):
    return jnp.minimum(x, 0.0) - jnp.log(1.0 + jnp.exp(-jnp.abs(x)))


def _rms(x):
    return x * lax.rsqrt(jnp.mean(x * x, axis=-1, keepdims=True) + NORM_EPS)


def _gelu_tanh(x):
    return 0.5 * x * (1.0 + jnp.tanh(0.7978845608028654 * (x + 0.044715 * (x * x * x))))


def _ada_kernel(c_ref, w_ref, b_ref, o_ref):
    c = c_ref[...]
    o_ref[...] = _bdot(c * _sigmoid(c), w_ref[...]) + b_ref[...]


def ada_mod(c, w, b, layer):
    rows, d = c.shape
    n = w.shape[2]
    tn = 1536
    return pl.pallas_call(
        _ada_kernel,
        out_shape=jax.ShapeDtypeStruct((rows, n), F32),
        grid=(n // tn,),
        in_specs=[pl.BlockSpec((rows, d), lambda j: (0, 0)),
                  pl.BlockSpec((None, d, tn), lambda j: (layer, 0, j)),
                  pl.BlockSpec((1, tn), lambda j: (0, j))],
        out_specs=pl.BlockSpec((rows, tn), lambda j: (0, j)),
        compiler_params=_cp("arbitrary"),
        name="ada_mod",
    )(c, w, b.reshape(1, n))


def _norm_proj_kernel(x_ref, g_ref, sc_ref, sh_ref, w_ref, o_ref, st_ref, hn_ref, *, tn):
    @pl.when(pl.program_id(1) == 0)
    def _():
        h = _rms(x_ref[...]) * g_ref[...] * (1.0 + sc_ref[...]) + sh_ref[...]
        hn_ref[...] = h.astype(BF16)

    res = jnp.dot(hn_ref[...], w_ref[...], preferred_element_type=F32)
    o_ref[...] = res

    @pl.when(pl.program_id(1) == Z_SMALL // tn)
    def _():
        off = Z_SMALL % tn
        st_ref[...] = res[:, off:off + LANES].T


def norm_proj(x, g, scale, shift, w, tm):
    t, d = x.shape
    n = w.shape[1]
    tn = 512
    per_tok = scale.shape[0] != 1
    mod_spec = (pl.BlockSpec((tm, d), lambda i, j: (i, 0)) if per_tok
                else pl.BlockSpec((1, d), lambda i, j: (0, 0)))
    return pl.pallas_call(
        functools.partial(_norm_proj_kernel, tn=tn),
        out_shape=(jax.ShapeDtypeStruct((t, n), F32), jax.ShapeDtypeStruct((LANES, t), F32)),
        grid=(t // tm, n // tn),
        in_specs=[pl.BlockSpec((tm, d), lambda i, j: (i, 0)),
                  pl.BlockSpec((1, d), lambda i, j: (0, 0)),
                  mod_spec, mod_spec,
                  pl.BlockSpec((d, tn), lambda i, j: (0, j))],
        out_specs=(pl.BlockSpec((tm, tn), lambda i, j: (i, j)), pl.BlockSpec((LANES, tm), lambda i, j: (0, i))),
        scratch_shapes=[pltpu.VMEM((tm, d), BF16)],
        compiler_params=_cp("arbitrary", "arbitrary"),
        name="norm_proj",
    )(x, g.reshape(1, d), scale, shift, w)


def _rope_lanes(x, c, sa, sb):
    return x * c + pltpu.roll(x, LANES - 16, 1) * sa + pltpu.roll(x, 16, 1) * sb


def _mla_prep_kernel(qc_ref, kvc_ref, sm_ref, gq_ref, gkv_ref, wqt_ref, wuk_ref, wuvt_ref,
                     ct_ref, st_ref, ck_ref, ska_ref, skb_ref,
                     qt_out, k_out, vt_out, lat_out, kr_out):
    half = MLA_ROPE // 2
    qn = (_rms(qc_ref[...]) * gq_ref[...]).astype(BF16)
    lat = _rms(kvc_ref[...]) * gkv_ref[...]
    lat_out[...] = lat
    latb = lat.astype(BF16)
    krf = _rope_lanes(sm_ref[...], ck_ref[...], ska_ref[...], skb_ref[...])
    kr_out[...] = krf[:, :MLA_ROPE]
    kr_sh = pltpu.roll(krf, MLA_NOPE, 1)
    cos_t, sin_t = ct_ref[...], st_ref[...]
    tm = qn.shape[0]
    ones_row = (lax.broadcasted_iota(jnp.int32, (LANES, tm), 0) == MLA_V).astype(F32)
    pad = jnp.zeros((LANES - MLA_NOPE - MLA_ROPE, tm), F32)
    for h in range(MLA_HEADS):
        qt = lax.dot_general(wqt_ref[h], qn, (((1,), (1,)), ((), ())), preferred_element_type=F32)
        x1 = qt[MLA_NOPE:MLA_NOPE + half]
        x2 = qt[MLA_NOPE + half:MLA_NOPE + MLA_ROPE]
        qt = jnp.concatenate([qt[:MLA_NOPE], x1 * cos_t - x2 * sin_t, x1 * sin_t + x2 * cos_t, pad], axis=0)
        qt_out[h] = (qt * (MLA_SCALE * LOG2E)).astype(BF16)
        k_out[h] = (jnp.dot(latb, wuk_ref[h], preferred_element_type=F32) + kr_sh).astype(BF16)
        vt = lax.dot_general(wuvt_ref[h], latb, (((1,), (1,)), ((), ())), preferred_element_type=F32)
        vt_out[h] = (vt + ones_row).astype(BF16)


def mla_prep(z, gq, gkv, wqt, wuk, wuvt, tabs_t, tabs, tm):
    t = z.shape[0]
    hh = MLA_HEADS
    row = lambda w, off: pl.BlockSpec((tm, w), lambda i, o=off // w: (i, o))
    const2 = lambda a: pl.BlockSpec(a.shape, lambda i: (0, 0))
    const3 = lambda a: pl.BlockSpec(a.shape, lambda i: (0, 0, 0))
    tab_spec = pl.BlockSpec((tm, LANES), lambda i: (i, 0))
    tabt_spec = pl.BlockSpec((MLA_ROPE // 2, tm), lambda i: (0, i))
    return pl.pallas_call(
        _mla_prep_kernel,
        out_shape=(jax.ShapeDtypeStruct((hh, LANES, t), BF16),
                   jax.ShapeDtypeStruct((hh, t, LANES), BF16),
                   jax.ShapeDtypeStruct((hh, LANES, t), BF16),
                   jax.ShapeDtypeStruct((t, KV_LORA), F32),
                   jax.ShapeDtypeStruct((t, MLA_ROPE), F32)),
        grid=(t // tm,),
        in_specs=[row(Q_LORA, Z_QC), row(KV_LORA, Z_KVC), row(LANES, Z_SMALL),
                  const2(gq), const2(gkv), const3(wqt), const3(wuk), const3(wuvt),
                  tabt_spec, tabt_spec, tab_spec, tab_spec, tab_spec],
        out_specs=(pl.BlockSpec((hh, LANES, tm), lambda i: (0, 0, i)),
                   pl.BlockSpec((hh, tm, LANES), lambda i: (0, i, 0)),
                   pl.BlockSpec((hh, LANES, tm), lambda i: (0, 0, i)),
                   pl.BlockSpec((tm, KV_LORA), lambda i: (i, 0)),
                   pl.BlockSpec((tm, MLA_ROPE), lambda i: (i, 0))),
        compiler_params=_cp("arbitrary"),
        name="mla_prep",
    )(z, z, z, gq, gkv, wqt, wuk, wuvt, *tabs_t, *tabs[3:])


def _flash_kernel(qi_ref, ki_ref, qt_ref, k_ref, vt_ref, o_ref, m_ref, acc_ref, *, tq):
    qi = qi_ref[pl.program_id(0)]
    ki = ki_ref[pl.program_id(0)]

    @pl.when(ki == 0)
    def _():
        m_ref[...] = jnp.full(m_ref.shape, NEG, F32)
        acc_ref[...] = jnp.zeros(acc_ref.shape, F32)

    def scores(h):
        return jnp.dot(k_ref[h], qt_ref[h], preferred_element_type=F32)

    def accumulate(h, p, alpha):
        acc_ref[h] = alpha * acc_ref[h] + jnp.dot(vt_ref[h], p, preferred_element_type=F32)

    def step(diag):
        if diag:
            key = lax.broadcasted_iota(jnp.int32, (tq, tq), 0)
            qry = lax.broadcasted_iota(jnp.int32, (tq, tq), 1)
            keep = key <= qry
        s_next = scores(0)
        pending = None
        for h in range(MLA_HEADS):
            s = s_next
            if h + 1 < MLA_HEADS:
                s_next = scores(h + 1)
            if pending is not None:
                accumulate(*pending)
            if diag:
                s = jnp.where(keep, s, NEG)
            m_prev = m_ref[h:h + 1, :]
            m_new = jnp.maximum(m_prev, jnp.max(s, axis=0, keepdims=True))
            p = jnp.exp2(s - m_new).astype(BF16)
            m_ref[h:h + 1, :] = m_new
            pending = (h, p, jnp.exp2(m_prev - m_new))
        accumulate(*pending)

    @pl.when(ki < qi)
    def _():
        step(False)

    @pl.when(ki == qi)
    def _():
        step(True)
        for h in range(MLA_HEADS):
            acc = acc_ref[h]
            o_ref[:, h * LANES:(h + 1) * LANES] = (acc / acc[MLA_V:MLA_V + 1, :]).T.astype(BF16)


def flash_attention(qt, k, vt, tq):
    hh, t, _ = k.shape
    n = t // tq
    pairs = np.array([(i, j) for i in range(n) for j in range(i + 1)], np.int32)
    return pl.pallas_call(
        functools.partial(_flash_kernel, tq=tq),
        out_shape=jax.ShapeDtypeStruct((t, hh * LANES), BF16),
        grid_spec=pltpu.PrefetchScalarGridSpec(
            num_scalar_prefetch=2,
            grid=(pairs.shape[0],),
            in_specs=[pl.BlockSpec((hh, LANES, tq), lambda s, qa, ka: (0, 0, qa[s])),
                      pl.BlockSpec((hh, tq, LANES), lambda s, qa, ka: (0, ka[s], 0)),
                      pl.BlockSpec((hh, LANES, tq), lambda s, qa, ka: (0, 0, ka[s]))],
            out_specs=pl.BlockSpec((tq, hh * LANES), lambda s, qa, ka: (qa[s], 0)),
            scratch_shapes=[pltpu.VMEM((hh, tq), F32), pltpu.VMEM((hh, LANES, tq), F32)]),
        compiler_params=_cp("arbitrary"),
        name="flash_attention",
    )(jnp.asarray(pairs[:, 0]), jnp.asarray(pairs[:, 1]), qt, k, vt)


def _chunk_constants(L):
    nl = int(np.log2(L))
    assert 2 ** nl == L
    idx = np.arange(L)
    tri = (idx[None, :] <= idx[:, None]).astype(np.float32)
    mq, mk, pm = [], [], []
    for lev in range(nl):
        p = 1 << lev
        start_right = (idx >> (lev + 1) << (lev + 1)) + p
        right = ((idx >> lev) & 1) == 1
        u = idx[None, :]
        mq.append((right[:, None] & (u >= start_right[:, None]) & (u <= idx[:, None])).astype(np.float32))
        mk.append(((~right)[:, None] & (u > idx[:, None]) & (u < start_right[:, None])).astype(np.float32))
        same = (idx[:, None] >> (lev + 1)) == (idx[None, :] >> (lev + 1))
        pm.append((same & right[:, None] & (~right)[None, :]).astype(np.float32))
    pm.append(np.eye(L, dtype=np.float32))
    sums = np.concatenate([tri] + [a + b for a, b in zip(mq, mk)], axis=0)
    return jnp.asarray(sums, BF16), jnp.asarray(np.stack(pm), F32), jnp.asarray(tri, BF16), jnp.asarray(tri.T, BF16)


def _head_mask(width, heads, h):
    lane = lax.broadcasted_iota(jnp.int32, (1, width), 1)
    per = width // heads
    return ((lane >= h * per) & (lane < (h + 1) * per)).astype(F32)


def _gla_scan_kernel(gq_ref, gk_ref, gv_ref, gg_ref, sm_ref, wa_ref, ba_ref, sums_ref, pm_ref, gn_ref,
                     o_ref, st_out, st_ref, *, L, nl):
    step = pl.program_id(0)

    @pl.when(step == 0)
    def _():
        st_ref[...] = jnp.zeros(st_ref.shape, F32)

    width = GLA_HEADS * GLA_DK
    x = _bdot(sm_ref[...], wa_ref[...]) + ba_ref[...]
    la = _log_sigmoid(x) * (1.0 / GLA_GATE_TEMP)
    dsum = _dot01(sums_ref[...], la)
    b = dsum[0:L]
    b_last = b[L - 1:L, :]
    q = gq_ref[...] * (GLA_DK ** -0.5)
    k = gk_ref[...]
    q_in = (q * jnp.exp(b)).astype(BF16)
    k_out = k * jnp.exp(b_last - b)
    decay_all = jnp.exp(b_last)
    hms = [_head_mask(width, GLA_HEADS, h) for h in range(GLA_HEADS)]
    atts = [jnp.zeros((L, L), F32) for _ in range(GLA_HEADS)]
    for lev in range(nl + 1):
        fac = jnp.exp(dsum[(1 + lev) * L:(2 + lev) * L]) if lev < nl else None
        ql = q if fac is None else q * fac
        kl = (k if fac is None else k * fac).astype(BF16)
        stacked = jnp.concatenate([ql * hm for hm in hms], axis=0).astype(BF16)
        sc = lax.dot_general(stacked, kl, (((1,), (1,)), ((), ())), preferred_element_type=F32)
        keep = pm_ref[lev] > 0.0
        for h in range(GLA_HEADS):
            atts[h] = atts[h] + jnp.where(keep, sc[h * L:(h + 1) * L], 0.0)
    for h in range(GLA_HEADS):
        hm = hms[h]
        att = atts[h]
        vh = gv_ref[:, h * GLA_DV:(h + 1) * GLA_DV]
        st = st_ref[h]
        o = (jnp.dot(att.astype(BF16), vh.astype(BF16), preferred_element_type=F32)
             + lax.dot_general(q_in, st.astype(BF16), (((1,), (1,)), ((), ())), preferred_element_type=F32))
        st_ref[h] = st * decay_all + jnp.dot(vh.T.astype(BF16), (k_out * hm).astype(BF16),
                                             preferred_element_type=F32)
        gg = gg_ref[:, h * GLA_DV:(h + 1) * GLA_DV]
        y = _rms(o) * gn_ref[:, h * GLA_DV:(h + 1) * GLA_DV] * (gg * _sigmoid(gg))
        o_ref[:, h * GLA_DV:(h + 1) * GLA_DV] = y.astype(BF16)

    @pl.when(step == pl.num_programs(0) - 1)
    def _():
        st_out[...] = st_ref[...]


def gla_scan(z, wa, ba, gn, consts, L):
    t = z.shape[0]
    sums, pm, _, _ = consts
    nl = pm.shape[0] - 1
    width = GLA_HEADS * GLA_DK
    vw = GLA_HEADS * GLA_DV
    row = lambda w, off: pl.BlockSpec((L, w), lambda i, o=off // w: (i, o))
    c2 = lambda a: pl.BlockSpec(a.shape, lambda i: (0, 0))
    c3 = lambda a: pl.BlockSpec(a.shape, lambda i: (0, 0, 0))
    return pl.pallas_call(
        functools.partial(_gla_scan_kernel, L=L, nl=nl),
        out_shape=(jax.ShapeDtypeStruct((t, vw), BF16),
                   jax.ShapeDtypeStruct((GLA_HEADS, GLA_DV, width), F32)),
        grid=(t // L,),
        in_specs=[row(width, Z_GQ), row(width, Z_GK), row(vw, Z_GV), row(vw, Z_GG), row(LANES, Z_SMALL),
                  c2(wa), c2(ba), c2(sums), c3(pm), c2(gn)],
        out_specs=(pl.BlockSpec((L, vw), lambda i: (i, 0)),
                   pl.BlockSpec((GLA_HEADS, GLA_DV, width), lambda i: (0, 0, 0))),
        scratch_shapes=[pltpu.VMEM((GLA_HEADS, GLA_DV, width), F32)],
        compiler_params=_cp("arbitrary"),
        name="gla_scan",
    )(z, z, z, z, z, wa, ba, sums, pm, gn)


def _mlstm_scan_kernel(mq_ref, mk_ref, mv_ref, mo_ref, sm_ref, smt_ref, brow_ref, bcol_ref, tri_ref, trit_ref,
                       nrm_ref, o_ref, ct_out, n_out, m_out, ct_ref, n_ref, m_ref, *, L):
    step = pl.program_id(0)

    @pl.when(step == 0)
    def _():
        ct_ref[...] = jnp.zeros(ct_ref.shape, F32)
        n_ref[...] = jnp.zeros(n_ref.shape, F32)
        m_ref[...] = jnp.zeros(m_ref.shape, F32)

    width = ML_HEADS * ML_DK
    pre_c = sm_ref[...] + brow_ref[...]
    pre_r = smt_ref[...] + bcol_ref[...]
    b_c_all = _dot01(tri_ref[...], _log_sigmoid(pre_c))
    b_r_all = _dot01_r(_log_sigmoid(pre_r), trit_ref[...])
    q = mq_ref[...]
    k = mk_ref[...] * (ML_DK ** -0.5)
    kb = k.astype(BF16)
    row = lax.broadcasted_iota(jnp.int32, (L, L), 0)
    col = lax.broadcasted_iota(jnp.int32, (L, L), 1)
    causal = col <= row
    for h in range(ML_HEADS):
        hm = _head_mask(width, ML_HEADS, h)
        bc = b_c_all[:, SM_MF + h:SM_MF + h + 1]
        br = b_r_all[ML_HEADS + h:ML_HEADS + h + 1, :]
        ic = pre_c[:, SM_MI + h:SM_MI + h + 1]
        ir = pre_r[h:h + 1, :]
        m_prev = m_ref[h:h + 1, 0:1]
        dlog = jnp.where(causal, bc - br + ir, NEG)
        inter = bc + m_prev
        mt = jnp.maximum(inter, jnp.max(dlog, axis=1, keepdims=True))
        qh = (q * hm).astype(BF16)
        qk = lax.dot_general(qh, kb, (((1,), (1,)), ((), ())), preferred_element_type=F32)
        wi = jnp.exp(dlog - mt) * qk
        wint = jnp.exp(inter - mt)
        vh = mv_ref[:, h * ML_DV:(h + 1) * ML_DV]
        ct = ct_ref[h]
        nrow = n_ref[h:h + 1, :]
        num = (jnp.dot(wi.astype(BF16), vh.astype(BF16), preferred_element_type=F32)
               + wint * lax.dot_general(qh, ct.astype(BF16), (((1,), (1,)), ((), ())),
                                        preferred_element_type=F32))
        nq = jnp.sum(wi, axis=1, keepdims=True) + wint * jnp.sum(q * nrow, axis=1, keepdims=True)
        hh = num / jnp.maximum(jnp.abs(nq), jnp.exp(-mt))
        m_new = mt[L - 1:L, :]
        b_last = bc[L - 1:L, :]
        keep = jnp.exp(b_last + m_prev - m_new)
        wn = jnp.exp(b_last - bc + ic - m_new)
        kw = k * hm * wn
        ct_ref[h] = keep * ct + jnp.dot(vh.T.astype(BF16), kw.astype(BF16), preferred_element_type=F32)
        n_ref[h:h + 1, :] = keep * nrow + jnp.sum(kw, axis=0, keepdims=True)
        m_ref[h:h + 1, :] = jnp.broadcast_to(m_new, (1, LANES))
        mo = mo_ref[:, h * ML_DV:(h + 1) * ML_DV]
        y = _rms(hh) * nrm_ref[:, h * ML_DV:(h + 1) * ML_DV] * _sigmoid(mo)
        o_ref[:, h * ML_DV:(h + 1) * ML_DV] = y.astype(BF16)

    @pl.when(step == pl.num_programs(0) - 1)
    def _():
        ct_out[...] = ct_ref[...]
        n_out[...] = n_ref[...]
        m_out[...] = m_ref[...]


def mlstm_scan(z, smt, brow, bcol, nrm, consts, L):
    t = z.shape[0]
    _, _, tri, trit = consts
    width = ML_HEADS * ML_DK
    vw = ML_HEADS * ML_DV
    row = lambda w, off: pl.BlockSpec((L, w), lambda i, o=off // w: (i, o))
    c2 = lambda a: pl.BlockSpec(a.shape, lambda i: (0, 0))
    st_shapes = [((ML_HEADS, ML_DV, width), lambda i: (0, 0, 0)), ((SUBLANES, width), lambda i: (0, 0)),
                 ((SUBLANES, LANES), lambda i: (0, 0))]
    return pl.pallas_call(
        functools.partial(_mlstm_scan_kernel, L=L),
        out_shape=(jax.ShapeDtypeStruct((t, vw), BF16),) + tuple(
            jax.ShapeDtypeStruct(s, F32) for s, _ in st_shapes),
        grid=(t // L,),
        in_specs=[row(width, Z_MQ), row(width, Z_MK), row(vw, Z_MV), row(vw, Z_MO), row(LANES, Z_SMALL),
                  pl.BlockSpec((SUBLANES, L), lambda i: (SM_MI // SUBLANES, i)),
                  c2(brow), c2(bcol), c2(tri), c2(trit), c2(nrm)],
        out_specs=(pl.BlockSpec((L, vw), lambda i: (i, 0)),) + tuple(pl.BlockSpec(s, f) for s, f in st_shapes),
        scratch_shapes=[pltpu.VMEM(s, F32) for s, _ in st_shapes],
        compiler_params=_cp("arbitrary"),
        name="mlstm_scan",
    )(z, z, z, z, z, smt, brow, bcol, tri, trit, nrm)


def _rec_step_kernel(gq_ref, gk_ref, gv_ref, gg_ref, mq_ref, mk_ref, mv_ref, mo_ref, sm_ref,
                     gqt_ref, gkt_ref, mqt_ref, mkt_ref, smt_ref,
                     wa_ref, ba_ref, wat_ref, bat_ref, brow_ref, gn_ref, mn_ref,
                     s_ref, c_ref, n_ref, m_ref,
                     bo_ref, co_ref, s_out, c_out, n_out, m_out, *, rows):
    blk = pl.program_id(0)
    nb = gqt_ref.shape[1]
    gw = GLA_HEADS * GLA_DK
    mw = ML_HEADS * ML_DK

    pre = sm_ref[...] + brow_ref[...]
    i_v = pre
    f_v = _log_sigmoid(pltpu.roll(pre, LANES - (SM_MF - SM_MI), 1))
    m_prev = m_ref[...]
    mt = jnp.maximum(f_v + m_prev, i_v)
    w_intra = jnp.exp(i_v - mt)
    w_inter = jnp.exp(f_v + m_prev - mt)
    inv_floor = jnp.exp(-mt)
    m_out[...] = mt
    mq = mq_ref[...]
    mk = mk_ref[...] * (ML_DK ** -0.5)
    n_old = n_ref[...]
    gq = gq_ref[...] * (GLA_DK ** -0.5)
    gk = gk_ref[...]
    qk_g, qk_m, qn_m, den = [], [], [], []
    keep_b = jnp.zeros((rows, mw), F32)
    wnew_b = jnp.zeros((rows, mw), F32)
    for h in range(ML_HEADS):
        hm = _head_mask(mw, ML_HEADS, h)
        qk_m.append(jnp.sum(mq * mk * hm, axis=1, keepdims=True))
        qn_m.append(jnp.sum(mq * n_old * hm, axis=1, keepdims=True))
        wi_h = w_intra[:, SM_MI + h:SM_MI + h + 1]
        we_h = w_inter[:, SM_MI + h:SM_MI + h + 1]
        nq = wi_h * qk_m[h] + we_h * qn_m[h]
        den.append(jnp.maximum(jnp.abs(nq), inv_floor[:, SM_MI + h:SM_MI + h + 1]))
        keep_b = keep_b + hm * we_h
        wnew_b = wnew_b + hm * wi_h
    n_out[...] = keep_b * n_old + wnew_b * mk
    for h in range(GLA_HEADS):
        qk_g.append(jnp.sum(gq * gk * _head_mask(gw, GLA_HEADS, h), axis=1, keepdims=True))

    xt = _bdot(wat_ref[...], smt_ref[...]) + bat_ref[...]
    at = jnp.exp(_log_sigmoid(xt) * (1.0 / GLA_GATE_TEMP))
    seq_lane = lax.broadcasted_iota(jnp.int32, (nb, LANES), 0)

    for r in range(rows):
        onehot = (seq_lane == blk * rows + r).astype(BF16)
        a_bc = _dot01_r(at, onehot)
        gk_bc = _dot01_r(gkt_ref[...], onehot)
        gq_bc = _dot01_r(gqt_ref[...], onehot) * (GLA_DK ** -0.5)
        mk_bc = _dot01_r(mkt_ref[...], onehot) * (ML_DK ** -0.5)
        mq_bc = _dot01_r(mqt_ref[...], onehot)
        for h in range(GLA_HEADS):
            sl = slice(h * GLA_DK, (h + 1) * GLA_DK)
            v_row = gv_ref[r:r + 1, h * GLA_DV:(h + 1) * GLA_DV]
            s_new = a_bc[sl] * s_ref[r, h] + gk_bc[sl] * v_row
            s_out[r, h] = s_new
            o = jnp.sum(gq_bc[sl] * s_new, axis=0, keepdims=True)
            gg = gg_ref[r:r + 1, h * GLA_DV:(h + 1) * GLA_DV]
            y = _rms(o) * gn_ref[:, h * GLA_DV:(h + 1) * GLA_DV] * (gg * _sigmoid(gg))
            bo_ref[r:r + 1, h * GLA_DV:(h + 1) * GLA_DV] = y.astype(BF16)
        for h in range(ML_HEADS):
            sl = slice(h * ML_DK, (h + 1) * ML_DK)
            v_row = mv_ref[r:r + 1, h * ML_DV:(h + 1) * ML_DV]
            c_old = c_ref[r, h]
            wi_s = w_intra[r:r + 1, SM_MI + h:SM_MI + h + 1]
            we_s = w_inter[r:r + 1, SM_MI + h:SM_MI + h + 1]
            num = (wi_s * qk_m[h][r:r + 1, :]) * v_row + we_s * jnp.sum(mq_bc[sl] * c_old, axis=0, keepdims=True)
            hh = num / den[h][r:r + 1, :]
            c_out[r, h] = we_s * c_old + wi_s * (mk_bc[sl] * v_row)
            mo = mo_ref[r:r + 1, h * ML_DV:(h + 1) * ML_DV]
            y = _rms(hh) * mn_ref[:, h * ML_DV:(h + 1) * ML_DV] * _sigmoid(mo)
            co_ref[r:r + 1, h * ML_DV:(h + 1) * ML_DV] = y.astype(BF16)


def rec_step(z, zt, wa, ba, wat, bat, brow, gn, mn, s0, c0, layer, n0, m0):
    nb = z.shape[0]
    rows = SUBLANES
    gw = GLA_HEADS * GLA_DK
    mw = ML_HEADS * ML_DK
    gvw = GLA_HEADS * GLA_DV
    mvw = ML_HEADS * ML_DV
    row = lambda w, off: pl.BlockSpec((rows, w), lambda i, o=off // w: (i, o))
    colb = lambda w, off: pl.BlockSpec((w, nb), lambda i, o=off // w: (o, 0))
    c2 = lambda a: pl.BlockSpec(a.shape, lambda i: (0, 0))
    st4 = lambda hds, dk, dv: pl.BlockSpec((rows, hds, dk, dv), lambda i: (i, 0, 0, 0))
    st5 = lambda hds, dk, dv: pl.BlockSpec((None, rows, hds, dk, dv), lambda i: (layer, i, 0, 0, 0))
    return pl.pallas_call(
        functools.partial(_rec_step_kernel, rows=rows),
        out_shape=(jax.ShapeDtypeStruct((nb, gvw), BF16), jax.ShapeDtypeStruct((nb, mvw), BF16),
                   jax.ShapeDtypeStruct(s0.shape[1:], F32), jax.ShapeDtypeStruct(c0.shape[1:], F32),
                   jax.ShapeDtypeStruct((nb, mw), F32), jax.ShapeDtypeStruct((nb, LANES), F32)),
        grid=(nb // rows,),
        in_specs=[row(gw, Z_GQ), row(gw, Z_GK), row(gvw, Z_GV), row(gvw, Z_GG),
                  row(mw, Z_MQ), row(mw, Z_MK), row(mvw, Z_MV), row(mvw, Z_MO), row(LANES, Z_SMALL),
                  colb(gw, Z_GQ), colb(gw, Z_GK), colb(mw, Z_MQ), colb(mw, Z_MK), colb(LANES, Z_SMALL),
                  c2(wa), c2(ba), c2(wat), c2(bat), c2(brow), c2(gn), c2(mn),
                  st5(GLA_HEADS, GLA_DK, GLA_DV), st5(ML_HEADS, ML_DK, ML_DV),
                  pl.BlockSpec((rows, mw), lambda i: (i, 0)), pl.BlockSpec((rows, LANES), lambda i: (i, 0))],
        out_specs=(pl.BlockSpec((rows, gvw), lambda i: (i, 0)), pl.BlockSpec((rows, mvw), lambda i: (i, 0)),
                   st4(GLA_HEADS, GLA_DK, GLA_DV), st4(ML_HEADS, ML_DK, ML_DV),
                   pl.BlockSpec((rows, mw), lambda i: (i, 0)), pl.BlockSpec((rows, LANES), lambda i: (i, 0))),
        compiler_params=_cp("arbitrary"),
        name="rec_step",
    )(z, z, z, z, z, z, z, z, z, zt, zt, zt, zt, zt, wa, ba, wat, bat, brow, gn, mn, s0, c0, n0, m0)


def _mla_prep_s_kernel(qc_ref, kvc_ref, sm_ref, gq_ref, gkv_ref, wq_ref, wukt_ref,
                       cq_ref, sqa_ref, sqb_ref, ck_ref, ska_ref, skb_ref,
                       ql_out, qp_out, lat_out, kr_out):
    qn = (_rms(qc_ref[...]) * gq_ref[...]).astype(BF16)
    lat_out[...] = _rms(kvc_ref[...]) * gkv_ref[...]
    krf = _rope_lanes(sm_ref[...], ck_ref[...], ska_ref[...], skb_ref[...])
    kr_out[...] = krf[:, :MLA_ROPE]
    cq, sqa, sqb = cq_ref[...], sqa_ref[...], sqb_ref[...]
    for h in range(MLA_HEADS):
        qh = _rope_lanes(jnp.dot(qn, wq_ref[h], preferred_element_type=F32), cq, sqa, sqb)
        ql_out[h] = _bdot(qh, wukt_ref[h])
        qp_out[h] = pltpu.roll(qh, LANES - MLA_NOPE, 1)


def mla_prep_s(z, gq, gkv, wq, wukt, tabs):
    nb = z.shape[0]
    hh = MLA_HEADS
    row = lambda w, off: pl.BlockSpec((nb, w), lambda i, o=off // w: (0, o))
    c2 = lambda a: pl.BlockSpec(a.shape, lambda i: (0, 0))
    c3 = lambda a: pl.BlockSpec(a.shape, lambda i: (0, 0, 0))
    return pl.pallas_call(
        _mla_prep_s_kernel,
        out_shape=(jax.ShapeDtypeStruct((hh, nb, LANES), F32), jax.ShapeDtypeStruct((hh, nb, LANES), F32),
                   jax.ShapeDtypeStruct((nb, KV_LORA), F32), jax.ShapeDtypeStruct((nb, MLA_ROPE), F32)),
        grid=(1,),
        in_specs=[row(Q_LORA, Z_QC), row(KV_LORA, Z_KVC), row(LANES, Z_SMALL),
                  c2(gq), c2(gkv), c3(wq), c3(wukt)] + [c2(t) for t in tabs],
        out_specs=(pl.BlockSpec((hh, nb, LANES), lambda i: (0, 0, 0)),
                   pl.BlockSpec((hh, nb, LANES), lambda i: (0, 0, 0)),
                   pl.BlockSpec((nb, KV_LORA), lambda i: (0, 0)), pl.BlockSpec((nb, MLA_ROPE), lambda i: (0, 0))),
        compiler_params=_cp("arbitrary"),
        name="mla_prep_sample",
    )(z, z, z, gq, gkv, wq, wukt, *tabs)


def _mla_decode_kernel(pt_ref, ql_ref, qp_ref, ln_ref, kn_ref, lat_hbm, kr_hbm, o_ref, lat_buf, kr_buf, sem,
                       *, layer, n_pages):
    b = pl.program_id(0)
    slot = lax.rem(b, 2)

    def page_copies(seq, sl, p):
        page = pt_ref[seq, p]
        off = pl.multiple_of(p * PAGE_SIZE, PAGE_SIZE)
        return (pltpu.make_async_copy(lat_hbm.at[layer, page], lat_buf.at[sl, pl.ds(off, PAGE_SIZE), :],
                                      sem.at[0, sl]),
                pltpu.make_async_copy(kr_hbm.at[layer, page], kr_buf.at[sl, :, pl.ds(off, PAGE_SIZE)],
                                      sem.at[1, sl]))

    def start_all(seq, sl):
        def body(p, carry):
            for parity in range(2):
                lat_copy, rope_copy = page_copies(seq, sl, 2 * p + parity)
                lat_copy.start(priority=parity)
                rope_copy.start(priority=1 - parity)
            return carry
        lax.fori_loop(0, n_pages // 2, body, 0, unroll=2)

    def wait_all(seq, sl):
        def body(p, carry):
            for c in page_copies(seq, sl, p):
                c.wait()
            return carry
        lax.fori_loop(0, n_pages, body, 0, unroll=4)

    @pl.when(b == 0)
    def _():
        start_all(0, 0)

    @pl.when(b + 1 < pl.num_programs(0))
    def _():
        start_all(b + 1, 1 - slot)

    wait_all(b, slot)

    ql = ql_ref[0]
    qp = qp_ref[0][:, :MLA_ROPE]
    lat = lat_buf[slot]
    latb = lat.astype(BF16)
    s = (lax.dot_general(ql.astype(BF16), latb, (((1,), (1,)), ((), ())), preferred_element_type=F32)
         + jnp.dot(qp.astype(BF16), kr_buf[slot].astype(BF16), preferred_element_type=F32)) * MLA_SCALE
    lat_new = ln_ref[0]
    s_new = (jnp.sum(ql * lat_new, axis=1, keepdims=True)
             + jnp.sum(qp * kn_ref[0], axis=1, keepdims=True)) * MLA_SCALE
    m = jnp.maximum(jnp.max(s, axis=1, keepdims=True), s_new)
    p = jnp.exp(s - m)
    pn = jnp.exp(s_new - m)
    num = jnp.dot(p.astype(BF16), latb, preferred_element_type=F32) + pn * lat_new
    o_ref[0] = num / (jnp.sum(p, axis=1, keepdims=True) + pn)


def mla_decode(page_table, ql, qp, lat_new, kr_new, cache_lat, cache_rope_t, layer):
    nb, n_pages = page_table.shape
    assert n_pages % 2 == 0
    past = n_pages * PAGE_SIZE
    hh = MLA_HEADS
    seq = lambda w: pl.BlockSpec((1, hh, w), lambda b, pt: (b, 0, 0))
    new = lambda w: pl.BlockSpec((1, 1, w), lambda b, pt: (b, 0, 0))
    return pl.pallas_call(
        functools.partial(_mla_decode_kernel, layer=layer, n_pages=n_pages),
        out_shape=jax.ShapeDtypeStruct((nb, hh, KV_LORA), F32),
        grid_spec=pltpu.PrefetchScalarGridSpec(
            num_scalar_prefetch=1,
            grid=(nb,),
            in_specs=[seq(LANES), seq(LANES), new(KV_LORA), new(MLA_ROPE),
                      pl.BlockSpec(memory_space=pl.ANY), pl.BlockSpec(memory_space=pl.ANY)],
            out_specs=pl.BlockSpec((1, hh, KV_LORA), lambda b, pt: (b, 0, 0)),
            scratch_shapes=[pltpu.VMEM((2, past, KV_LORA), F32), pltpu.VMEM((2, MLA_ROPE, past), F32),
                            pltpu.SemaphoreType.DMA((2, 2))]),
        compiler_params=_cp("arbitrary"),
        name="mla_decode",
    )(page_table, ql, qp, lat_new, kr_new, cache_lat, cache_rope_t)


def _uv_proj_kernel(o_ref, w_ref, a_ref):
    a_ref[...] = _bdot(o_ref[...], w_ref[0]).astype(BF16)


def uv_proj(o_lat, wuv):
    nb = o_lat.shape[0]
    return pl.pallas_call(
        _uv_proj_kernel,
        out_shape=jax.ShapeDtypeStruct((nb, MLA_HEADS * LANES), BF16),
        grid=(MLA_HEADS,),
        in_specs=[pl.BlockSpec((nb, KV_LORA), lambda h: (0, h)),
                  pl.BlockSpec((1, KV_LORA, LANES), lambda h: (h, 0, 0))],
        out_specs=pl.BlockSpec((nb, LANES), lambda h: (0, h)),
        compiler_params=_cp("arbitrary"),
        name="uv_proj",
    )(o_lat, wuv)


def _merge_kernel(a_ref, b_ref, c_ref, ga_ref, gb_ref, gc_ref, x_ref, g1_ref, wa_ref, wb_ref, wc_ref, wo_ref,
                  nf_ref, sc_ref, sh_ref, x1_ref, h2_ref):
    merged = (_sigmoid(ga_ref[...]) * jnp.dot(a_ref[...], wa_ref[...], preferred_element_type=F32)
              + _sigmoid(gb_ref[...]) * jnp.dot(b_ref[...], wb_ref[...], preferred_element_type=F32)
              + _sigmoid(gc_ref[...]) * jnp.dot(c_ref[...], wc_ref[...], preferred_element_type=F32))
    mix = _bdot(merged, wo_ref[...])
    x1 = x_ref[...] + g1_ref[...] * mix
    x1_ref[...] = x1
    h2 = _rms(x1) * nf_ref[...] * (1.0 + sc_ref[...]) + sh_ref[...]
    h2_ref[...] = h2.astype(BF16)


def merge(a, b, c, z, x, gate1, wa, wb, wc, wo, nf, scale2, shift2, tm):
    t, d = x.shape
    per_tok = gate1.shape[0] != 1
    mod_spec = (pl.BlockSpec((tm, d), lambda i: (i, 0)) if per_tok else pl.BlockSpec((1, d), lambda i: (0, 0)))
    rowb = lambda w, o=0: pl.BlockSpec((tm, w), lambda i, o=o: (i, o))
    c2 = lambda arr: pl.BlockSpec(arr.shape, lambda i: (0, 0))
    return pl.pallas_call(
        _merge_kernel,
        out_shape=(jax.ShapeDtypeStruct((t, d), F32), jax.ShapeDtypeStruct((t, d), BF16)),
        grid=(t // tm,),
        in_specs=[rowb(a.shape[1]), rowb(b.shape[1]), rowb(c.shape[1]),
                  rowb(d, Z_GA // d), rowb(d, Z_GB // d), rowb(d, Z_GC // d), rowb(d), mod_spec,
                  c2(wa), c2(wb), c2(wc), c2(wo), pl.BlockSpec((1, d), lambda i: (0, 0)), mod_spec, mod_spec],
        out_specs=(rowb(d), rowb(d)),
        compiler_params=_cp("arbitrary"),
        name="merge",
    )(a, b, c, z, z, z, x, gate1, wa, wb, wc, wo, nf.reshape(1, d), scale2, shift2)


def _oddeven_merge(lo, hi, r):
    step = r * 2
    if step < hi - lo:
        yield from _oddeven_merge(lo, hi, step)
        yield from _oddeven_merge(lo + r, hi, step)
        yield from [(i, i + r) for i in range(lo + r, hi - r, step)]
    else:
        yield (lo, lo + r)


def _oddeven_merge_sort(lo, hi):
    if hi - lo >= 1:
        mid = lo + (hi - lo) // 2
        yield from _oddeven_merge_sort(lo, mid)
        yield from _oddeven_merge_sort(mid + 1, hi)
        yield from _oddeven_merge(lo, hi, 1)


def _top_rows(x, n):
    groups = x.shape[0] // SUBLANES
    width = 1 << (groups - 1).bit_length()
    cols = [x[SUBLANES * r:SUBLANES * (r + 1)] for r in range(groups)]
    cols += [jnp.full_like(cols[0], -jnp.inf)] * (width - groups)
    for i, j in _oddeven_merge_sort(0, width - 1):
        cols[i], cols[j] = jnp.maximum(cols[i], cols[j]), jnp.minimum(cols[i], cols[j])
    rows = []
    for k in range(n):
        head = cols[0]
        m = jnp.max(head, axis=0, keepdims=True)
        rows.append(m)
        hit = head == m
        depth = min(width, n - k)
        for r in range(depth - 1):
            cols[r] = jnp.where(hit, cols[r + 1], cols[r])
        cols[depth - 1] = jnp.where(hit, -jnp.inf, cols[depth - 1])
    return rows


def _peer_topk_kernel(h2_ref, wq_ref, k1_ref, k2_ref, thr_out, s2_out, l1_out):
    h2 = h2_ref[...]
    k1 = PEER_TOPK + 1
    for h in range(PEER_HEADS):
        qh = jnp.dot(h2, wq_ref[h], preferred_element_type=F32)
        s1 = _bdot_nt(k1_ref[h], qh[:, :PEER_HALF])
        s2 = _bdot_nt(k2_ref[h], qh[:, PEER_HALF:])
        a = _top_rows(s1, k1)
        b = _top_rows(s2, k1)
        n_rows = -(-k1 // SUBLANES) * SUBLANES
        ninf = jnp.full_like(b[0], -jnp.inf)
        split = 4
        b_all = jnp.concatenate(b + [ninf] * (n_rows - k1), axis=0)
        a_tail = jnp.concatenate([ninf] * split + a[split:] + [ninf] * (n_rows - k1), axis=0)
        span = lambda r: -(-(k1 // (r + 1)) // SUBLANES) * SUBLANES
        cand = [a[p] + b_all[:span(p)] for p in range(split)]
        cand += [b[q] + a_tail[:span(q)] for q in range(k1 // (split + 1))]
        best = _top_rows(jnp.concatenate(cand, axis=0), k1)
        zsum = jnp.zeros_like(best[0])
        for c in best[:PEER_TOPK]:
            zsum = zsum + jnp.exp(c - best[0])
        mid = 0.5 * (best[PEER_TOPK - 1] + best[PEER_TOPK])
        tabs = ((thr_out, ((mid - s1) - b[0]) * LOG2E), (s2_out, (s2 - b[0]) * LOG2E),
                (l1_out, (s1 - a[0]) * LOG2E - jnp.log(zsum) * LOG2E))
        for ref, val in tabs:
            for c in range(val.shape[1] // LANES):
                ref[h, c] = val[:, c * LANES:(c + 1) * LANES]


def peer_topk(h2, wq, k1, k2, tm):
    t, d = h2.shape
    hh = PEER_HEADS
    c3 = lambda a: pl.BlockSpec(a.shape, lambda i: (0, 0, 0))
    assert tm % LANES == 0
    tab = jax.ShapeDtypeStruct((hh, t // LANES, PEER_KEYS, LANES), F32)
    tab_spec = pl.BlockSpec((hh, tm // LANES, PEER_KEYS, LANES), lambda i: (0, i, 0, 0))
    return pl.pallas_call(
        _peer_topk_kernel,
        out_shape=(tab, tab, tab),
        grid=(t // tm,),
        in_specs=[pl.BlockSpec((tm, d), lambda i: (i, 0)), c3(wq), c3(k1), c3(k2)],
        out_specs=(tab_spec, tab_spec, tab_spec),
        compiler_params=_cp("arbitrary"),
        name="peer_topk",
    )(h2, wq, k1, k2)


def _peer_dense_kernel(h2_ref, u_ref, vt_ref, thr_ref, l1_ref, s2_ref, x1_ref, g2_ref, gf_ref,
                       o_ref, acc_ref, act_ref, wg_ref, *, ni, final):
    e = pl.program_id(1)

    @pl.when(e == 0)
    def _():
        acc_ref[...] = jnp.zeros(acc_ref.shape, F32)

    act_ref[...] = _gelu_tanh(lax.dot_general(u_ref[...], h2_ref[...], (((1,), (1,)), ((), ())),
                                              preferred_element_type=F32))
    tm = act_ref.shape[1]
    for c in range(tm // LANES):
        cs = slice(c * LANES, (c + 1) * LANES)
        for ii in range(ni):
            rs = slice(ii * PEER_KEYS, (ii + 1) * PEER_KEYS)
            w = None
            for h in range(PEER_HEADS):
                s2 = s2_ref[h, c]
                part = jnp.where(s2 >= thr_ref[h, c, ii:ii + 1, :],
                                 jnp.exp2(s2 + l1_ref[h, c, ii:ii + 1, :]), 0.0)
                w = part if w is None else w + part
            wg_ref[rs, cs] = (w * act_ref[rs, cs]).astype(BF16)
    acc_ref[...] += jnp.dot(vt_ref[...], wg_ref[...], preferred_element_type=F32)

    @pl.when(e == pl.num_programs(1) - 1)
    def _():
        out = x1_ref[...] + g2_ref[...] * acc_ref[...].T
        o_ref[...] = _rms(out) * gf_ref[...] if final else out


def peer_dense(h2, u, vt, layer, thr, l1, s2, x1, gate2, g_final, final, tm, ni):
    t, d = h2.shape
    n_exp = u.shape[1]
    te = ni * PEER_KEYS
    hh = PEER_HEADS
    per_tok = gate2.shape[0] != 1
    mod_spec = (pl.BlockSpec((tm, d), lambda i, e: (i, 0)) if per_tok
                else pl.BlockSpec((1, d), lambda i, e: (0, 0)))
    nc = tm // LANES
    sub = pl.BlockSpec((hh, nc, ni, LANES), lambda i, e: (0, i, e, 0))
    full = pl.BlockSpec((hh, nc, PEER_KEYS, LANES), lambda i, e: (0, i, 0, 0))
    return pl.pallas_call(
        functools.partial(_peer_dense_kernel, ni=ni, final=final),
        out_shape=jax.ShapeDtypeStruct((t, d), F32),
        grid=(t // tm, n_exp // te),
        in_specs=[pl.BlockSpec((tm, d), lambda i, e: (i, 0)),
                  pl.BlockSpec((None, te, d), lambda i, e: (layer, e, 0)),
                  pl.BlockSpec((None, d, te), lambda i, e: (layer, 0, e)),
                  sub, sub, full,
                  pl.BlockSpec((tm, d), lambda i, e: (i, 0)), mod_spec,
                  pl.BlockSpec((1, d), lambda i, e: (0, 0))],
        out_specs=pl.BlockSpec((tm, d), lambda i, e: (i, 0)),
        scratch_shapes=[pltpu.VMEM((d, tm), F32), pltpu.VMEM((te, tm), F32), pltpu.VMEM((te, tm), BF16)],
        compiler_params=_cp("arbitrary", "arbitrary"),
        name="peer_dense",
    )(h2, u, vt, thr, l1, s2, x1, gate2, g_final.reshape(1, d))


def _pad_to(a, axis, size):
    pad = [(0, 0)] * a.ndim
    pad[axis] = (0, size - a.shape[axis])
    return jnp.pad(a, pad)


def _prep_w_in(w):
    sizes = (Q_LORA, KV_LORA, MLA_ROPE, 256, 256, 512, GLA_GATE_RANK, 512, 256, 256, 512, ML_HEADS, ML_HEADS, 512,
             D_MODEL, D_MODEL, D_MODEL)
    offs = np.concatenate([[0], np.cumsum(sizes)])
    (q_c, kv_c, k_pe, gq, gk, gv, ga, gg, mq, mk, mv, mi, mf, mo, g_a, g_b, g_c) = [
        w[:, offs[i]:offs[i + 1]] for i in range(len(sizes))]
    small = _pad_to(jnp.concatenate([k_pe, ga, mi, mf], axis=1), 1, LANES)
    out = jnp.concatenate([g_a, g_b, g_c, gv, gg, mv, mo, q_c, gq, gk, mq, mk, kv_c, small], axis=1)
    assert out.shape[1] == Z_WIDTH
    return out.astype(BF16)


def _rope_tables(pos):
    half = MLA_ROPE // 2
    inv = ROPE_BASE ** (-jnp.arange(half, dtype=F32) / half)
    ang = pos.astype(F32)[:, None] * inv[None, :]
    cos, sin = jnp.cos(ang), jnp.sin(ang)
    n = pos.shape[0]
    z = lambda w: jnp.zeros((n, w), F32)
    one = jnp.ones((n, MLA_NOPE), F32)
    rest = LANES - MLA_NOPE - MLA_ROPE
    cq = jnp.concatenate([one, cos, cos, z(rest)], axis=1)
    sqa = jnp.concatenate([z(MLA_NOPE), -sin, z(half), z(rest)], axis=1)
    sqb = jnp.concatenate([z(MLA_NOPE), z(half), sin, z(rest)], axis=1)
    ck = jnp.concatenate([cos, cos, z(LANES - MLA_ROPE)], axis=1)
    ska = jnp.concatenate([-sin, z(LANES - half)], axis=1)
    skb = jnp.concatenate([z(half), sin, z(LANES - MLA_ROPE)], axis=1)
    return (cq, sqa, sqb, ck, ska, skb), (cos.T, sin.T)


def _layer_weights(l, P):
    w = {}
    w['w_in'] = _prep_w_in(P['w_in'][l])
    w['gq'] = P['mla_q_norm'][l].reshape(1, Q_LORA)
    w['gkv'] = P['mla_kv_norm'][l].reshape(1, KV_LORA)
    w['wq'] = _pad_to(jnp.transpose(P['w_uq'][l], (1, 0, 2)), 2, LANES).astype(BF16)
    wuk = jnp.transpose(P['w_uk'][l], (1, 0, 2))
    w['wuk'] = _pad_to(wuk, 2, LANES).astype(BF16)
    w['wukt'] = _pad_to(jnp.transpose(wuk, (0, 2, 1)), 1, LANES).astype(BF16)
    w['wuv'] = _pad_to(jnp.transpose(P['w_uv'][l], (1, 0, 2)), 2, LANES).astype(BF16)
    w['wqt'] = _pad_to(jnp.transpose(P['w_uq'][l], (1, 2, 0)), 1, LANES).astype(BF16)
    w['wuvt'] = _pad_to(jnp.transpose(P['w_uv'][l], (1, 2, 0)), 1, LANES).astype(BF16)
    wa = jnp.zeros((LANES, GLA_HEADS * GLA_DK), F32).at[SM_GA:SM_GA + GLA_GATE_RANK].set(P['w_gla_a'][l])
    w['gla_wa'] = wa.astype(BF16)
    w['gla_wat'] = wa.T.astype(BF16)
    w['gla_ba'] = P['b_gla_a'][l].reshape(1, -1)
    w['gla_bat'] = P['b_gla_a'][l].reshape(-1, 1)
    w['gla_norm'] = P['gla_norm'][l].reshape(1, -1)
    w['ml_norm'] = P['ml_norm'][l].reshape(1, -1)
    brow = jnp.zeros((1, LANES), F32)
    brow = brow.at[0, SM_MI:SM_MI + ML_HEADS].set(P['ml_b_i'][l]).at[0, SM_MF:SM_MF + ML_HEADS].set(P['ml_b_f'][l])
    w['ml_brow'] = brow
    w['ml_bcol'] = jnp.concatenate([P['ml_b_i'][l], P['ml_b_f'][l]]).reshape(2 * ML_HEADS, 1)
    wpa = P['w_proj_a'][l].reshape(MLA_HEADS, MLA_V, D_MODEL)
    w['wpa'] = _pad_to(wpa, 1, LANES).reshape(MLA_HEADS * LANES, D_MODEL).astype(BF16)
    w['wpb'] = P['w_proj_b'][l].astype(BF16)
    w['wpc'] = P['w_proj_c'][l].astype(BF16)
    w['wo'] = P['w_o'][l].astype(BF16)
    w['peer_wq'] = jnp.transpose(P['peer_wq'][l], (1, 0, 2)).astype(BF16)
    w['peer_k1'] = P['peer_keys'][l][:, 0].astype(BF16)
    w['peer_k2'] = P['peer_keys'][l][:, 1].astype(BF16)
    w['layer'] = l
    w['peer_u'] = P['peer_u_b']
    w['peer_vt'] = P['peer_vt_b']
    return w


def _mods(c, w_ada, b_ada, layer):
    rows = c.shape[0]
    cp = _pad_to(c, 0, SUBLANES) if rows < SUBLANES else c
    mod = ada_mod(cp, w_ada, b_ada[layer], layer)[:rows]
    return jnp.split(mod, 6, axis=-1)


def _peer_block(h2, x1, gate2, w, g_final, final, tm, ni):
    thr, s2, l1 = peer_topk(h2, w['peer_wq'], w['peer_k1'], w['peer_k2'], min(tm, 2 * LANES))
    return peer_dense(h2, w['peer_u'], w['peer_vt'], w['layer'], thr, l1, s2, x1, gate2, g_final, final, tm, ni)


def _trunk_prompt(x, c, P, W):
    t = x.shape[0]
    tm = min(512, t)
    L = min(128, t)
    consts = _chunk_constants(L)
    tabs, tabs_t = _rope_tables(jnp.arange(t, dtype=jnp.int32))
    lat_rows, rope_rows, gla_st, c_st, n_st, m_st = [], [], [], [], [], []
    for l in range(len(W)):
        w = W[l]
        shift1, scale1, gate1, shift2, scale2, gate2 = _mods(c, P['w_ada'], P['b_ada'], l)
        z, zst = norm_proj(x, P['norm_mix'][l], scale1, shift1, w['w_in'], min(2 * tm, t))
        qt, k, vt, lat, kr = mla_prep(z, w['gq'], w['gkv'], w['wqt'], w['wuk'], w['wuvt'], tabs_t, tabs, tm)
        a = flash_attention(qt, k, vt, min(2 * tm, t))
        b, st = gla_scan(z, w['gla_wa'], w['gla_ba'], w['gla_norm'], consts, L)
        cc, ct, nrow, mrow = mlstm_scan(z, zst, w['ml_brow'], w['ml_bcol'], w['ml_norm'], consts, L)
        x1, h2 = merge(a, b, cc, z, x, gate1, w['wpa'], w['wpb'], w['wpc'], w['wo'],
                       P['norm_ffn'][l], scale2, shift2, tm)
        x = _peer_block(h2, x1, gate2, w, P['norm_final'], l == len(W) - 1, tm, PEER_TILE_ROWS)
        lat_rows.append(lat)
        rope_rows.append(kr)
        gla_st.append(jnp.stack([st[h][:, h * GLA_DK:(h + 1) * GLA_DK].T for h in range(GLA_HEADS)]))
        c_st.append(jnp.stack([ct[h][:, h * ML_DK:(h + 1) * ML_DK].T for h in range(ML_HEADS)]))
        n_st.append(jnp.stack([nrow[h, h * ML_DK:(h + 1) * ML_DK] for h in range(ML_HEADS)]))
        m_st.append(mrow[:ML_HEADS, 0])
    return x, tuple(jnp.stack(r) for r in (lat_rows, rope_rows, gla_st, c_st, n_st, m_st))


def _trunk_sample(x, c, P, W, cache_lat, cache_rope, page_table, s_gla, s_c, s_n, s_m):
    nb = x.shape[0]
    past = page_table.shape[1] * PAGE_SIZE
    tabs, _ = _rope_tables(jnp.full((1,), past, dtype=jnp.int32))
    cache_rope = jnp.swapaxes(cache_rope, 2, 3)
    lat_rows, rope_rows, gla_st, c_st, n_st, m_st = [], [], [], [], [], []
    for l in range(len(W)):
        w = W[l]
        shift1, scale1, gate1, shift2, scale2, gate2 = _mods(c, P['w_ada'], P['b_ada'], l)
        z, _ = norm_proj(x, P['norm_mix'][l], scale1, shift1, w['w_in'], nb)
        ql, qp, lat, kr = mla_prep_s(z, w['gq'], w['gkv'], w['wq'], w['wukt'], tabs)
        o_lat = mla_decode(page_table, jnp.transpose(ql, (1, 0, 2)), jnp.transpose(qp, (1, 0, 2)),
                           lat[:, None, :], kr[:, None, :], cache_lat, cache_rope, l)
        a = uv_proj(o_lat.reshape(nb, MLA_HEADS * KV_LORA), w['wuv'])
        m_in = jnp.zeros((nb, LANES), F32).at[:, SM_MI:SM_MI + ML_HEADS].set(s_m[l])
        b, cc, s_new, c_new, n_new, m_new = rec_step(
            z, z.T, w['gla_wa'], w['gla_ba'], w['gla_wat'], w['gla_bat'], w['ml_brow'], w['gla_norm'], w['ml_norm'],
            s_gla, s_c, l, s_n[l].reshape(nb, ML_HEADS * ML_DK), m_in)
        x1, h2 = merge(a, b, cc, z, x, gate1, w['wpa'], w['wpb'], w['wpc'], w['wo'],
                       P['norm_ffn'][l], scale2, shift2, nb)
        x = _peer_block(h2, x1, gate2, w, P['norm_final'], l == len(W) - 1, nb, PEER_TILE_ROWS)
        lat_rows.append(lat[:, None, :])
        rope_rows.append(kr[:, None, :])
        gla_st.append(s_new)
        c_st.append(c_new)
        n_st.append(n_new.reshape(nb, ML_HEADS, ML_DK))
        m_st.append(m_new[:, SM_MI:SM_MI + ML_HEADS])
    return x, tuple(jnp.stack(r) for r in (lat_rows, rope_rows, gla_st, c_st, n_st, m_st))


def kernel(x_prompt, x_sample, cache_kv_latent, cache_k_rope, state_gla, state_mlstm_C, state_mlstm_n, state_mlstm_m, page_table, c_prompt, c_sample, w_ada, b_ada, norm_mix, norm_ffn, norm_final, w_in, mla_q_norm, mla_kv_norm, w_uq, w_uk, w_uv, w_gla_a, b_gla_a, gla_norm, ml_b_i, ml_b_f, ml_norm, w_proj_a, w_proj_b, w_proj_c, w_o, peer_wq, peer_keys, peer_u, peer_v):
    P = {'w_ada': w_ada, 'b_ada': b_ada, 'norm_mix': norm_mix, 'norm_ffn': norm_ffn, 'norm_final': norm_final,
         'w_in': w_in, 'mla_q_norm': mla_q_norm, 'mla_kv_norm': mla_kv_norm, 'w_uq': w_uq, 'w_uk': w_uk,
         'w_uv': w_uv, 'w_gla_a': w_gla_a, 'b_gla_a': b_gla_a, 'gla_norm': gla_norm, 'ml_b_i': ml_b_i,
         'ml_b_f': ml_b_f, 'ml_norm': ml_norm, 'w_proj_a': w_proj_a, 'w_proj_b': w_proj_b,
         'w_proj_c': w_proj_c, 'w_o': w_o, 'peer_wq': peer_wq, 'peer_keys': peer_keys, 'peer_u': peer_u,
         'peer_v': peer_v}
    depth = w_in.shape[0]
    P['peer_u_b'] = peer_u.astype(BF16)
    P['peer_vt_b'] = jnp.swapaxes(peer_v, 1, 2).astype(BF16)
    W = [_layer_weights(l, P) for l in range(depth)]
    bp, sp, d = x_prompt.shape
    assert bp == 1 and x_sample.shape[1] == 1
    nb = x_sample.shape[0]
    y_p, (lat_p, rope_p, gla_p, c_p, n_p, m_p) = _trunk_prompt(x_prompt.reshape(sp, d), c_prompt, P, W)
    y_s, (lat_s, rope_s, gla_s, c_s, n_s, m_s) = _trunk_sample(
        x_sample.reshape(nb, d), c_sample, P, W, cache_kv_latent, cache_k_rope, page_table,
        state_gla, state_mlstm_C, state_mlstm_n, state_mlstm_m)
    return (y_p.reshape(bp, sp, d), y_s.reshape(nb, 1, d),
            lat_p[:, None], rope_p[:, None], gla_p[:, None], c_p[:, None], n_p[:, None], m_p[:, None],
            lat_s, rope_s, gla_s, c_s, n_s, m_s)
```
